```python
import jax, jax.numpy as jnp
from jax import lax
import numpy as np

D_MODEL = 2048
BATCH = 4
SEQ = 2048
DEPTH = 1
DEC_BATCH = 32
DEC_SEQ = 8
PAST_LEN = 16384
PAGE_SIZE = 128

HEAD_DIM = 64
ATTN_WIDTH = D_MODEL // 2
N_HEADS = ATTN_WIDTH // HEAD_DIM
CONV_CH = D_MODEL - ATTN_WIDTH
MIX_WIDTH = ATTN_WIDTH + CONV_CH
IN_WIDTH = 3 * ATTN_WIDTH + 2 * CONV_CH
CONV_WIDTH = 31
CONV_STATE = CONV_WIDTH - 1
DILATED = ((128, 1), (512, 4), (2048, 16))
MAX_WINDOW = 2048
Q_BLOCK = 128
D_FF = 5632
ROPE_THETA = 10000.0
EPS = 1e-6
FFN_RES = 0.5

kernel_name = "hymba_dilated_attn_conformer_conv_macaron"


def rmsnorm(x, g):
    xf = x.astype(jnp.float32)
    y = xf * lax.rsqrt(jnp.mean(xf * xf, axis=-1, keepdims=True) + EPS)
    return (y * g.astype(jnp.float32)).astype(x.dtype)


def layernorm(x, g, b):
    xf = x.astype(jnp.float32)
    mu = jnp.mean(xf, axis=-1, keepdims=True)
    xc = xf - mu
    y = xc * lax.rsqrt(jnp.mean(xc * xc, axis=-1, keepdims=True) + EPS)
    return (y * g.astype(jnp.float32) + b.astype(jnp.float32)).astype(x.dtype)


def rope(x, pos):
    half = HEAD_DIM // 2
    inv = ROPE_THETA ** (-jnp.arange(half, dtype=jnp.float32) / half)
    ang = pos.astype(jnp.float32)[:, None] * inv[None, :]
    cos = jnp.cos(ang)[:, None, :]
    sin = jnp.sin(ang)[:, None, :]
    x1 = x[..., :half].astype(jnp.float32)
    x2 = x[..., half:].astype(jnp.float32)
    out = jnp.concatenate([x1 * cos - x2 * sin, x2 * cos + x1 * sin], axis=-1)
    return out.astype(x.dtype)


def ffn_half(x, g, wg, wu, wd):
    h = rmsnorm(x, g)
    return x + FFN_RES * ((jax.nn.silu(h @ wg) * (h @ wu)) @ wd)


def mixer_inputs(h, w_in, g_q, g_k, pos):
    z = h @ w_in
    q, k, v, c = jnp.split(z, [ATTN_WIDTH, 2 * ATTN_WIDTH, 3 * ATTN_WIDTH], axis=-1)
    shp = h.shape[:-1] + (N_HEADS, HEAD_DIM)
    q = rope(rmsnorm(q.reshape(shp), g_q), pos)
    k = rope(rmsnorm(k.reshape(shp), g_k), pos)
    v = v.reshape(shp)
    a, b = jnp.split(c, 2, axis=-1)
    u = a * jax.nn.sigmoid(b)
    return q, k, v, u


def dilated_prompt(q, k, v, d, span):
    B, S, H, Dh = q.shape
    L = S // d
    nb = -(-L // Q_BLOCK)
    Lp = nb * Q_BLOCK

    def by_residue(x):
        return x.reshape(B, L, d, H, Dh).transpose(0, 2, 1, 3, 4)

    qr, kr, vr = by_residue(q), by_residue(k), by_residue(v)
    kpad = ((0, 0), (0, 0), (Q_BLOCK, Lp - L), (0, 0), (0, 0))
    kb = jnp.pad(kr, kpad).reshape(B, d, nb + 1, Q_BLOCK, H, Dh)
    vb = jnp.pad(vr, kpad).reshape(B, d, nb + 1, Q_BLOCK, H, Dh)
    qb = jnp.pad(qr, ((0, 0), (0, 0), (0, Lp - L), (0, 0), (0, 0))).reshape(B, d, nb, Q_BLOCK, H, Dh)
    kband = jnp.concatenate([kb[:, :, :-1], kb[:, :, 1:]], axis=3)
    vband = jnp.concatenate([vb[:, :, :-1], vb[:, :, 1:]], axis=3)
    s = jnp.einsum('brnqhd,brnkhd->brnhqk', qb, kband,
                   preferred_element_type=jnp.float32) * (HEAD_DIM ** -0.5)
    qi = jnp.arange(Q_BLOCK)[:, None]
    ki = jnp.arange(2 * Q_BLOCK)[None, :]
    rel = qi + Q_BLOCK - ki
    key_m = jnp.arange(nb)[:, None, None] * Q_BLOCK - Q_BLOCK + ki
    mask = (rel >= 0) & (rel <= span) & (key_m >= 0)
    s = jnp.where(mask[:, None], s, -jnp.inf)
    lse = jax.nn.logsumexp(s, axis=-1)
    p = jnp.exp(s - lse[..., None])
    o = jnp.einsum('brnhqk,brnkhd->brnqhd', p, vband.astype(jnp.float32))
    o = o.reshape(B, d, Lp, H, Dh)[:, :, :L].transpose(0, 2, 1, 3, 4).reshape(B, S, H, Dh)
    lse = lse.transpose(0, 1, 2, 4, 3).reshape(B, d, Lp, H)[:, :, :L]
    lse = lse.transpose(0, 2, 1, 3).reshape(B, S, H)
    return o, lse


def dilated_sample(q, k_all, v_all, buf_len, d, span):
    T = q.shape[1]
    j = jnp.arange(span + 1)
    idx = buf_len + jnp.arange(T)[:, None] - d * j[None, :]
    valid = idx >= 0
    idxc = jnp.maximum(idx, 0)
    kg = k_all[:, idxc]
    vg = v_all[:, idxc]
    s = jnp.einsum('bthd,btjhd->bthj', q, kg,
                   preferred_element_type=jnp.float32) * (HEAD_DIM ** -0.5)
    s = jnp.where(valid[:, None, :], s, -jnp.inf)
    lse = jax.nn.logsumexp(s, axis=-1)
    p = jnp.exp(s - lse[..., None])
    o = jnp.einsum('bthj,btjhd->bthd', p, vg.astype(jnp.float32))
    return o, lse


def dilated_mixture(parts):
    o = jnp.stack([p[0] for p in parts])
    lse = jnp.stack([p[1] for p in parts])
    w = jax.nn.softmax(lse, axis=0)
    return jnp.einsum('pbsh,pbshd->bshd', w, o)


def conv_branch(u_ext, dw_w, dw_b, ln_g, ln_b):
    y = lax.conv_general_dilated(u_ext, dw_w.astype(u_ext.dtype)[:, None, :],
                                 window_strides=(1,), padding='VALID',
                                 dimension_numbers=('NWC', 'WIO', 'NWC'),
                                 feature_group_count=CONV_CH)
    return jax.nn.silu(layernorm(y + dw_b, ln_g, ln_b))


def mix_out(attn, conv, w_out):
    B, S = conv.shape[:2]
    cat = jnp.concatenate([attn.astype(conv.dtype).reshape(B, S, ATTN_WIDTH), conv], axis=-1)
    return cat @ w_out


def setup_inputs(seed: int = 0) -> dict:
    key = jax.random.key(seed)
    ks = jax.random.split(key, 24)
    f32 = jnp.float32
    kv_buf = min(MAX_WINDOW, PAST_LEN)
    nrm = lambda k, shp, sc: jax.random.normal(k, shp, f32) * sc
    gain = lambda k, shp: 1.0 + 0.02 * jax.random.normal(k, shp, f32)
    return {
        "x_prompt": nrm(ks[0], (BATCH, SEQ, D_MODEL), 1.0),
        "x_sample": nrm(ks[1], (DEC_BATCH, DEC_SEQ, D_MODEL), 1.0),
        "cache_k": nrm(ks[2], (DEPTH, DEC_BATCH, kv_buf, N_HEADS, HEAD_DIM), 1.0),
        "cache_v": nrm(ks[3], (DEPTH, DEC_BATCH, kv_buf, N_HEADS, HEAD_DIM), 1.0),
        "state_conv": nrm(ks[4], (DEPTH, DEC_BATCH, CONV_STATE, CONV_CH), 0.5),
        "ln_ffn1": gain(ks[5], (DEPTH, D_MODEL)),
        "ffn1_w_gate": nrm(ks[6], (DEPTH, D_MODEL, D_FF), D_MODEL ** -0.5),
        "ffn1_w_up": nrm(ks[7], (DEPTH, D_MODEL, D_FF), D_MODEL ** -0.5),
        "ffn1_w_down": nrm(ks[8], (DEPTH, D_FF, D_MODEL), D_FF ** -0.5),
        "ln_mix": gain(ks[9], (DEPTH, D_MODEL)),
        "w_in": nrm(ks[10], (DEPTH, D_MODEL, IN_WIDTH), D_MODEL ** -0.5),
        "q_norm": gain(ks[11], (DEPTH, HEAD_DIM)),
        "k_norm": gain(ks[12], (DEPTH, HEAD_DIM)),
        "conv_dw_w": nrm(ks[13], (DEPTH, CONV_WIDTH, CONV_CH), CONV_WIDTH ** -0.5),
        "conv_dw_b": nrm(ks[14], (DEPTH, CONV_CH), 0.02),
        "conv_ln_g": gain(ks[15], (DEPTH, CONV_CH)),
        "conv_ln_b": nrm(ks[16], (DEPTH, CONV_CH), 0.02),
        "w_out": nrm(ks[17], (DEPTH, MIX_WIDTH, D_MODEL), MIX_WIDTH ** -0.5),
        "ln_ffn2": gain(ks[18], (DEPTH, D_MODEL)),
        "ffn2_w_gate": nrm(ks[19], (DEPTH, D_MODEL, D_FF), D_MODEL ** -0.5),
        "ffn2_w_up": nrm(ks[20], (DEPTH, D_MODEL, D_FF), D_MODEL ** -0.5),
        "ffn2_w_down": nrm(ks[21], (DEPTH, D_FF, D_MODEL), D_FF ** -0.5),
    }


def reference(x_prompt, x_sample, cache_k, cache_v, state_conv, ln_ffn1, ffn1_w_gate,
              ffn1_w_up, ffn1_w_down, ln_mix, w_in, q_norm, k_norm, conv_dw_w, conv_dw_b,
              conv_ln_g, conv_ln_b, w_out, ln_ffn2, ffn2_w_gate, ffn2_w_up, ffn2_w_down):
    S = x_prompt.shape[1]
    T = x_sample.shape[1]
    buf_len = cache_k.shape[2]
    pos_p = jnp.arange(S)
    pos_s = PAST_LEN + jnp.arange(T)
    hp, hs = x_prompt, x_sample
    kp_l, vp_l, cp_l, ks_l, vs_l, cs_l = [], [], [], [], [], []
    for l in range(DEPTH):
        hp = ffn_half(hp, ln_ffn1[l], ffn1_w_gate[l], ffn1_w_up[l], ffn1_w_down[l])
        hs = ffn_half(hs, ln_ffn1[l], ffn1_w_gate[l], ffn1_w_up[l], ffn1_w_down[l])

        q, k, v, u = mixer_inputs(rmsnorm(hp, ln_mix[l]), w_in[l], q_norm[l], k_norm[l], pos_p)
        attn = dilated_mixture([dilated_prompt(q, k, v, d, w // d) for (w, d) in DILATED])
        u_ext = jnp.pad(u, ((0, 0), (CONV_STATE, 0), (0, 0)))
        conv = conv_branch(u_ext, conv_dw_w[l], conv_dw_b[l], conv_ln_g[l], conv_ln_b[l])
        hp = hp + mix_out(attn, conv, w_out[l])
        kp_l.append(k[:, -min(MAX_WINDOW, S):])
        vp_l.append(v[:, -min(MAX_WINDOW, S):])
        cp_l.append(u_ext[:, -CONV_STATE:])

        q, k, v, u = mixer_inputs(rmsnorm(hs, ln_mix[l]), w_in[l], q_norm[l], k_norm[l], pos_s)
        k_all = jnp.concatenate([cache_k[l].astype(k.dtype), k], axis=1)
        v_all = jnp.concatenate([cache_v[l].astype(v.dtype), v], axis=1)
        attn = dilated_mixture([dilated_sample(q, k_all, v_all, buf_len, d, w // d)
                                for (w, d) in DILATED])
        u_ext = jnp.concatenate([state_conv[l].astype(u.dtype), u], axis=1)
        conv = conv_branch(u_ext, conv_dw_w[l], conv_dw_b[l], conv_ln_g[l], conv_ln_b[l])
        hs = hs + mix_out(attn, conv, w_out[l])
        ks_l.append(k)
        vs_l.append(v)
        cs_l.append(u_ext[:, -CONV_STATE:])

        hp = ffn_half(hp, ln_ffn2[l], ffn2_w_gate[l], ffn2_w_up[l], ffn2_w_down[l])
        hs = ffn_half(hs, ln_ffn2[l], ffn2_w_gate[l], ffn2_w_up[l], ffn2_w_down[l])
    return (hp, hs, jnp.stack(kp_l), jnp.stack(vp_l), jnp.stack(cp_l),
            jnp.stack(ks_l), jnp.stack(vs_l), jnp.stack(cs_l))
```

```python
import functools

import numpy as np
import jax
import jax.numpy as jnp
from jax import lax
from jax.experimental import pallas as pl
from jax.experimental.pallas import tpu as pltpu

F32 = jnp.float32
BF16 = jnp.bfloat16

HEAD_DIM = 64
PAST_LEN = 16384
DILATED = ((128, 1), (512, 4), (2048, 16))
Q_BLOCK = 128
ROPE_THETA = 10000.0
EPS = 1e-6
FFN_RES = 0.5
NEG = -1e30

LANES = 128
MXU_DIM = 256
VMEM_LIMIT = 56 * 1024 * 1024

NT_DIMS = (((1,), (1,)), ((), ()))


def _silu(x):
    return x * jax.nn.sigmoid(x)


def _rms_rows(x, gain):
    ms = jnp.mean(x * x, axis=-1, keepdims=True)
    return x * lax.rsqrt(ms + EPS) * gain


def _ffn_body(x_ref, g_ref, wg_ref, wu_ref, wd_ref, o_ref, h_ref):
    @pl.when(pl.program_id(1) == 0)
    def _():
        x = x_ref[...]
        h_ref[...] = _rms_rows(x, g_ref[...]).astype(BF16)
        o_ref[...] = x

    h = h_ref[...]
    g = jnp.dot(h, wg_ref[...], preferred_element_type=F32)
    u = jnp.dot(h, wu_ref[...], preferred_element_type=F32)
    a = (_silu(g) * u * FFN_RES).astype(BF16)
    o_ref[...] += jnp.dot(a, wd_ref[...], preferred_element_type=F32)


def _ffn(x, gain, wg, wu, wd, tm, tf=512):
    t, d = x.shape
    dff = wg.shape[1]
    return pl.pallas_call(
        _ffn_body,
        grid=(t // tm, dff // tf),
        in_specs=[
            pl.BlockSpec((tm, d), lambda i, f: (i, 0)),
            pl.BlockSpec((1, d), lambda i, f: (0, 0)),
            pl.BlockSpec((d, tf), lambda i, f: (0, f)),
            pl.BlockSpec((d, tf), lambda i, f: (0, f)),
            pl.BlockSpec((tf, d), lambda i, f: (f, 0)),
        ],
        out_specs=pl.BlockSpec((tm, d), lambda i, f: (i, 0)),
        out_shape=jax.ShapeDtypeStruct((t, d), F32),
        scratch_shapes=[pltpu.VMEM((tm, d), BF16)],
        compiler_params=pltpu.CompilerParams(
            dimension_semantics=("parallel", "arbitrary"), vmem_limit_bytes=VMEM_LIMIT),
        name="ffn",
    )(x, gain, wg, wu, wd)


def _inproj_body(x_ref, g_ref, w_ref, gq_ref, gk_ref, cos_ref, sin_ref, pm_ref,
                 q_ref, k_ref, v_ref, u_ref, hn_ref, a_ref):
    j = pl.program_id(1)

    @pl.when(j == 0)
    def _():
        hn_ref[...] = _rms_rows(x_ref[...], g_ref[...]).astype(BF16)

    z = jnp.dot(hn_ref[...], w_ref[...], preferred_element_type=F32)
    width = z.shape[1]
    lane = lax.broadcasted_iota(jnp.int32, (z.shape[0], LANES), 1)
    first_half = (lane % HEAD_DIM) < (HEAD_DIM // 2)

    def norm_rope(gain_ref, out_ref, scale):
        zz = z * z
        hi = zz.astype(BF16)
        lo = (zz - hi.astype(F32)).astype(BF16)
        pm = pm_ref[...]
        for c in range(width // MXU_DIM):
            sl = slice(c * MXU_DIM, (c + 1) * MXU_DIM)
            ss = (jnp.dot(hi[:, sl], pm, preferred_element_type=F32)
                  + jnp.dot(lo[:, sl], pm, preferred_element_type=F32))
            zn = z[:, sl] * lax.rsqrt(ss * (1.0 / HEAD_DIM) + EPS) * gain_ref[:, sl]
            for e in range(MXU_DIM // LANES):
                x = zn[:, e * LANES:(e + 1) * LANES]
                partner = jnp.where(first_half,
                                    pltpu.roll(x, LANES - HEAD_DIM // 2, 1),
                                    pltpu.roll(x, HEAD_DIM // 2, 1))
                y = x * cos_ref[...] + partner * sin_ref[...]
                lo_lane = c * MXU_DIM + e * LANES
                out_ref[:, lo_lane:lo_lane + LANES] = y * scale

    @pl.when(j == 0)
    def _():
        norm_rope(gq_ref, q_ref, HEAD_DIM ** -0.5)

    @pl.when(j == 1)
    def _():
        norm_rope(gk_ref, k_ref, 1.0)

    @pl.when(j == 2)
    def _():
        v_ref[...] = z

    @pl.when(j == 3)
    def _():
        a_ref[...] = z

    @pl.when(j == 4)
    def _():
        u_ref[...] = a_ref[...] * jax.nn.sigmoid(z)


def _inproj(x, gain, w, gq, gk, cos_t, sin_t, pm, tm):
    t, d = x.shape
    width = w.shape[1] // 5
    period_blocks = cos_t.shape[0] // tm
    tab_spec = pl.BlockSpec((tm, LANES), lambda i, j: (i % period_blocks, 0))
    out_spec = pl.BlockSpec((tm, width), lambda i, j: (i, 0))
    out_sds = jax.ShapeDtypeStruct((t, width), F32)
    return pl.pallas_call(
        _inproj_body,
        grid=(t // tm, 5),
        in_specs=[
            pl.BlockSpec((tm, d), lambda i, j: (i, 0)),
            pl.BlockSpec((1, d), lambda i, j: (0, 0)),
            pl.BlockSpec((d, width), lambda i, j: (0, j)),
            pl.BlockSpec((1, width), lambda i, j: (0, 0)),
            pl.BlockSpec((1, width), lambda i, j: (0, 0)),
            tab_spec, tab_spec,
            pl.BlockSpec((MXU_DIM, MXU_DIM), lambda i, j: (0, 0)),
        ],
        out_specs=[out_spec] * 4,
        out_shape=[out_sds] * 4,
        scratch_shapes=[pltpu.VMEM((tm, d), BF16), pltpu.VMEM((tm, width), F32)],
        compiler_params=pltpu.CompilerParams(
            dimension_semantics=("parallel", "arbitrary"), vmem_limit_bytes=VMEM_LIMIT),
        name="inproj",
    )(x, gain, w, gq, gk, cos_t, sin_t, pm)


def _pattn_body(q_ref, k_ref, v_ref, o_ref, qd, kd, vd, od, md, ld, ot, mt, lt, *, dils, span):
    s_len = q_ref.shape[0]
    nblk = s_len // Q_BLOCK

    for p, d in enumerate(dils):
        cls = s_len // d
        for r in range(d):
            rows = pl.ds(r, cls, stride=d) if d > 1 else pl.ds(0, cls)
            dst = pl.ds(r * cls, cls)
            qd[p, dst, :] = q_ref[rows, :].astype(BF16)
            kd[p, dst, :] = k_ref[rows, :].astype(BF16)
            vd[p, dst, :] = v_ref[rows, :].astype(BF16)

    lane = lax.broadcasted_iota(jnp.int32, (Q_BLOCK, LANES), 1)
    head0 = lane < HEAD_DIM
    qi = lax.broadcasted_iota(jnp.int32, (Q_BLOCK, Q_BLOCK), 0)
    ki = lax.broadcasted_iota(jnp.int32, (Q_BLOCK, Q_BLOCK), 1)
    causal = ki <= qi
    band = (qi + Q_BLOCK - ki) <= span

    for p, d in enumerate(dils):
        blocks_per_class = (s_len // d) // Q_BLOCK

        def block(n, carry, p=p, blocks_per_class=blocks_per_class):
            r0 = pl.multiple_of(n * Q_BLOCK, Q_BLOCK)
            q = qd[p, pl.ds(r0, Q_BLOCK), :]
            kc = kd[p, pl.ds(r0, Q_BLOCK), :]
            vc = vd[p, pl.ds(r0, Q_BLOCK), :]
            use_prev = blocks_per_class > 1
            if use_prev:
                rp = pl.multiple_of(jnp.maximum(n - 1, 0) * Q_BLOCK, Q_BLOCK)
                kp = kd[p, pl.ds(rp, Q_BLOCK), :]
                vp = vd[p, pl.ds(rp, Q_BLOCK), :]
                prev_ok = band & ((n % blocks_per_class) != 0)
            res = []
            for h in range(2):
                qh = jnp.where(head0 if h == 0 else ~head0, q, jnp.zeros_like(q))
                sc = lax.dot_general(qh, kc, NT_DIMS, preferred_element_type=F32)
                sc = jnp.where(causal, sc, NEG)
                m = jnp.max(sc, axis=1, keepdims=True)
                if use_prev:
                    sp = lax.dot_general(qh, kp, NT_DIMS, preferred_element_type=F32)
                    sp = jnp.where(prev_ok, sp, NEG)
                    m = jnp.maximum(m, jnp.max(sp, axis=1, keepdims=True))
                pc = jnp.exp(sc - m)
                l = jnp.sum(pc, axis=1, keepdims=True)
                o = jnp.dot(pc.astype(BF16), vc, preferred_element_type=F32)
                if use_prev:
                    pp = jnp.exp(sp - m)
                    l = l + jnp.sum(pp, axis=1, keepdims=True)
                    o = o + jnp.dot(pp.astype(BF16), vp, preferred_element_type=F32)
                res.append((m, l, o))
            (m0, l0, o0), (m1, l1, o1) = res
            od[p, pl.ds(r0, Q_BLOCK), :] = jnp.where(head0, o0, o1)
            md[p, pl.ds(r0, Q_BLOCK), :] = jnp.where(head0, m0, m1)
            ld[p, pl.ds(r0, Q_BLOCK), :] = jnp.where(head0, l0, l1)
            return carry

        lax.fori_loop(0, nblk, block, 0)

    for p, d in enumerate(dils):
        if d == 1:
            continue
        cls = s_len // d
        for r in range(d):
            src = pl.ds(r * cls, cls)
            dst = pl.ds(r, cls, stride=d)
            ot[p - 1, dst, :] = od[p, src, :]
            mt[p - 1, dst, :] = md[p, src, :]
            lt[p - 1, dst, :] = ld[p, src, :]

    def merge(c, carry):
        rows = pl.ds(pl.multiple_of(c * Q_BLOCK, Q_BLOCK), Q_BLOCK)
        parts = []
        for p, d in enumerate(dils):
            if d == 1:
                parts.append((md[p, rows, :], ld[p, rows, :], od[p, rows, :]))
            else:
                parts.append((mt[p - 1, rows, :], lt[p - 1, rows, :], ot[p - 1, rows, :]))
        m = parts[0][0]
        for mp, _, _ in parts[1:]:
            m = jnp.maximum(m, mp)
        num = jnp.zeros((Q_BLOCK, LANES), F32)
        den = jnp.zeros((Q_BLOCK, LANES), F32)
        for mp, lp, op in parts:
            e = jnp.exp(mp - m)
            num = num + op * e
            den = den + lp * e
        o_ref[rows, :] = num / den
        return carry

    lax.fori_loop(0, nblk, merge, 0)


def _prompt_attention(q, k, v, batch, s_len):
    t, width = q.shape
    dils = tuple(d for _, d in DILATED)
    spans = {w // d for w, d in DILATED}
    assert len(spans) == 1 and dils[0] == 1
    span = spans.pop()
    assert span == Q_BLOCK and all(s_len % (d * Q_BLOCK) == 0 for d in dils)
    npat = len(dils)
    spec = pl.BlockSpec((s_len, LANES), lambda b, hp: (b, hp))
    res_bf = pltpu.VMEM((npat, s_len, LANES), BF16)
    res_f = pltpu.VMEM((npat, s_len, LANES), F32)
    tok_f = pltpu.VMEM((npat - 1, s_len, LANES), F32)
    return pl.pallas_call(
        functools.partial(_pattn_body, dils=dils, span=span),
        grid=(batch, width // LANES),
        in_specs=[spec, spec, spec],
        out_specs=spec,
        out_shape=jax.ShapeDtypeStruct((t, width), F32),
        scratch_shapes=[res_bf, res_bf, res_bf, res_f, res_f, res_f, tok_f, tok_f, tok_f],
        compiler_params=pltpu.CompilerParams(
            dimension_semantics=("parallel", "parallel"), vmem_limit_bytes=VMEM_LIMIT),
        name="prompt_attn",
    )(q, k, v)


def _sattn_body(q_ref, kn_ref, vn_ref, kt_ref, vt_ref, mc_ref, mn_ref, o_ref):
    t_len, width = q_ref.shape
    n_heads = width // HEAD_DIM
    rows = n_heads * t_len
    q = q_ref[...]
    q_rep = jnp.concatenate([q] * n_heads, axis=0)
    row_h = lax.broadcasted_iota(jnp.int32, (rows, width), 0) // t_len
    lane_h = lax.broadcasted_iota(jnp.int32, (rows, width), 1) // HEAD_DIM
    own = row_h == lane_h
    q_exp = jnp.where(own, q_rep, 0.0).astype(BF16)

    pad = jnp.zeros((mn_ref.shape[1] - t_len, width), F32)
    kn = jnp.concatenate([kn_ref[...], pad], axis=0).astype(BF16)
    vn = jnp.concatenate([vn_ref[...], pad], axis=0).astype(BF16)

    s = jnp.dot(q_exp, kt_ref[0].astype(BF16), preferred_element_type=F32)
    sn = lax.dot_general(q_exp, kn, NT_DIMS, preferred_element_type=F32)
    mc = mc_ref[...]
    mn = mn_ref[...]
    s = jnp.where(mc > 0, s, NEG)
    sn = jnp.where(mn > 0, sn, NEG)
    m = jnp.maximum(jnp.max(s, axis=1, keepdims=True), jnp.max(sn, axis=1, keepdims=True))
    p = mc * jnp.exp(s - m)
    pn = mn * jnp.exp(sn - m)
    l = jnp.sum(p, axis=1, keepdims=True) + jnp.sum(pn, axis=1, keepdims=True)
    o = lax.dot_general(p.astype(BF16), vt_ref[0].astype(BF16), NT_DIMS,
                        preferred_element_type=F32)
    o = o + jnp.dot(pn.astype(BF16), vn, preferred_element_type=F32)
    o = jnp.where(own, o / l, 0.0)
    out = o[0:t_len]
    for h in range(1, n_heads):
        out = out + o[h * t_len:(h + 1) * t_len]
    o_ref[...] = out


def _sample_mult(t_len, buf_len):
    mult = np.zeros((t_len, buf_len + t_len), np.float32)
    for w, d in DILATED:
        for t in range(t_len):
            for j in range(w // d + 1):
                idx = buf_len + t - d * j
                if idx >= 0:
                    mult[t, idx] += 1.0
    return mult


def _sample_attention(q, kn, vn, kt, vt, t_len):
    t, width = q.shape
    batch, _, buf_len = kt.shape
    n_heads = width // HEAD_DIM
    mult = _sample_mult(t_len, buf_len)
    mc = np.tile(mult[:, :buf_len], (n_heads, 1))
    mn = np.zeros((n_heads * t_len, LANES), np.float32)
    mn[:, :t_len] = np.tile(mult[:, buf_len:], (n_heads, 1))
    row_spec = pl.BlockSpec((t_len, width), lambda b: (b, 0))
    cache_spec = pl.BlockSpec((1, width, buf_len), lambda b: (b, 0, 0))
    return pl.pallas_call(
        _sattn_body,
        grid=(batch,),
        in_specs=[row_spec, row_spec, row_spec, cache_spec, cache_spec,
                  pl.BlockSpec(mc.shape, lambda b: (0, 0)),
                  pl.BlockSpec(mn.shape, lambda b: (0, 0))],
        out_specs=row_spec,
        out_shape=jax.ShapeDtypeStruct((t, width), F32),
        compiler_params=pltpu.CompilerParams(
            dimension_semantics=("parallel",), vmem_limit_bytes=VMEM_LIMIT),
        name="sample_attn",
    )(q, kn, vn, kt, vt, jnp.asarray(mc), jnp.asarray(mn))


def _ln_swish(y, g, b):
    mu = jnp.mean(y, axis=-1, keepdims=True)
    yc = y - mu
    var = jnp.mean(yc * yc, axis=-1, keepdims=True)
    return _silu(yc * lax.rsqrt(var + EPS) * g + b)


CONV_ROWS = 32


def _conv_out_prompt_body(ucur_ref, uprev_ref, attn_ref, h_ref, w_ref, b_ref, g_ref, beta_ref,
                          wo_ref, o_ref, ext_ref, conv_ref, *, tiles_per_seq):
    tm, ch = ucur_ref.shape
    halo = uprev_ref.shape[0]
    taps = w_ref.shape[0]
    first = (pl.program_id(0) % tiles_per_seq) == 0
    ext_ref[0:halo, :] = jnp.where(first, 0.0, uprev_ref[...])
    ext_ref[halo:halo + tm, :] = ucur_ref[...]
    off = halo - (taps - 1)
    for c in range(tm // CONV_ROWS):
        acc = jnp.zeros((CONV_ROWS, ch), F32)
        for w in range(taps):
            acc = acc + ext_ref[pl.ds(c * CONV_ROWS + off + w, CONV_ROWS), :] * w_ref[w:w + 1, :]
        act = _ln_swish(acc + b_ref[...], g_ref[...], beta_ref[...])
        conv_ref[c * CONV_ROWS:(c + 1) * CONV_ROWS, :] = act.astype(BF16)
    aw = attn_ref.shape[1]
    o_ref[...] = (h_ref[...]
                  + jnp.dot(attn_ref[...].astype(BF16), wo_ref[0:aw, :], preferred_element_type=F32)
                  + jnp.dot(conv_ref[...], wo_ref[aw:aw + ch, :], preferred_element_type=F32))


def _conv_out_prompt(u, attn, h, dw_w, dw_b, ln_g, ln_b, wo, s_len, tm=256, halo=32):
    t, ch = u.shape
    d = h.shape[1]
    aw = attn.shape[1]
    taps = dw_w.shape[0]
    assert taps - 1 <= halo and s_len % tm == 0 and tm % halo == 0
    hb = tm // halo
    vec = pl.BlockSpec((1, ch), lambda i: (0, 0))
    return pl.pallas_call(
        functools.partial(_conv_out_prompt_body, tiles_per_seq=s_len // tm),
        grid=(t // tm,),
        in_specs=[
            pl.BlockSpec((tm, ch), lambda i: (i, 0)),
            pl.BlockSpec((halo, ch), lambda i: (jnp.maximum(i * hb - 1, 0), 0)),
            pl.BlockSpec((tm, aw), lambda i: (i, 0)),
            pl.BlockSpec((tm, d), lambda i: (i, 0)),
            pl.BlockSpec((taps, ch), lambda i: (0, 0)),
            vec, vec, vec,
            pl.BlockSpec((aw + ch, d), lambda i: (0, 0)),
        ],
        out_specs=pl.BlockSpec((tm, d), lambda i: (i, 0)),
        out_shape=jax.ShapeDtypeStruct((t, d), F32),
        scratch_shapes=[pltpu.VMEM((halo + tm, ch), F32), pltpu.VMEM((tm, ch), BF16)],
        compiler_params=pltpu.CompilerParams(
            dimension_semantics=("parallel",), vmem_limit_bytes=VMEM_LIMIT),
        name="conv_out_prompt",
    )(u, u, attn, h, dw_w, dw_b, ln_g, ln_b, wo)


def _conv_out_sample_body(u_ref, st_ref, attn_ref, h_ref, w_ref, b_ref, g_ref, beta_ref, wo_ref,
                          o_ref, ns_ref, ext_ref, conv_ref, at_ref):
    n_state, batch, ch = st_ref.shape
    t_len = u_ref.shape[0]
    taps = w_ref.shape[0]
    ext_ref[0:n_state] = st_ref[...]
    ext_ref[n_state:n_state + t_len] = u_ref[...]
    ns_ref[...] = ext_ref[t_len:t_len + n_state]
    for t in range(t_len):
        acc = jnp.zeros((batch, ch), F32)
        for w in range(taps):
            acc = acc + ext_ref[t + w] * w_ref[w:w + 1, :]
        act = _ln_swish(acc + b_ref[...], g_ref[...], beta_ref[...])
        conv_ref[t * batch:(t + 1) * batch, :] = act.astype(BF16)
        at_ref[t * batch:(t + 1) * batch, :] = attn_ref[t].astype(BF16)
    aw = attn_ref.shape[2]
    proj = (jnp.dot(at_ref[...], wo_ref[0:aw, :], preferred_element_type=F32)
            + jnp.dot(conv_ref[...], wo_ref[aw:aw + ch, :], preferred_element_type=F32))
    for t in range(t_len):
        o_ref[t] = h_ref[t] + proj[t * batch:(t + 1) * batch, :]


def _conv_out_sample(u_t, state_t, attn_t, h_t, dw_w, dw_b, ln_g, ln_b, wo):
    t_len, batch, ch = u_t.shape
    d = h_t.shape[2]
    n_state = state_t.shape[0]
    assert dw_w.shape[0] == n_state + 1
    return pl.pallas_call(
        _conv_out_sample_body,
        out_shape=[jax.ShapeDtypeStruct((t_len, batch, d), F32),
                   jax.ShapeDtypeStruct((n_state, batch, ch), F32)],
        scratch_shapes=[pltpu.VMEM((n_state + t_len, batch, ch), F32),
                        pltpu.VMEM((t_len * batch, ch), BF16),
                        pltpu.VMEM((t_len * batch, attn_t.shape[2]), BF16)],
        compiler_params=pltpu.CompilerParams(vmem_limit_bytes=VMEM_LIMIT),
        name="conv_out_sample",
    )(u_t, state_t, attn_t, h_t, dw_w, dw_b, ln_g, ln_b, wo)


def _rope_tables(positions):
    half = HEAD_DIM // 2
    inv = ROPE_THETA ** (-np.arange(half, dtype=np.float64) / half)
    ang = np.asarray(positions, np.float64)[:, None] * inv[None, :]
    cos, sin = np.cos(ang), np.sin(ang)
    reps = LANES // HEAD_DIM
    cos_t = np.tile(np.concatenate([cos, cos], axis=1), (1, reps))
    sin_t = np.tile(np.concatenate([-sin, sin], axis=1), (1, reps))
    return jnp.asarray(cos_t, F32), jnp.asarray(sin_t, F32)


def kernel(x_prompt, x_sample, cache_k, cache_v, state_conv, ln_ffn1, ffn1_w_gate, ffn1_w_up,
           ffn1_w_down, ln_mix, w_in, q_norm, k_norm, conv_dw_w, conv_dw_b, conv_ln_g, conv_ln_b,
           w_out, ln_ffn2, ffn2_w_gate, ffn2_w_up, ffn2_w_down):
    batch, s_len, d_model = x_prompt.shape
    dec_batch, t_len, _ = x_sample.shape
    depth, _, buf_len, n_heads, head_dim = cache_k.shape
    assert depth == 1 and head_dim == HEAD_DIM
    width = n_heads * head_dim
    ch = conv_dw_w.shape[2]

    xp = x_prompt.reshape(batch * s_len, d_model)
    xs = x_sample.reshape(dec_batch * t_len, d_model)
    ts = dec_batch * t_len

    bf = lambda a: a[0].astype(BF16)
    wg1, wu1, wd1 = bf(ffn1_w_gate), bf(ffn1_w_up), bf(ffn1_w_down)
    wg2, wu2, wd2 = bf(ffn2_w_gate), bf(ffn2_w_up), bf(ffn2_w_down)
    wi, wo = bf(w_in), bf(w_out)
    reps = width // HEAD_DIM
    gq = jnp.tile(q_norm, (1, reps))
    gk = jnp.tile(k_norm, (1, reps))
    heads_per_tile = MXU_DIM // HEAD_DIM
    pm = jnp.asarray(np.kron(np.eye(heads_per_tile), np.ones((HEAD_DIM, HEAD_DIM))), BF16)

    cos_p, sin_p = _rope_tables(np.arange(s_len))
    cos_s, sin_s = _rope_tables(np.tile(PAST_LEN + np.arange(t_len), dec_batch))

    hp = _ffn(xp, ln_ffn1, wg1, wu1, wd1, tm=512)
    hs = _ffn(xs, ln_ffn1, wg1, wu1, wd1, tm=ts)

    qp, kp, vp, up = _inproj(hp, ln_mix, wi, gq, gk, cos_p, sin_p, pm, tm=512)
    qs, ks, vs, us = _inproj(hs, ln_mix, wi, gq, gk, cos_s, sin_s, pm, tm=ts)

    attn_p = _prompt_attention(qp, kp, vp, batch, s_len)
    kt = jnp.transpose(cache_k[0], (0, 2, 3, 1)).reshape(dec_batch, width, buf_len)
    vt = jnp.transpose(cache_v[0], (0, 2, 3, 1)).reshape(dec_batch, width, buf_len)
    attn_s = _sample_attention(qs, ks, vs, kt, vt, t_len)

    hp = _conv_out_prompt(up, attn_p, hp, conv_dw_w[0], conv_dw_b, conv_ln_g, conv_ln_b, wo, s_len)
    state_t = jnp.transpose(state_conv[0], (1, 0, 2))
    time_major = lambda a: jnp.transpose(a.reshape(dec_batch, t_len, a.shape[-1]), (1, 0, 2))
    hs_t, new_state_t = _conv_out_sample(time_major(us), state_t, time_major(attn_s),
                                         time_major(hs), conv_dw_w[0], conv_dw_b,
                                         conv_ln_g, conv_ln_b, wo)
    hs = jnp.transpose(hs_t, (1, 0, 2)).reshape(ts, d_model)

    yp = _ffn(hp, ln_ffn2, wg2, wu2, wd2, tm=512)
    ys = _ffn(hs, ln_ffn2, wg2, wu2, wd2, tm=ts)

    n_state = state_conv.shape[2]
    kv_p = (1, batch, s_len, n_heads, head_dim)
    kv_s = (1, dec_batch, t_len, n_heads, head_dim)
    return (yp.reshape(batch, s_len, d_model),
            ys.reshape(dec_batch, t_len, d_model),
            kp.reshape(kv_p), vp.reshape(kv_p),
            up.reshape(batch, s_len, ch)[:, s_len - n_state:][None],
            ks.reshape(kv_s), vs.reshape(kv_s),
            jnp.transpose(new_state_t, (1, 0, 2))[None])
```

```python
import functools

import numpy as np
import jax
import jax.numpy as jnp
from jax import lax
from jax.experimental import pallas as pl
from jax.experimental.pallas import tpu as pltpu

F32 = jnp.float32
BF16 = jnp.bfloat16

HEAD_DIM = 64
PAST_LEN = 16384
DILATED = ((128, 1), (512, 4), (2048, 16))
Q_BLOCK = 128
ROPE_THETA = 10000.0
EPS = 1e-6
FFN_RES = 0.5
NEG = -1e30

LANES = 128
MXU_DIM = 256
VMEM_LIMIT = 56 * 1024 * 1024

NT_DIMS = (((1,), (1,)), ((), ()))


def _silu(x):
    return x * jax.nn.sigmoid(x)


def _rms_rows(x, gain):
    ms = jnp.mean(x * x, axis=-1, keepdims=True)
    return x * lax.rsqrt(ms + EPS) * gain


def _ffn_body(x_ref, g_ref, wg_ref, wu_ref, wd_ref, o_ref, h_ref):
    @pl.when(pl.program_id(1) == 0)
    def _():
        x = x_ref[...]
        h_ref[...] = _rms_rows(x, g_ref[...]).astype(BF16)
        o_ref[...] = x

    h = h_ref[...]
    g = jnp.dot(h, wg_ref[...], preferred_element_type=F32)
    u = jnp.dot(h, wu_ref[...], preferred_element_type=F32)
    a = (_silu(g) * u * FFN_RES).astype(BF16)
    o_ref[...] += jnp.dot(a, wd_ref[...], preferred_element_type=F32)


def _ffn(x, gain, wg, wu, wd, tm, tf=512):
    t, d = x.shape
    dff = wg.shape[1]
    return pl.pallas_call(
        _ffn_body,
        grid=(t // tm, dff // tf),
        in_specs=[
            pl.BlockSpec((tm, d), lambda i, f: (i, 0)),
            pl.BlockSpec((1, d), lambda i, f: (0, 0)),
            pl.BlockSpec((d, tf), lambda i, f: (0, f)),
            pl.BlockSpec((d, tf), lambda i, f: (0, f)),
            pl.BlockSpec((tf, d), lambda i, f: (f, 0)),
        ],
        out_specs=pl.BlockSpec((tm, d), lambda i, f: (i, 0)),
        out_shape=jax.ShapeDtypeStruct((t, d), F32),
        scratch_shapes=[pltpu.VMEM((tm, d), BF16)],
        compiler_params=pltpu.CompilerParams(
            dimension_semantics=("parallel", "arbitrary"), vmem_limit_bytes=VMEM_LIMIT),
        name="ffn",
    )(x, gain, wg, wu, wd)


def _inproj_body(x_ref, g_ref, w_ref, gq_ref, gk_ref, cos_ref, sin_ref, pm_ref,
                 q_ref, k_ref, v_ref, u_ref, hn_ref, a_ref):
    j = pl.program_id(1)

    @pl.when(j == 0)
    def _():
        hn_ref[...] = _rms_rows(x_ref[...], g_ref[...]).astype(BF16)

    z = jnp.dot(hn_ref[...], w_ref[...], preferred_element_type=F32)
    width = z.shape[1]
    lane = lax.broadcasted_iota(jnp.int32, (z.shape[0], LANES), 1)
    first_half = (lane % HEAD_DIM) < (HEAD_DIM // 2)

    def norm_rope(gain_ref, out_ref, scale):
        zz = z * z
        hi = zz.astype(BF16)
        lo = (zz - hi.astype(F32)).astype(BF16)
        pm = pm_ref[...]
        for c in range(width // MXU_DIM):
            sl = slice(c * MXU_DIM, (c + 1) * MXU_DIM)
            ss = (jnp.dot(hi[:, sl], pm, preferred_element_type=F32)
                  + jnp.dot(lo[:, sl], pm, preferred_element_type=F32))
            zn = z[:, sl] * lax.rsqrt(ss * (1.0 / HEAD_DIM) + EPS) * gain_ref[:, sl]
            for e in range(MXU_DIM // LANES):
                x = zn[:, e * LANES:(e + 1) * LANES]
                partner = jnp.where(first_half,
                                    pltpu.roll(x, LANES - HEAD_DIM // 2, 1),
                                    pltpu.roll(x, HEAD_DIM // 2, 1))
                y = x * cos_ref[...] + partner * sin_ref[...]
                lo_lane = c * MXU_DIM + e * LANES
                out_ref[:, lo_lane:lo_lane + LANES] = y * scale

    @pl.when(j == 0)
    def _():
        norm_rope(gq_ref, q_ref, HEAD_DIM ** -0.5)

    @pl.when(j == 1)
    def _():
        norm_rope(gk_ref, k_ref, 1.0)

    @pl.when(j == 2)
    def _():
        v_ref[...] = z

    @pl.when(j == 3)
    def _():
        a_ref[...] = z

    @pl.when(j == 4)
    def _():
        u_ref[...] = a_ref[...] * jax.nn.sigmoid(z)


def _inproj(x, gain, w, gq, gk, cos_t, sin_t, pm, tm):
    t, d = x.shape
    width = w.shape[1] // 5
    period_blocks = cos_t.shape[0] // tm
    tab_spec = pl.BlockSpec((tm, LANES), lambda i, j: (i % period_blocks, 0))
    out_spec = pl.BlockSpec((tm, width), lambda i, j: (i, 0))
    out_sds = jax.ShapeDtypeStruct((t, width), F32)
    return pl.pallas_call(
        _inproj_body,
        grid=(t // tm, 5),
        in_specs=[
            pl.BlockSpec((tm, d), lambda i, j: (i, 0)),
            pl.BlockSpec((1, d), lambda i, j: (0, 0)),
            pl.BlockSpec((d, width), lambda i, j: (0, j)),
            pl.BlockSpec((1, width), lambda i, j: (0, 0)),
            pl.BlockSpec((1, width), lambda i, j: (0, 0)),
            tab_spec, tab_spec,
            pl.BlockSpec((MXU_DIM, MXU_DIM), lambda i, j: (0, 0)),
        ],
        out_specs=[out_spec] * 4,
        out_shape=[out_sds] * 4,
        scratch_shapes=[pltpu.VMEM((tm, d), BF16), pltpu.VMEM((tm, width), F32)],
        compiler_params=pltpu.CompilerParams(
            dimension_semantics=("parallel", "arbitrary"), vmem_limit_bytes=VMEM_LIMIT),
        name="inproj",
    )(x, gain, w, gq, gk, cos_t, sin_t, pm)


def _pattn_body(q_ref, k_ref, v_ref, o_ref, qd, kt, vd, on, ls, ont, lst, *, dils, span):
    s_len = q_ref.shape[0]
    nblk = s_len // Q_BLOCK

    for p, d in enumerate(dils):
        cls = s_len // d
        for r in range(d):
            rows = pl.ds(r, cls, stride=d) if d > 1 else pl.ds(0, cls)
            dst = pl.ds(r * cls, cls)
            qd[p, dst, :] = q_ref[rows, :].astype(BF16)
            vd[p, dst, :] = v_ref[rows, :].astype(BF16)
            for j in range(cls // Q_BLOCK):
                blk = (pl.ds(r + j * Q_BLOCK * d, Q_BLOCK, stride=d) if d > 1
                       else pl.ds(j * Q_BLOCK, Q_BLOCK))
                kt[p, (r * cls) // Q_BLOCK + j] = k_ref[blk, :].T.astype(BF16)

    lane = lax.broadcasted_iota(jnp.int32, (Q_BLOCK, LANES), 1)
    head0 = lane < HEAD_DIM
    qi = lax.broadcasted_iota(jnp.int32, (Q_BLOCK, Q_BLOCK), 0)
    ki = lax.broadcasted_iota(jnp.int32, (Q_BLOCK, Q_BLOCK), 1)
    causal = ki <= qi
    band = (qi + Q_BLOCK - ki) <= span
    one = jnp.ones((), BF16)
    zero = jnp.zeros((), BF16)

    for p, d in enumerate(dils):
        blocks_per_class = (s_len // d) // Q_BLOCK
        use_prev = blocks_per_class > 1

        def block(n, carry, p=p, blocks_per_class=blocks_per_class, use_prev=use_prev):
            rows = pl.ds(pl.multiple_of(n * Q_BLOCK, Q_BLOCK), Q_BLOCK)
            q = qd[p, rows, :]
            vc = vd[p, rows, :]
            ktc = kt[p, n]
            if use_prev:
                n_prev = jnp.maximum(n - 1, 0)
                ktp = kt[p, n_prev]
                vp = vd[p, pl.ds(pl.multiple_of(n_prev * Q_BLOCK, Q_BLOCK), Q_BLOCK), :]
                prev_ok = band & ((n % blocks_per_class) != 0)
            res = []
            for h in range(2):
                own = head0 if h == 0 else ~head0
                qh = jnp.where(own, q, zero)
                sc = jnp.where(causal, jnp.dot(qh, ktc, preferred_element_type=F32), NEG)
                if use_prev:
                    sp = jnp.where(prev_ok, jnp.dot(qh, ktp, preferred_element_type=F32), NEG)
                    m = jnp.max(jnp.maximum(sc, sp), axis=1, keepdims=True)
                else:
                    m = jnp.max(sc, axis=1, keepdims=True)
                o = jnp.dot(jnp.exp(sc - m).astype(BF16), jnp.where(own, vc, one),
                            preferred_element_type=F32)
                if use_prev:
                    o = o + jnp.dot(jnp.exp(sp - m).astype(BF16), jnp.where(own, vp, one),
                                    preferred_element_type=F32)
                res.append((m, o))
            (m0, o0), (m1, o1) = res
            l = pltpu.roll(jnp.where(head0, o1, o0), HEAD_DIM, 1)
            on[p, rows, :] = jnp.where(head0, o0, o1) / l
            ls[p, rows, :] = jnp.where(head0, m0, m1) + jnp.log(l)
            return carry

        lax.fori_loop(0, nblk, block, 0, unroll=8)

    for p, d in enumerate(dils):
        if d == 1:
            continue
        cls = s_len // d
        for r in range(d):
            src = pl.ds(r * cls, cls)
            dst = pl.ds(r, cls, stride=d)
            ont[p - 1, dst, :] = on[p, src, :]
            lst[p - 1, dst, :] = ls[p, src, :]

    def merge(c, carry):
        rows = pl.ds(pl.multiple_of(c * Q_BLOCK, Q_BLOCK), Q_BLOCK)
        parts = []
        for p, d in enumerate(dils):
            if d == 1:
                parts.append((ls[p, rows, :], on[p, rows, :]))
            else:
                parts.append((lst[p - 1, rows, :], ont[p - 1, rows, :]))
        m = parts[0][0]
        for lp, _ in parts[1:]:
            m = jnp.maximum(m, lp)
        num = jnp.zeros((Q_BLOCK, LANES), F32)
        den = jnp.zeros((Q_BLOCK, LANES), F32)
        for lp, op in parts:
            e = jnp.exp(lp - m)
            num = num + op * e
            den = den + e
        o_ref[rows, :] = num / den
        return carry

    lax.fori_loop(0, nblk, merge, 0, unroll=2)


def _prompt_attention(q, k, v, batch, s_len):
    t, width = q.shape
    dils = tuple(d for _, d in DILATED)
    spans = {w // d for w, d in DILATED}
    assert len(spans) == 1 and dils[0] == 1
    span = spans.pop()
    assert span == Q_BLOCK and all(s_len % (d * Q_BLOCK) == 0 for d in dils)
    npat = len(dils)
    nblk = s_len // Q_BLOCK
    spec = pl.BlockSpec((s_len, LANES), lambda b, hp: (b, hp))
    res_bf = pltpu.VMEM((npat, s_len, LANES), BF16)
    res_f = pltpu.VMEM((npat, s_len, LANES), F32)
    tok_f = pltpu.VMEM((npat - 1, s_len, LANES), F32)
    return pl.pallas_call(
        functools.partial(_pattn_body, dils=dils, span=span),
        grid=(batch, width // LANES),
        in_specs=[spec, spec, spec],
        out_specs=spec,
        out_shape=jax.ShapeDtypeStruct((t, width), F32),
        scratch_shapes=[res_bf, pltpu.VMEM((npat, nblk, LANES, Q_BLOCK), BF16), res_bf,
                        res_f, res_f, tok_f, tok_f],
        compiler_params=pltpu.CompilerParams(
            dimension_semantics=("parallel", "parallel"), vmem_limit_bytes=VMEM_LIMIT),
        name="prompt_attn",
    )(q, k, v)


def _sattn_body(q_ref, kn_ref, vn_ref, kt_ref, vt_ref, mc_ref, mn_ref, o_ref):
    t_len, width = q_ref.shape
    n_heads = width // HEAD_DIM
    rows = n_heads * t_len
    q = q_ref[...]
    q_rep = jnp.concatenate([q] * n_heads, axis=0)
    row_h = lax.broadcasted_iota(jnp.int32, (rows, width), 0) // t_len
    lane_h = lax.broadcasted_iota(jnp.int32, (rows, width), 1) // HEAD_DIM
    own = row_h == lane_h
    q_exp = jnp.where(own, q_rep, 0.0).astype(BF16)

    pad = jnp.zeros((mn_ref.shape[1] - t_len, width), F32)
    kn = jnp.concatenate([kn_ref[...], pad], axis=0).astype(BF16)
    vn = jnp.concatenate([vn_ref[...], pad], axis=0).astype(BF16)

    s = jnp.dot(q_exp, kt_ref[0].astype(BF16), preferred_element_type=F32)
    sn = lax.dot_general(q_exp, kn, NT_DIMS, preferred_element_type=F32)
    mc = mc_ref[...]
    mn = mn_ref[...]
    s = jnp.where(mc > 0, s, NEG)
    sn = jnp.where(mn > 0, sn, NEG)
    m = jnp.maximum(jnp.max(s, axis=1, keepdims=True), jnp.max(sn, axis=1, keepdims=True))
    p = mc * jnp.exp(s - m)
    pn = mn * jnp.exp(sn - m)
    l = jnp.sum(p, axis=1, keepdims=True) + jnp.sum(pn, axis=1, keepdims=True)
    o = lax.dot_general(p.astype(BF16), vt_ref[0].astype(BF16), NT_DIMS,
                        preferred_element_type=F32)
    o = o + jnp.dot(pn.astype(BF16), vn, preferred_element_type=F32)
    o = jnp.where(own, o / l, 0.0)
    out = o[0:t_len]
    for h in range(1, n_heads):
        out = out + o[h * t_len:(h + 1) * t_len]
    o_ref[...] = out


def _sample_mult(t_len, buf_len):
    mult = np.zeros((t_len, buf_len + t_len), np.float32)
    for w, d in DILATED:
        for t in range(t_len):
            for j in range(w // d + 1):
                idx = buf_len + t - d * j
                if idx >= 0:
                    mult[t, idx] += 1.0
    return mult


def _sample_attention(q, kn, vn, kt, vt, t_len):
    t, width = q.shape
    batch, _, buf_len = kt.shape
    n_heads = width // HEAD_DIM
    mult = _sample_mult(t_len, buf_len)
    mc = np.tile(mult[:, :buf_len], (n_heads, 1))
    mn = np.zeros((n_heads * t_len, LANES), np.float32)
    mn[:, :t_len] = np.tile(mult[:, buf_len:], (n_heads, 1))
    row_spec = pl.BlockSpec((t_len, width), lambda b: (b, 0))
    cache_spec = pl.BlockSpec((1, width, buf_len), lambda b: (b, 0, 0))
    return pl.pallas_call(
        _sattn_body,
        grid=(batch,),
        in_specs=[row_spec, row_spec, row_spec, cache_spec, cache_spec,
                  pl.BlockSpec(mc.shape, lambda b: (0, 0)),
                  pl.BlockSpec(mn.shape, lambda b: (0, 0))],
        out_specs=row_spec,
        out_shape=jax.ShapeDtypeStruct((t, width), F32),
        compiler_params=pltpu.CompilerParams(
            dimension_semantics=("parallel",), vmem_limit_bytes=VMEM_LIMIT),
        name="sample_attn",
    )(q, kn, vn, kt, vt, jnp.asarray(mc), jnp.asarray(mn))


def _ln_swish(y, g, b):
    mu = jnp.mean(y, axis=-1, keepdims=True)
    yc = y - mu
    var = jnp.mean(yc * yc, axis=-1, keepdims=True)
    return _silu(yc * lax.rsqrt(var + EPS) * g + b)


CONV_ROWS = 16
SUBLANES = 8


def _conv_out_prompt_body(ucur_ref, uprev_ref, attn_ref, h_ref, w_ref, b_ref, g_ref, beta_ref,
                          wo_ref, o_ref, ext_ref, sh_ref, wb_ref, conv_ref, *, tiles_per_seq):
    tm, ch = ucur_ref.shape
    halo = uprev_ref.shape[0]
    taps = w_ref.shape[0]
    first = (pl.program_id(0) % tiles_per_seq) == 0
    ext_ref[0:halo, :] = jnp.where(first, 0.0, uprev_ref[...])
    ext_ref[halo:halo + tm, :] = ucur_ref[...]
    sh_rows = sh_ref.shape[1]
    for s in range(1, SUBLANES):
        sh_ref[s - 1] = ext_ref[pl.ds(s, sh_rows), :]
    for w in range(taps):
        wb_ref[w] = jnp.broadcast_to(w_ref[w:w + 1, :], (SUBLANES, ch))
    off = halo - (taps - 1)
    groups = CONV_ROWS // SUBLANES
    for c in range(tm // CONV_ROWS):
        accs = [jnp.zeros((SUBLANES, ch), F32) for _ in range(groups)]
        for w in range(taps):
            s = (off + w) % SUBLANES
            wb = wb_ref[w]
            for g in range(groups):
                base = c * CONV_ROWS + g * SUBLANES + off + w - s
                x = (ext_ref[pl.ds(base, SUBLANES), :] if s == 0
                     else sh_ref[s - 1, pl.ds(base, SUBLANES), :])
                accs[g] = accs[g] + x * wb
        acc = jnp.concatenate(accs, axis=0)
        act = _ln_swish(acc + b_ref[...], g_ref[...], beta_ref[...])
        conv_ref[c * CONV_ROWS:(c + 1) * CONV_ROWS, :] = act.astype(BF16)
    aw = attn_ref.shape[1]
    o_ref[...] = (h_ref[...]
                  + jnp.dot(attn_ref[...].astype(BF16), wo_ref[0:aw, :], preferred_element_type=F32)
                  + jnp.dot(conv_ref[...], wo_ref[aw:aw + ch, :], preferred_element_type=F32))


def _conv_out_prompt(u, attn, h, dw_w, dw_b, ln_g, ln_b, wo, s_len, tm=256, halo=32):
    t, ch = u.shape
    d = h.shape[1]
    aw = attn.shape[1]
    taps = dw_w.shape[0]
    assert taps - 1 <= halo and s_len % tm == 0 and tm % halo == 0 and halo % SUBLANES == 0
    hb = tm // halo
    vec = pl.BlockSpec((1, ch), lambda i: (0, 0))
    return pl.pallas_call(
        functools.partial(_conv_out_prompt_body, tiles_per_seq=s_len // tm),
        grid=(t // tm,),
        in_specs=[
            pl.BlockSpec((tm, ch), lambda i: (i, 0)),
            pl.BlockSpec((halo, ch), lambda i: (jnp.maximum(i * hb - 1, 0), 0)),
            pl.BlockSpec((tm, aw), lambda i: (i, 0)),
            pl.BlockSpec((tm, d), lambda i: (i, 0)),
            pl.BlockSpec((taps, ch), lambda i: (0, 0)),
            vec, vec, vec,
            pl.BlockSpec((aw + ch, d), lambda i: (0, 0)),
        ],
        out_specs=pl.BlockSpec((tm, d), lambda i: (i, 0)),
        out_shape=jax.ShapeDtypeStruct((t, d), F32),
        scratch_shapes=[pltpu.VMEM((halo + tm, ch), F32),
                        pltpu.VMEM((SUBLANES - 1, halo + tm - SUBLANES, ch), F32),
                        pltpu.VMEM((taps, SUBLANES, ch), F32),
                        pltpu.VMEM((tm, ch), BF16)],
        compiler_params=pltpu.CompilerParams(
            dimension_semantics=("parallel",), vmem_limit_bytes=VMEM_LIMIT),
        name="conv_out_prompt",
    )(u, u, attn, h, dw_w, dw_b, ln_g, ln_b, wo)


def _conv_out_sample_body(u_ref, st_ref, attn_ref, h_ref, w_ref, b_ref, g_ref, beta_ref, wo_ref,
                          o_ref, ns_ref, ext_ref, conv_ref, at_ref):
    n_state, batch, ch = st_ref.shape
    t_len = u_ref.shape[0]
    taps = w_ref.shape[0]
    ext_ref[0:n_state] = st_ref[...]
    ext_ref[n_state:n_state + t_len] = u_ref[...]
    ns_ref[...] = ext_ref[t_len:t_len + n_state]
    for t in range(t_len):
        acc = jnp.zeros((batch, ch), F32)
        for w in range(taps):
            acc = acc + ext_ref[t + w] * w_ref[w:w + 1, :]
        act = _ln_swish(acc + b_ref[...], g_ref[...], beta_ref[...])
        conv_ref[t * batch:(t + 1) * batch, :] = act.astype(BF16)
        at_ref[t * batch:(t + 1) * batch, :] = attn_ref[t].astype(BF16)
    aw = attn_ref.shape[2]
    proj = (jnp.dot(at_ref[...], wo_ref[0:aw, :], preferred_element_type=F32)
            + jnp.dot(conv_ref[...], wo_ref[aw:aw + ch, :], preferred_element_type=F32))
    for t in range(t_len):
        o_ref[t] = h_ref[t] + proj[t * batch:(t + 1) * batch, :]


def _conv_out_sample(u_t, state_t, attn_t, h_t, dw_w, dw_b, ln_g, ln_b, wo):
    t_len, batch, ch = u_t.shape
    d = h_t.shape[2]
    n_state = state_t.shape[0]
    assert dw_w.shape[0] == n_state + 1
    return pl.pallas_call(
        _conv_out_sample_body,
        out_shape=[jax.ShapeDtypeStruct((t_len, batch, d), F32),
                   jax.ShapeDtypeStruct((n_state, batch, ch), F32)],
        scratch_shapes=[pltpu.VMEM((n_state + t_len, batch, ch), F32),
                        pltpu.VMEM((t_len * batch, ch), BF16),
                        pltpu.VMEM((t_len * batch, attn_t.shape[2]), BF16)],
        compiler_params=pltpu.CompilerParams(vmem_limit_bytes=VMEM_LIMIT),
        name="conv_out_sample",
    )(u_t, state_t, attn_t, h_t, dw_w, dw_b, ln_g, ln_b, wo)


def _rope_tables(positions):
    half = HEAD_DIM // 2
    inv = ROPE_THETA ** (-np.arange(half, dtype=np.float64) / half)
    ang = np.asarray(positions, np.float64)[:, None] * inv[None, :]
    cos, sin = np.cos(ang), np.sin(ang)
    reps = LANES // HEAD_DIM
    cos_t = np.tile(np.concatenate([cos, cos], axis=1), (1, reps))
    sin_t = np.tile(np.concatenate([-sin, sin], axis=1), (1, reps))
    return jnp.asarray(cos_t, F32), jnp.asarray(sin_t, F32)


def kernel(x_prompt, x_sample, cache_k, cache_v, state_conv, ln_ffn1, ffn1_w_gate, ffn1_w_up,
           ffn1_w_down, ln_mix, w_in, q_norm, k_norm, conv_dw_w, conv_dw_b, conv_ln_g, conv_ln_b,
           w_out, ln_ffn2, ffn2_w_gate, ffn2_w_up, ffn2_w_down):
    batch, s_len, d_model = x_prompt.shape
    dec_batch, t_len, _ = x_sample.shape
    depth, _, buf_len, n_heads, head_dim = cache_k.shape
    assert depth == 1 and head_dim == HEAD_DIM
    width = n_heads * head_dim
    ch = conv_dw_w.shape[2]

    xp = x_prompt.reshape(batch * s_len, d_model)
    xs = x_sample.reshape(dec_batch * t_len, d_model)
    ts = dec_batch * t_len

    bf = lambda a: a[0].astype(BF16)
    wg1, wu1, wd1 = bf(ffn1_w_gate), bf(ffn1_w_up), bf(ffn1_w_down)
    wg2, wu2, wd2 = bf(ffn2_w_gate), bf(ffn2_w_up), bf(ffn2_w_down)
    wi, wo = bf(w_in), bf(w_out)
    reps = width // HEAD_DIM
    gq = jnp.tile(q_norm, (1, reps))
    gk = jnp.tile(k_norm, (1, reps))
    heads_per_tile = MXU_DIM // HEAD_DIM
    pm = jnp.asarray(np.kron(np.eye(heads_per_tile), np.ones((HEAD_DIM, HEAD_DIM))), BF16)

    cos_p, sin_p = _rope_tables(np.arange(s_len))
    cos_s, sin_s = _rope_tables(np.tile(PAST_LEN + np.arange(t_len), dec_batch))

    hp = _ffn(xp, ln_ffn1, wg1, wu1, wd1, tm=512)
    hs = _ffn(xs, ln_ffn1, wg1, wu1, wd1, tm=ts)

    qp, kp, vp, up = _inproj(hp, ln_mix, wi, gq, gk, cos_p, sin_p, pm, tm=512)
    qs, ks, vs, us = _inproj(hs, ln_mix, wi, gq, gk, cos_s, sin_s, pm, tm=ts)

    attn_p = _prompt_attention(qp, kp, vp, batch, s_len)
    kt = jnp.transpose(cache_k[0], (0, 2, 3, 1)).reshape(dec_batch, width, buf_len)
    vt = jnp.transpose(cache_v[0], (0, 2, 3, 1)).reshape(dec_batch, width, buf_len)
    attn_s = _sample_attention(qs, ks, vs, kt, vt, t_len)

    hp = _conv_out_prompt(up, attn_p, hp, conv_dw_w[0], conv_dw_b, conv_ln_g, conv_ln_b, wo, s_len)
    state_t = jnp.transpose(state_conv[0], (1, 0, 2))
    time_major = lambda a: jnp.transpose(a.reshape(dec_batch, t_len, a.shape[-1]), (1, 0, 2))
    hs_t, new_state_t = _conv_out_sample(time_major(us), state_t, time_major(attn_s),
                                         time_major(hs), conv_dw_w[0], conv_dw_b,
                                         conv_ln_g, conv_ln_b, wo)
    hs = jnp.transpose(hs_t, (1, 0, 2)).reshape(ts, d_model)

    yp = _ffn(hp, ln_ffn2, wg2, wu2, wd2, tm=512)
    ys = _ffn(hs, ln_ffn2, wg2, wu2, wd2, tm=ts)

    n_state = state_conv.shape[2]
    kv_p = (1, batch, s_len, n_heads, head_dim)
    kv_s = (1, dec_batch, t_len, n_heads, head_dim)
    return (yp.reshape(batch, s_len, d_model),
            ys.reshape(dec_batch, t_len, d_model),
            kp.reshape(kv_p), vp.reshape(kv_p),
            up.reshape(batch, s_len, ch)[:, s_len - n_state:][None],
            ks.reshape(kv_s), vs.reshape(kv_s),
            jnp.transpose(new_state_t, (1, 0, 2))[None])
```

```python
import functools

import numpy as np
import jax
import jax.numpy as jnp
from jax import lax
from jax.experimental import pallas as pl
from jax.experimental.pallas import tpu as pltpu

F32 = jnp.float32
BF16 = jnp.bfloat16

HEAD_DIM = 64
PAST_LEN = 16384
DILATED = ((128, 1), (512, 4), (2048, 16))
Q_BLOCK = 128
ROPE_THETA = 10000.0
EPS = 1e-6
FFN_RES = 0.5
NEG = -1e30

LANES = 128
MXU_DIM = 256
VMEM_LIMIT = 56 * 1024 * 1024
FFN_VMEM_LIMIT = 60 * 1024 * 1024

NT_DIMS = (((1,), (1,)), ((), ()))


def _silu(x):
    return x * jax.nn.sigmoid(x)


def _rms_rows(x, gain):
    ms = jnp.mean(x * x, axis=-1, keepdims=True)
    return x * lax.rsqrt(ms + EPS) * gain


FFN_ROW_CHUNK = 512


def _ffn_body(xp_ref, xs_ref, g_ref, wg_ref, wu_ref, wd_ref, op_ref, os_ref,
              h_ref, wgb_ref, wub_ref, wdb_ref):
    tp = xp_ref.shape[0]
    ts = xs_ref.shape[0]

    @pl.when(pl.program_id(1) == 0)
    def _():
        for r0 in range(0, tp, FFN_ROW_CHUNK):
            x = xp_ref[r0:r0 + FFN_ROW_CHUNK, :]
            h_ref[r0:r0 + FFN_ROW_CHUNK, :] = _rms_rows(x, g_ref[...]).astype(BF16)
            op_ref[r0:r0 + FFN_ROW_CHUNK, :] = x
        x = xs_ref[...]
        h_ref[tp:tp + ts, :] = _rms_rows(x, g_ref[...]).astype(BF16)
        os_ref[...] = x

    wgb_ref[...] = wg_ref[...].astype(BF16)
    wub_ref[...] = wu_ref[...].astype(BF16)
    wdb_ref[...] = wd_ref[...].astype(BF16)

    def half_step(h):
        g = jnp.dot(h, wgb_ref[...], preferred_element_type=F32)
        u = jnp.dot(h, wub_ref[...], preferred_element_type=F32)
        a = (_silu(g) * u * FFN_RES).astype(BF16)
        return jnp.dot(a, wdb_ref[...], preferred_element_type=F32)

    chunk = (tp + ts) // 2
    op_ref[0:chunk, :] += half_step(h_ref[0:chunk, :])
    res = half_step(h_ref[chunk:tp + ts, :])
    op_ref[chunk:tp, :] += res[0:tp - chunk, :]
    os_ref[...] += res[tp - chunk:chunk, :]


def _ffn(xp, xs, gain, wg, wu, wd, tiles=8, tf=256):
    tp_all, d = xp.shape
    ts_all = xs.shape[0]
    dff = wg.shape[1]
    tp, ts = tp_all // tiles, ts_all // tiles
    assert tp % FFN_ROW_CHUNK == 0 and ts % 16 == 0 and dff % tf == 0 and (tp + ts) % 32 == 0
    row = lambda rows: pl.BlockSpec((rows, d), lambda i, f: (i, 0))
    return pl.pallas_call(
        _ffn_body,
        grid=(tiles, dff // tf),
        in_specs=[
            row(tp), row(ts),
            pl.BlockSpec((1, d), lambda i, f: (0, 0)),
            pl.BlockSpec((d, tf), lambda i, f: (0, f)),
            pl.BlockSpec((d, tf), lambda i, f: (0, f)),
            pl.BlockSpec((tf, d), lambda i, f: (f, 0)),
        ],
        out_specs=[row(tp), row(ts)],
        out_shape=[jax.ShapeDtypeStruct((tp_all, d), F32), jax.ShapeDtypeStruct((ts_all, d), F32)],
        scratch_shapes=[pltpu.VMEM((tp + ts, d), BF16), pltpu.VMEM((d, tf), BF16),
                        pltpu.VMEM((d, tf), BF16), pltpu.VMEM((tf, d), BF16)],
        compiler_params=pltpu.CompilerParams(
            dimension_semantics=("parallel", "arbitrary"), vmem_limit_bytes=FFN_VMEM_LIMIT),
        name="ffn",
    )(xp, xs, gain, wg, wu, wd)


def _inproj_body(x_ref, g_ref, w_ref, gq_ref, gk_ref, cos_ref, sin_ref, pm_ref,
                 q_ref, k_ref, v_ref, u_ref, hn_ref, a_ref):
    j = pl.program_id(1)

    @pl.when(j == 0)
    def _():
        hn_ref[...] = _rms_rows(x_ref[...], g_ref[...]).astype(BF16)

    z = jnp.dot(hn_ref[...], w_ref[...], preferred_element_type=F32)
    width = z.shape[1]
    lane = lax.broadcasted_iota(jnp.int32, (z.shape[0], LANES), 1)
    first_half = (lane % HEAD_DIM) < (HEAD_DIM // 2)

    def norm_rope(gain_ref, out_ref, scale):
        zz = z * z
        hi = zz.astype(BF16)
        lo = (zz - hi.astype(F32)).astype(BF16)
        pm = pm_ref[...]
        for c in range(width // MXU_DIM):
            sl = slice(c * MXU_DIM, (c + 1) * MXU_DIM)
            ss = (jnp.dot(hi[:, sl], pm, preferred_element_type=F32)
                  + jnp.dot(lo[:, sl], pm, preferred_element_type=F32))
            zn = z[:, sl] * lax.rsqrt(ss * (1.0 / HEAD_DIM) + EPS) * gain_ref[:, sl]
            for e in range(MXU_DIM // LANES):
                x = zn[:, e * LANES:(e + 1) * LANES]
                partner = jnp.where(first_half,
                                    pltpu.roll(x, LANES - HEAD_DIM // 2, 1),
                                    pltpu.roll(x, HEAD_DIM // 2, 1))
                y = x * cos_ref[...] + partner * sin_ref[...]
                lo_lane = c * MXU_DIM + e * LANES
                out_ref[:, lo_lane:lo_lane + LANES] = y * scale

    @pl.when(j == 0)
    def _():
        norm_rope(gq_ref, q_ref, HEAD_DIM ** -0.5)

    @pl.when(j == 1)
    def _():
        norm_rope(gk_ref, k_ref, 1.0)

    @pl.when(j == 2)
    def _():
        v_ref[...] = z

    @pl.when(j == 3)
    def _():
        a_ref[...] = z

    @pl.when(j == 4)
    def _():
        u_ref[...] = a_ref[...] * jax.nn.sigmoid(z)


def _inproj(x, gain, w, gq, gk, cos_t, sin_t, pm, tm):
    t, d = x.shape
    width = w.shape[1] // 5
    period_blocks = cos_t.shape[0] // tm
    tab_spec = pl.BlockSpec((tm, LANES), lambda i, j: (i % period_blocks, 0))
    out_spec = pl.BlockSpec((tm, width), lambda i, j: (i, 0))
    out_sds = jax.ShapeDtypeStruct((t, width), F32)
    return pl.pallas_call(
        _inproj_body,
        grid=(t // tm, 5),
        in_specs=[
            pl.BlockSpec((tm, d), lambda i, j: (i, 0)),
            pl.BlockSpec((1, d), lambda i, j: (0, 0)),
            pl.BlockSpec((d, width), lambda i, j: (0, j)),
            pl.BlockSpec((1, width), lambda i, j: (0, 0)),
            pl.BlockSpec((1, width), lambda i, j: (0, 0)),
            tab_spec, tab_spec,
            pl.BlockSpec((MXU_DIM, MXU_DIM), lambda i, j: (0, 0)),
        ],
        out_specs=[out_spec] * 4,
        out_shape=[out_sds] * 4,
        scratch_shapes=[pltpu.VMEM((tm, d), BF16), pltpu.VMEM((tm, width), F32)],
        compiler_params=pltpu.CompilerParams(
            dimension_semantics=("parallel", "arbitrary"), vmem_limit_bytes=VMEM_LIMIT),
        name="inproj",
    )(x, gain, w, gq, gk, cos_t, sin_t, pm)


def _pattn_body(q_ref, k_ref, v_ref, o_ref, qd, kt, vd, on, ls, ont, lst, *, dils, span):
    s_len = q_ref.shape[0]
    nblk = s_len // Q_BLOCK

    for p, d in enumerate(dils):
        cls = s_len // d
        for r in range(d):
            rows = pl.ds(r, cls, stride=d) if d > 1 else pl.ds(0, cls)
            dst = pl.ds(r * cls, cls)
            qd[p, dst, :] = q_ref[rows, :].astype(BF16)
            vd[p, dst, :] = v_ref[rows, :].astype(BF16)
            for j in range(cls // Q_BLOCK):
                blk = (pl.ds(r + j * Q_BLOCK * d, Q_BLOCK, stride=d) if d > 1
                       else pl.ds(j * Q_BLOCK, Q_BLOCK))
                kt[p, (r * cls) // Q_BLOCK + j] = k_ref[blk, :].T.astype(BF16)

    lane = lax.broadcasted_iota(jnp.int32, (Q_BLOCK, LANES), 1)
    head0 = lane < HEAD_DIM
    qi = lax.broadcasted_iota(jnp.int32, (Q_BLOCK, Q_BLOCK), 0)
    ki = lax.broadcasted_iota(jnp.int32, (Q_BLOCK, Q_BLOCK), 1)
    causal = ki <= qi
    band = (qi + Q_BLOCK - ki) <= span
    one = jnp.ones((), BF16)
    zero = jnp.zeros((), BF16)

    for p, d in enumerate(dils):
        blocks_per_class = (s_len // d) // Q_BLOCK
        use_prev = blocks_per_class > 1

        def block(n, carry, p=p, blocks_per_class=blocks_per_class, use_prev=use_prev):
            rows = pl.ds(pl.multiple_of(n * Q_BLOCK, Q_BLOCK), Q_BLOCK)
            q = qd[p, rows, :]
            vc = vd[p, rows, :]
            ktc = kt[p, n]
            if use_prev:
                n_prev = jnp.maximum(n - 1, 0)
                ktp = kt[p, n_prev]
                vp = vd[p, pl.ds(pl.multiple_of(n_prev * Q_BLOCK, Q_BLOCK), Q_BLOCK), :]
                prev_ok = band & ((n % blocks_per_class) != 0)
            res = []
            for h in range(2):
                own = head0 if h == 0 else ~head0
                qh = jnp.where(own, q, zero)
                sc = jnp.where(causal, jnp.dot(qh, ktc, preferred_element_type=F32), NEG)
                if use_prev:
                    sp = jnp.where(prev_ok, jnp.dot(qh, ktp, preferred_element_type=F32), NEG)
                    m = jnp.max(jnp.maximum(sc, sp), axis=1, keepdims=True)
                else:
                    m = jnp.max(sc, axis=1, keepdims=True)
                o = jnp.dot(jnp.exp(sc - m).astype(BF16), jnp.where(own, vc, one),
                            preferred_element_type=F32)
                if use_prev:
                    o = o + jnp.dot(jnp.exp(sp - m).astype(BF16), jnp.where(own, vp, one),
                                    preferred_element_type=F32)
                res.append((m, o))
            (m0, o0), (m1, o1) = res
            l = pltpu.roll(jnp.where(head0, o1, o0), HEAD_DIM, 1)
            on[p, rows, :] = jnp.where(head0, o0, o1) / l
            ls[p, rows, :] = jnp.where(head0, m0, m1) + jnp.log(l)
            return carry

        lax.fori_loop(0, nblk, block, 0, unroll=8)

    for p, d in enumerate(dils):
        if d == 1:
            continue
        cls = s_len // d
        for r in range(d):
            src = pl.ds(r * cls, cls)
            dst = pl.ds(r, cls, stride=d)
            ont[p - 1, dst, :] = on[p, src, :]
            lst[p - 1, dst, :] = ls[p, src, :]

    def merge(c, carry):
        rows = pl.ds(pl.multiple_of(c * Q_BLOCK, Q_BLOCK), Q_BLOCK)
        parts = []
        for p, d in enumerate(dils):
            if d == 1:
                parts.append((ls[p, rows, :], on[p, rows, :]))
            else:
                parts.append((lst[p - 1, rows, :], ont[p - 1, rows, :]))
        m = parts[0][0]
        for lp, _ in parts[1:]:
            m = jnp.maximum(m, lp)
        num = jnp.zeros((Q_BLOCK, LANES), F32)
        den = jnp.zeros((Q_BLOCK, LANES), F32)
        for lp, op in parts:
            e = jnp.exp(lp - m)
            num = num + op * e
            den = den + e
        o_ref[rows, :] = num / den
        return carry

    lax.fori_loop(0, nblk, merge, 0, unroll=2)


def _prompt_attention(q, k, v, batch, s_len):
    t, width = q.shape
    dils = tuple(d for _, d in DILATED)
    spans = {w // d for w, d in DILATED}
    assert len(spans) == 1 and dils[0] == 1
    span = spans.pop()
    assert span == Q_BLOCK and all(s_len % (d * Q_BLOCK) == 0 for d in dils)
    npat = len(dils)
    nblk = s_len // Q_BLOCK
    spec = pl.BlockSpec((s_len, LANES), lambda b, hp: (b, hp))
    res_bf = pltpu.VMEM((npat, s_len, LANES), BF16)
    res_f = pltpu.VMEM((npat, s_len, LANES), F32)
    tok_f = pltpu.VMEM((npat - 1, s_len, LANES), F32)
    return pl.pallas_call(
        functools.partial(_pattn_body, dils=dils, span=span),
        grid=(batch, width // LANES),
        in_specs=[spec, spec, spec],
        out_specs=spec,
        out_shape=jax.ShapeDtypeStruct((t, width), F32),
        scratch_shapes=[res_bf, pltpu.VMEM((npat, nblk, LANES, Q_BLOCK), BF16), res_bf,
                        res_f, res_f, tok_f, tok_f],
        compiler_params=pltpu.CompilerParams(
            dimension_semantics=("parallel", "parallel"), vmem_limit_bytes=VMEM_LIMIT),
        name="prompt_attn",
    )(q, k, v)


def _sattn_body(q_ref, kn_ref, vn_ref, kt_ref, vt_ref, mc_ref, mn_ref, o_ref):
    t_len, width = q_ref.shape
    n_heads = width // HEAD_DIM
    rows = n_heads * t_len
    q = q_ref[...]
    q_rep = jnp.concatenate([q] * n_heads, axis=0)
    row_h = lax.broadcasted_iota(jnp.int32, (rows, width), 0) // t_len
    lane_h = lax.broadcasted_iota(jnp.int32, (rows, width), 1) // HEAD_DIM
    own = row_h == lane_h
    q_exp = jnp.where(own, q_rep, 0.0).astype(BF16)

    pad = jnp.zeros((mn_ref.shape[1] - t_len, width), F32)
    kn = jnp.concatenate([kn_ref[...], pad], axis=0).astype(BF16)
    vn = jnp.concatenate([vn_ref[...], pad], axis=0).astype(BF16)

    s = jnp.dot(q_exp, kt_ref[0].astype(BF16), preferred_element_type=F32)
    sn = lax.dot_general(q_exp, kn, NT_DIMS, preferred_element_type=F32)
    mc = mc_ref[...]
    mn = mn_ref[...]
    s = jnp.where(mc > 0, s, NEG)
    sn = jnp.where(mn > 0, sn, NEG)
    m = jnp.maximum(jnp.max(s, axis=1, keepdims=True), jnp.max(sn, axis=1, keepdims=True))
    p = mc * jnp.exp(s - m)
    pn = mn * jnp.exp(sn - m)
    l = jnp.sum(p, axis=1, keepdims=True) + jnp.sum(pn, axis=1, keepdims=True)
    o = lax.dot_general(p.astype(BF16), vt_ref[0].astype(BF16), NT_DIMS,
                        preferred_element_type=F32)
    o = o + jnp.dot(pn.astype(BF16), vn, preferred_element_type=F32)
    o = jnp.where(own, o / l, 0.0)
    out = o[0:t_len]
    for h in range(1, n_heads):
        out = out + o[h * t_len:(h + 1) * t_len]
    o_ref[...] = out


def _sample_mult(t_len, buf_len):
    mult = np.zeros((t_len, buf_len + t_len), np.float32)
    for w, d in DILATED:
        for t in range(t_len):
            for j in range(w // d + 1):
                idx = buf_len + t - d * j
                if idx >= 0:
                    mult[t, idx] += 1.0
    return mult


def _sample_attention(q, kn, vn, kt, vt, t_len):
    t, width = q.shape
    batch, _, buf_len = kt.shape
    n_heads = width // HEAD_DIM
    mult = _sample_mult(t_len, buf_len)
    mc = np.tile(mult[:, :buf_len], (n_heads, 1))
    mn = np.zeros((n_heads * t_len, LANES), np.float32)
    mn[:, :t_len] = np.tile(mult[:, buf_len:], (n_heads, 1))
    row_spec = pl.BlockSpec((t_len, width), lambda b: (b, 0))
    cache_spec = pl.BlockSpec((1, width, buf_len), lambda b: (b, 0, 0))
    return pl.pallas_call(
        _sattn_body,
        grid=(batch,),
        in_specs=[row_spec, row_spec, row_spec, cache_spec, cache_spec,
                  pl.BlockSpec(mc.shape, lambda b: (0, 0)),
                  pl.BlockSpec(mn.shape, lambda b: (0, 0))],
        out_specs=row_spec,
        out_shape=jax.ShapeDtypeStruct((t, width), F32),
        compiler_params=pltpu.CompilerParams(
            dimension_semantics=("parallel",), vmem_limit_bytes=VMEM_LIMIT),
        name="sample_attn",
    )(q, kn, vn, kt, vt, jnp.asarray(mc), jnp.asarray(mn))


def _ln_swish(y, g, b):
    mu = jnp.mean(y, axis=-1, keepdims=True)
    yc = y - mu
    var = jnp.mean(yc * yc, axis=-1, keepdims=True)
    return _silu(yc * lax.rsqrt(var + EPS) * g + b)


CONV_ROWS = 16
SUBLANES = 8


def _conv_out_prompt_body(ucur_ref, uprev_ref, attn_ref, h_ref, w_ref, b_ref, g_ref, beta_ref,
                          wo_ref, o_ref, ext_ref, sh_ref, wb_ref, conv_ref, *, tiles_per_seq):
    tm, ch = ucur_ref.shape
    halo = uprev_ref.shape[0]
    taps = w_ref.shape[0]
    first = (pl.program_id(0) % tiles_per_seq) == 0
    ext_ref[0:halo, :] = jnp.where(first, 0.0, uprev_ref[...])
    ext_ref[halo:halo + tm, :] = ucur_ref[...]
    sh_rows = sh_ref.shape[1]
    for s in range(1, SUBLANES):
        sh_ref[s - 1] = ext_ref[pl.ds(s, sh_rows), :]
    for w in range(taps):
        wb_ref[w] = jnp.broadcast_to(w_ref[w:w + 1, :], (SUBLANES, ch))
    off = halo - (taps - 1)
    groups = CONV_ROWS // SUBLANES
    for c in range(tm // CONV_ROWS):
        accs = [jnp.zeros((SUBLANES, ch), F32) for _ in range(groups)]
        for w in range(taps):
            s = (off + w) % SUBLANES
            wb = wb_ref[w]
            for g in range(groups):
                base = c * CONV_ROWS + g * SUBLANES + off + w - s
                x = (ext_ref[pl.ds(base, SUBLANES), :] if s == 0
                     else sh_ref[s - 1, pl.ds(base, SUBLANES), :])
                accs[g] = accs[g] + x * wb
        acc = jnp.concatenate(accs, axis=0)
        act = _ln_swish(acc + b_ref[...], g_ref[...], beta_ref[...])
        conv_ref[c * CONV_ROWS:(c + 1) * CONV_ROWS, :] = act.astype(BF16)
    aw = attn_ref.shape[1]
    o_ref[...] = (h_ref[...]
                  + jnp.dot(attn_ref[...].astype(BF16), wo_ref[0:aw, :], preferred_element_type=F32)
                  + jnp.dot(conv_ref[...], wo_ref[aw:aw + ch, :], preferred_element_type=F32))


def _conv_out_prompt(u, attn, h, dw_w, dw_b, ln_g, ln_b, wo, s_len, tm=256, halo=32):
    t, ch = u.shape
    d = h.shape[1]
    aw = attn.shape[1]
    taps = dw_w.shape[0]
    assert taps - 1 <= halo and s_len % tm == 0 and tm % halo == 0 and halo % SUBLANES == 0
    hb = tm // halo
    vec = pl.BlockSpec((1, ch), lambda i: (0, 0))
    return pl.pallas_call(
        functools.partial(_conv_out_prompt_body, tiles_per_seq=s_len // tm),
        grid=(t // tm,),
        in_specs=[
            pl.BlockSpec((tm, ch), lambda i: (i, 0)),
            pl.BlockSpec((halo, ch), lambda i: (jnp.maximum(i * hb - 1, 0), 0)),
            pl.BlockSpec((tm, aw), lambda i: (i, 0)),
            pl.BlockSpec((tm, d), lambda i: (i, 0)),
            pl.BlockSpec((taps, ch), lambda i: (0, 0)),
            vec, vec, vec,
            pl.BlockSpec((aw + ch, d), lambda i: (0, 0)),
        ],
        out_specs=pl.BlockSpec((tm, d), lambda i: (i, 0)),
        out_shape=jax.ShapeDtypeStruct((t, d), F32),
        scratch_shapes=[pltpu.VMEM((halo + tm, ch), F32),
                        pltpu.VMEM((SUBLANES - 1, halo + tm - SUBLANES, ch), F32),
                        pltpu.VMEM((taps, SUBLANES, ch), F32),
                        pltpu.VMEM((tm, ch), BF16)],
        compiler_params=pltpu.CompilerParams(
            dimension_semantics=("parallel",), vmem_limit_bytes=VMEM_LIMIT),
        name="conv_out_prompt",
    )(u, u, attn, h, dw_w, dw_b, ln_g, ln_b, wo)


def _conv_out_sample_body(u_ref, st_ref, attn_ref, h_ref, w_ref, b_ref, g_ref, beta_ref, wo_ref,
                          o_ref, ns_ref, ext_ref, conv_ref, at_ref):
    n_state, batch, ch = st_ref.shape
    t_len = u_ref.shape[0]
    taps = w_ref.shape[0]
    ext_ref[0:n_state] = st_ref[...]
    ext_ref[n_state:n_state + t_len] = u_ref[...]
    ns_ref[...] = ext_ref[t_len:t_len + n_state]
    for t in range(t_len):
        acc = jnp.zeros((batch, ch), F32)
        for w in range(taps):
            acc = acc + ext_ref[t + w] * w_ref[w:w + 1, :]
        act = _ln_swish(acc + b_ref[...], g_ref[...], beta_ref[...])
        conv_ref[t * batch:(t + 1) * batch, :] = act.astype(BF16)
        at_ref[t * batch:(t + 1) * batch, :] = attn_ref[t].astype(BF16)
    aw = attn_ref.shape[2]
    proj = (jnp.dot(at_ref[...], wo_ref[0:aw, :], preferred_element_type=F32)
            + jnp.dot(conv_ref[...], wo_ref[aw:aw + ch, :], preferred_element_type=F32))
    for t in range(t_len):
        o_ref[t] = h_ref[t] + proj[t * batch:(t + 1) * batch, :]


def _conv_out_sample(u_t, state_t, attn_t, h_t, dw_w, dw_b, ln_g, ln_b, wo):
    t_len, batch, ch = u_t.shape
    d = h_t.shape[2]
    n_state = state_t.shape[0]
    assert dw_w.shape[0] == n_state + 1
    return pl.pallas_call(
        _conv_out_sample_body,
        out_shape=[jax.ShapeDtypeStruct((t_len, batch, d), F32),
                   jax.ShapeDtypeStruct((n_state, batch, ch), F32)],
        scratch_shapes=[pltpu.VMEM((n_state + t_len, batch, ch), F32),
                        pltpu.VMEM((t_len * batch, ch), BF16),
                        pltpu.VMEM((t_len * batch, attn_t.shape[2]), BF16)],
        compiler_params=pltpu.CompilerParams(vmem_limit_bytes=VMEM_LIMIT),
        name="conv_out_sample",
    )(u_t, state_t, attn_t, h_t, dw_w, dw_b, ln_g, ln_b, wo)


def _rope_tables(positions):
    half = HEAD_DIM // 2
    inv = ROPE_THETA ** (-np.arange(half, dtype=np.float64) / half)
    ang = np.asarray(positions, np.float64)[:, None] * inv[None, :]
    cos, sin = np.cos(ang), np.sin(ang)
    reps = LANES // HEAD_DIM
    cos_t = np.tile(np.concatenate([cos, cos], axis=1), (1, reps))
    sin_t = np.tile(np.concatenate([-sin, sin], axis=1), (1, reps))
    return jnp.asarray(cos_t, F32), jnp.asarray(sin_t, F32)


def kernel(x_prompt, x_sample, cache_k, cache_v, state_conv, ln_ffn1, ffn1_w_gate, ffn1_w_up,
           ffn1_w_down, ln_mix, w_in, q_norm, k_norm, conv_dw_w, conv_dw_b, conv_ln_g, conv_ln_b,
           w_out, ln_ffn2, ffn2_w_gate, ffn2_w_up, ffn2_w_down):
    batch, s_len, d_model = x_prompt.shape
    dec_batch, t_len, _ = x_sample.shape
    depth, _, buf_len, n_heads, head_dim = cache_k.shape
    assert depth == 1 and head_dim == HEAD_DIM
    width = n_heads * head_dim
    ch = conv_dw_w.shape[2]

    xp = x_prompt.reshape(batch * s_len, d_model)
    xs = x_sample.reshape(dec_batch * t_len, d_model)
    ts = dec_batch * t_len

    bf = lambda a: a[0].astype(BF16)
    wi, wo = bf(w_in), bf(w_out)
    reps = width // HEAD_DIM
    gq = jnp.tile(q_norm, (1, reps))
    gk = jnp.tile(k_norm, (1, reps))
    heads_per_tile = MXU_DIM // HEAD_DIM
    pm = jnp.asarray(np.kron(np.eye(heads_per_tile), np.ones((HEAD_DIM, HEAD_DIM))), BF16)

    cos_p, sin_p = _rope_tables(np.arange(s_len))
    cos_s, sin_s = _rope_tables(np.tile(PAST_LEN + np.arange(t_len), dec_batch))

    hp, hs = _ffn(xp, xs, ln_ffn1, ffn1_w_gate[0], ffn1_w_up[0], ffn1_w_down[0])

    qp, kp, vp, up = _inproj(hp, ln_mix, wi, gq, gk, cos_p, sin_p, pm, tm=512)
    qs, ks, vs, us = _inproj(hs, ln_mix, wi, gq, gk, cos_s, sin_s, pm, tm=ts)

    attn_p = _prompt_attention(qp, kp, vp, batch, s_len)
    kt = jnp.transpose(cache_k[0], (0, 2, 3, 1)).reshape(dec_batch, width, buf_len)
    vt = jnp.transpose(cache_v[0], (0, 2, 3, 1)).reshape(dec_batch, width, buf_len)
    attn_s = _sample_attention(qs, ks, vs, kt, vt, t_len)

    hp = _conv_out_prompt(up, attn_p, hp, conv_dw_w[0], conv_dw_b, conv_ln_g, conv_ln_b, wo, s_len)
    state_t = jnp.transpose(state_conv[0], (1, 0, 2))
    time_major = lambda a: jnp.transpose(a.reshape(dec_batch, t_len, a.shape[-1]), (1, 0, 2))
    hs_t, new_state_t = _conv_out_sample(time_major(us), state_t, time_major(attn_s),
                                         time_major(hs), conv_dw_w[0], conv_dw_b,
                                         conv_ln_g, conv_ln_b, wo)
    hs = jnp.transpose(hs_t, (1, 0, 2)).reshape(ts, d_model)

    yp, ys = _ffn(hp, hs, ln_ffn2, ffn2_w_gate[0], ffn2_w_up[0], ffn2_w_down[0])

    n_state = state_conv.shape[2]
    kv_p = (1, batch, s_len, n_heads, head_dim)
    kv_s = (1, dec_batch, t_len, n_heads, head_dim)
    return (yp.reshape(batch, s_len, d_model),
            ys.reshape(dec_batch, t_len, d_model),
            kp.reshape(kv_p), vp.reshape(kv_p),
            up.reshape(batch, s_len, ch)[:, s_len - n_state:][None],
            ks.reshape(kv_s), vs.reshape(kv_s),
            jnp.transpose(new_state_t, (1, 0, 2))[None])
```

```python
import functools

import numpy as np
import jax
import jax.numpy as jnp
from jax import lax
from jax.experimental import pallas as pl
from jax.experimental.pallas import tpu as pltpu

F32 = jnp.float32
BF16 = jnp.bfloat16

HEAD_DIM = 64
PAST_LEN = 16384
DILATED = ((128, 1), (512, 4), (2048, 16))
Q_BLOCK = 128
ROPE_THETA = 10000.0
EPS = 1e-6
FFN_RES = 0.5
NEG = -1e30

LANES = 128
MXU_DIM = 256
VMEM_LIMIT = 56 * 1024 * 1024
FFN_VMEM_LIMIT = 60 * 1024 * 1024

NT_DIMS = (((1,), (1,)), ((), ()))


def _silu(x):
    return x * jax.nn.sigmoid(x)


def _rms_rows(x, gain):
    ms = jnp.mean(x * x, axis=-1, keepdims=True)
    return x * lax.rsqrt(ms + EPS) * gain


FFN_ROW_CHUNK = 512


def _ffn_body(xp_ref, xs_ref, g_ref, wg_ref, wu_ref, wd_ref, op_ref, os_ref,
              h_ref, wgb_ref, wub_ref, wdb_ref):
    tp = xp_ref.shape[0]
    ts = xs_ref.shape[0]

    @pl.when(pl.program_id(1) == 0)
    def _():
        for r0 in range(0, tp, FFN_ROW_CHUNK):
            x = xp_ref[r0:r0 + FFN_ROW_CHUNK, :]
            h_ref[r0:r0 + FFN_ROW_CHUNK, :] = _rms_rows(x, g_ref[...]).astype(BF16)
            op_ref[r0:r0 + FFN_ROW_CHUNK, :] = x
        x = xs_ref[...]
        h_ref[tp:tp + ts, :] = _rms_rows(x, g_ref[...]).astype(BF16)
        os_ref[...] = x

    wgb_ref[...] = wg_ref[...].astype(BF16)
    wub_ref[...] = wu_ref[...].astype(BF16)
    wdb_ref[...] = wd_ref[...].astype(BF16)

    def half_step(h):
        g = jnp.dot(h, wgb_ref[...], preferred_element_type=F32)
        u = jnp.dot(h, wub_ref[...], preferred_element_type=F32)
        a = (_silu(g) * u * FFN_RES).astype(BF16)
        return jnp.dot(a, wdb_ref[...], preferred_element_type=F32)

    chunk = (tp + ts) // 2
    op_ref[0:chunk, :] += half_step(h_ref[0:chunk, :])
    res = half_step(h_ref[chunk:tp + ts, :])
    op_ref[chunk:tp, :] += res[0:tp - chunk, :]
    os_ref[...] += res[tp - chunk:chunk, :]


def _ffn(xp, xs, gain, wg, wu, wd, tiles=8, tf=256):
    tp_all, d = xp.shape
    ts_all = xs.shape[0]
    dff = wg.shape[1]
    tp, ts = tp_all // tiles, ts_all // tiles
    assert tp % FFN_ROW_CHUNK == 0 and ts % 16 == 0 and dff % tf == 0 and (tp + ts) % 32 == 0
    row = lambda rows: pl.BlockSpec((rows, d), lambda i, f: (i, 0))
    return pl.pallas_call(
        _ffn_body,
        grid=(tiles, dff // tf),
        in_specs=[
            row(tp), row(ts),
            pl.BlockSpec((1, d), lambda i, f: (0, 0)),
            pl.BlockSpec((d, tf), lambda i, f: (0, f)),
            pl.BlockSpec((d, tf), lambda i, f: (0, f)),
            pl.BlockSpec((tf, d), lambda i, f: (f, 0)),
        ],
        out_specs=[row(tp), row(ts)],
        out_shape=[jax.ShapeDtypeStruct((tp_all, d), F32), jax.ShapeDtypeStruct((ts_all, d), F32)],
        scratch_shapes=[pltpu.VMEM((tp + ts, d), BF16), pltpu.VMEM((d, tf), BF16),
                        pltpu.VMEM((d, tf), BF16), pltpu.VMEM((tf, d), BF16)],
        compiler_params=pltpu.CompilerParams(
            dimension_semantics=("parallel", "arbitrary"), vmem_limit_bytes=FFN_VMEM_LIMIT),
        name="ffn",
    )(xp, xs, gain, wg, wu, wd)


def _inproj_body(x_ref, g_ref, w_ref, gq_ref, gk_ref, cos_ref, sin_ref, pm_ref,
                 q_ref, k_ref, v_ref, u_ref, *rest, transposed_kv):
    if transposed_kv:
        kt_ref, vt_ref, hn_ref, z_ref = rest
    else:
        kt_ref = vt_ref = None
        hn_ref, z_ref = rest
    j = pl.program_id(1)
    tm, width = q_ref.shape
    lane = lax.broadcasted_iota(jnp.int32, (tm, LANES), 1)
    first_half = (lane % HEAD_DIM) < (HEAD_DIM // 2)

    def matmul(slot, c=None):
        sl = slice(None) if c is None else slice(c * MXU_DIM, (c + 1) * MXU_DIM)
        z_ref[slot, :, sl] = jnp.dot(hn_ref[...], w_ref[:, sl], preferred_element_type=F32)

    def norm_rope(slot, gain_ref, out_ref, t_ref, scale):
        pm = pm_ref[...]
        for c in range(width // MXU_DIM):
            sl = slice(c * MXU_DIM, (c + 1) * MXU_DIM)
            matmul(1 - slot, c)
            z = z_ref[slot, :, sl]
            zz = z * z
            hi = zz.astype(BF16)
            lo = (zz - hi.astype(F32)).astype(BF16)
            ss = (jnp.dot(hi, pm, preferred_element_type=F32)
                  + jnp.dot(lo, pm, preferred_element_type=F32))
            zn = z * lax.rsqrt(ss * (1.0 / HEAD_DIM) + EPS) * gain_ref[:, sl]
            for e in range(MXU_DIM // LANES):
                x = zn[:, e * LANES:(e + 1) * LANES]
                partner = jnp.where(first_half,
                                    pltpu.roll(x, LANES - HEAD_DIM // 2, 1),
                                    pltpu.roll(x, HEAD_DIM // 2, 1))
                y = (x * cos_ref[...] + partner * sin_ref[...]) * scale
                lo_lane = c * MXU_DIM + e * LANES
                out_ref[:, lo_lane:lo_lane + LANES] = y
                if t_ref is not None:
                    t_ref[0, lo_lane:lo_lane + LANES, :] = y.T

    @pl.when(j == 0)
    def _():
        hn_ref[...] = _rms_rows(x_ref[...], g_ref[...]).astype(BF16)
        matmul(0)

    @pl.when(j == 1)
    def _():
        norm_rope(0, gq_ref, q_ref, None, HEAD_DIM ** -0.5)

    @pl.when(j == 2)
    def _():
        norm_rope(1, gk_ref, k_ref, kt_ref, 1.0)

    @pl.when(j == 3)
    def _():
        for c in range(width // MXU_DIM):
            sl = slice(c * MXU_DIM, (c + 1) * MXU_DIM)
            matmul(1, c)
            v = z_ref[0, :, sl]
            v_ref[:, sl] = v
            if vt_ref is not None:
                vt_ref[0, sl, :] = v.T

    @pl.when(j == 4)
    def _():
        matmul(0)

    @pl.when(j == 5)
    def _():
        u_ref[...] = z_ref[1] * jax.nn.sigmoid(z_ref[0])


def _inproj(x, gain, w, gq, gk, cos_t, sin_t, pm, tm, seq_len=None):
    t, d = x.shape
    groups = 5
    width = w.shape[1] // groups
    period_blocks = cos_t.shape[0] // tm
    tab_spec = pl.BlockSpec((tm, LANES), lambda i, j: (i % period_blocks, 0))
    out_specs = [pl.BlockSpec((tm, width), lambda i, j: (i, 0))] * 4
    out_shape = [jax.ShapeDtypeStruct((t, width), F32)] * 4
    if seq_len is not None:
        tiles_per_seq = seq_len // tm
        out_specs += [pl.BlockSpec((1, width, tm),
                                   lambda i, j: (i // tiles_per_seq, 0, i % tiles_per_seq))] * 2
        out_shape += [jax.ShapeDtypeStruct((t // seq_len, width, seq_len), F32)] * 2
    return pl.pallas_call(
        functools.partial(_inproj_body, transposed_kv=seq_len is not None),
        grid=(t // tm, groups + 1),
        in_specs=[
            pl.BlockSpec((tm, d), lambda i, j: (i, 0)),
            pl.BlockSpec((1, d), lambda i, j: (0, 0)),
            pl.BlockSpec((d, width), lambda i, j: (0, jnp.minimum(j, groups - 1))),
            pl.BlockSpec((1, width), lambda i, j: (0, 0)),
            pl.BlockSpec((1, width), lambda i, j: (0, 0)),
            tab_spec, tab_spec,
            pl.BlockSpec((MXU_DIM, MXU_DIM), lambda i, j: (0, 0)),
        ],
        out_specs=out_specs,
        out_shape=out_shape,
        scratch_shapes=[pltpu.VMEM((tm, d), BF16), pltpu.VMEM((2, tm, width), F32)],
        compiler_params=pltpu.CompilerParams(
            dimension_semantics=("parallel", "arbitrary"), vmem_limit_bytes=VMEM_LIMIT),
        name="inproj",
    )(x, gain, w, gq, gk, cos_t, sin_t, pm)


def _pattn_body(q_ref, k_ref, v_ref, o_ref, qd, kt, vd, on, ls, ont, lst, *, dils, span):
    s_len = q_ref.shape[0]
    nblk = s_len // Q_BLOCK

    for p, d in enumerate(dils):
        cls = s_len // d
        for r in range(d):
            rows = pl.ds(r, cls, stride=d) if d > 1 else pl.ds(0, cls)
            dst = pl.ds(r * cls, cls)
            qd[p, dst, :] = q_ref[rows, :].astype(BF16)
            vd[p, dst, :] = v_ref[rows, :].astype(BF16)
            for j in range(cls // Q_BLOCK):
                blk = (pl.ds(r + j * Q_BLOCK * d, Q_BLOCK, stride=d) if d > 1
                       else pl.ds(j * Q_BLOCK, Q_BLOCK))
                kt[p, (r * cls) // Q_BLOCK + j] = k_ref[blk, :].T.astype(BF16)

    lane = lax.broadcasted_iota(jnp.int32, (Q_BLOCK, LANES), 1)
    head0 = lane < HEAD_DIM
    qi = lax.broadcasted_iota(jnp.int32, (Q_BLOCK, Q_BLOCK), 0)
    ki = lax.broadcasted_iota(jnp.int32, (Q_BLOCK, Q_BLOCK), 1)
    causal = ki <= qi
    band = (qi + Q_BLOCK - ki) <= span
    one = jnp.ones((), BF16)
    zero = jnp.zeros((), BF16)

    for p, d in enumerate(dils):
        blocks_per_class = (s_len // d) // Q_BLOCK
        use_prev = blocks_per_class > 1

        def block(n, carry, p=p, blocks_per_class=blocks_per_class, use_prev=use_prev):
            rows = pl.ds(pl.multiple_of(n * Q_BLOCK, Q_BLOCK), Q_BLOCK)
            q = qd[p, rows, :]
            vc = vd[p, rows, :]
            ktc = kt[p, n]
            if use_prev:
                n_prev = jnp.maximum(n - 1, 0)
                ktp = kt[p, n_prev]
                vp = vd[p, pl.ds(pl.multiple_of(n_prev * Q_BLOCK, Q_BLOCK), Q_BLOCK), :]
                prev_ok = band & ((n % blocks_per_class) != 0)
            res = []
            for h in range(2):
                own = head0 if h == 0 else ~head0
                qh = jnp.where(own, q, zero)
                sc = jnp.where(causal, jnp.dot(qh, ktc, preferred_element_type=F32), NEG)
                if use_prev:
                    sp = jnp.where(prev_ok, jnp.dot(qh, ktp, preferred_element_type=F32), NEG)
                    m = jnp.max(jnp.maximum(sc, sp), axis=1, keepdims=True)
                else:
                    m = jnp.max(sc, axis=1, keepdims=True)
                o = jnp.dot(jnp.exp(sc - m).astype(BF16), jnp.where(own, vc, one),
                            preferred_element_type=F32)
                if use_prev:
                    o = o + jnp.dot(jnp.exp(sp - m).astype(BF16), jnp.where(own, vp, one),
                                    preferred_element_type=F32)
                res.append((m, o))
            (m0, o0), (m1, o1) = res
            l = pltpu.roll(jnp.where(head0, o1, o0), HEAD_DIM, 1)
            on[p, rows, :] = jnp.where(head0, o0, o1) / l
            ls[p, rows, :] = jnp.where(head0, m0, m1) + jnp.log(l)
            return carry

        lax.fori_loop(0, nblk, block, 0, unroll=8)

    for p, d in enumerate(dils):
        if d == 1:
            continue
        cls = s_len // d
        for r in range(d):
            src = pl.ds(r * cls, cls)
            dst = pl.ds(r, cls, stride=d)
            ont[p - 1, dst, :] = on[p, src, :]
            lst[p - 1, dst, :] = ls[p, src, :]

    def merge(c, carry):
        rows = pl.ds(pl.multiple_of(c * Q_BLOCK, Q_BLOCK), Q_BLOCK)
        parts = []
        for p, d in enumerate(dils):
            if d == 1:
                parts.append((ls[p, rows, :], on[p, rows, :]))
            else:
                parts.append((lst[p - 1, rows, :], ont[p - 1, rows, :]))
        m = parts[0][0]
        for lp, _ in parts[1:]:
            m = jnp.maximum(m, lp)
        num = jnp.zeros((Q_BLOCK, LANES), F32)
        den = jnp.zeros((Q_BLOCK, LANES), F32)
        for lp, op in parts:
            e = jnp.exp(lp - m)
            num = num + op * e
            den = den + e
        o_ref[rows, :] = num / den
        return carry

    lax.fori_loop(0, nblk, merge, 0, unroll=2)


def _prompt_attention(q, k, v, batch, s_len):
    t, width = q.shape
    dils = tuple(d for _, d in DILATED)
    spans = {w // d for w, d in DILATED}
    assert len(spans) == 1 and dils[0] == 1
    span = spans.pop()
    assert span == Q_BLOCK and all(s_len % (d * Q_BLOCK) == 0 for d in dils)
    npat = len(dils)
    nblk = s_len // Q_BLOCK
    spec = pl.BlockSpec((s_len, LANES), lambda b, hp: (b, hp))
    res_bf = pltpu.VMEM((npat, s_len, LANES), BF16)
    res_f = pltpu.VMEM((npat, s_len, LANES), F32)
    tok_f = pltpu.VMEM((npat - 1, s_len, LANES), F32)
    return pl.pallas_call(
        functools.partial(_pattn_body, dils=dils, span=span),
        grid=(batch, width // LANES),
        in_specs=[spec, spec, spec],
        out_specs=spec,
        out_shape=jax.ShapeDtypeStruct((t, width), F32),
        scratch_shapes=[res_bf, pltpu.VMEM((npat, nblk, LANES, Q_BLOCK), BF16), res_bf,
                        res_f, res_f, tok_f, tok_f],
        compiler_params=pltpu.CompilerParams(
            dimension_semantics=("parallel", "parallel"), vmem_limit_bytes=VMEM_LIMIT),
        name="prompt_attn",
    )(q, k, v)


def _sattn_body(q_ref, kn_ref, vn_ref, kt_ref, vt_ref, mc_ref, mn_ref, o_ref):
    t_len, width = q_ref.shape
    n_heads = width // HEAD_DIM
    rows = n_heads * t_len
    q = q_ref[...]
    q_rep = jnp.concatenate([q] * n_heads, axis=0)
    row_h = lax.broadcasted_iota(jnp.int32, (rows, width), 0) // t_len
    lane_h = lax.broadcasted_iota(jnp.int32, (rows, width), 1) // HEAD_DIM
    own = row_h == lane_h
    q_exp = jnp.where(own, q_rep, 0.0).astype(BF16)

    pad = jnp.zeros((mn_ref.shape[1] - t_len, width), F32)
    kn = jnp.concatenate([kn_ref[...], pad], axis=0).astype(BF16)
    vn = jnp.concatenate([vn_ref[...], pad], axis=0).astype(BF16)

    s = jnp.dot(q_exp, kt_ref[0].astype(BF16), preferred_element_type=F32)
    sn = lax.dot_general(q_exp, kn, NT_DIMS, preferred_element_type=F32)
    mc = mc_ref[...]
    mn = mn_ref[...]
    s = jnp.where(mc > 0, s, NEG)
    sn = jnp.where(mn > 0, sn, NEG)
    m = jnp.maximum(jnp.max(s, axis=1, keepdims=True), jnp.max(sn, axis=1, keepdims=True))
    p = mc * jnp.exp(s - m)
    pn = mn * jnp.exp(sn - m)
    l = jnp.sum(p, axis=1, keepdims=True) + jnp.sum(pn, axis=1, keepdims=True)
    o = lax.dot_general(p.astype(BF16), vt_ref[0].astype(BF16), NT_DIMS,
                        preferred_element_type=F32)
    o = o + jnp.dot(pn.astype(BF16), vn, preferred_element_type=F32)
    o = jnp.where(own, o / l, 0.0)
    out = o[0:t_len]
    for h in range(1, n_heads):
        out = out + o[h * t_len:(h + 1) * t_len]
    o_ref[...] = out


def _sample_mult(t_len, buf_len):
    mult = np.zeros((t_len, buf_len + t_len), np.float32)
    for w, d in DILATED:
        for t in range(t_len):
            for j in range(w // d + 1):
                idx = buf_len + t - d * j
                if idx >= 0:
                    mult[t, idx] += 1.0
    return mult


def _sample_attention(q, kn, vn, kt, vt, t_len):
    t, width = q.shape
    batch, _, buf_len = kt.shape
    n_heads = width // HEAD_DIM
    mult = _sample_mult(t_len, buf_len)
    mc = np.tile(mult[:, :buf_len], (n_heads, 1))
    mn = np.zeros((n_heads * t_len, LANES), np.float32)
    mn[:, :t_len] = np.tile(mult[:, buf_len:], (n_heads, 1))
    row_spec = pl.BlockSpec((t_len, width), lambda b: (b, 0))
    cache_spec = pl.BlockSpec((1, width, buf_len), lambda b: (b, 0, 0))
    return pl.pallas_call(
        _sattn_body,
        grid=(batch,),
        in_specs=[row_spec, row_spec, row_spec, cache_spec, cache_spec,
                  pl.BlockSpec(mc.shape, lambda b: (0, 0)),
                  pl.BlockSpec(mn.shape, lambda b: (0, 0))],
        out_specs=row_spec,
        out_shape=jax.ShapeDtypeStruct((t, width), F32),
        compiler_params=pltpu.CompilerParams(
            dimension_semantics=("parallel",), vmem_limit_bytes=VMEM_LIMIT),
        name="sample_attn",
    )(q, kn, vn, kt, vt, jnp.asarray(mc), jnp.asarray(mn))


def _ln_swish(y, g, b):
    mu = jnp.mean(y, axis=-1, keepdims=True)
    yc = y - mu
    var = jnp.mean(yc * yc, axis=-1, keepdims=True)
    return _silu(yc * lax.rsqrt(var + EPS) * g + b)


CONV_ROWS = 16
SUBLANES = 8


def _conv_out_prompt_body(ucur_ref, uprev_ref, attn_ref, h_ref, w_ref, b_ref, g_ref, beta_ref,
                          wo_ref, o_ref, ext_ref, sh_ref, wb_ref, conv_ref, *, tiles_per_seq):
    tm, ch = ucur_ref.shape
    halo = uprev_ref.shape[0]
    taps = w_ref.shape[0]
    first = (pl.program_id(0) % tiles_per_seq) == 0
    ext_ref[0:halo, :] = jnp.where(first, 0.0, uprev_ref[...])
    ext_ref[halo:halo + tm, :] = ucur_ref[...]
    sh_rows = sh_ref.shape[1]
    for s in range(1, SUBLANES):
        sh_ref[s - 1] = ext_ref[pl.ds(s, sh_rows), :]
    for w in range(taps):
        wb_ref[w] = jnp.broadcast_to(w_ref[w:w + 1, :], (SUBLANES, ch))
    off = halo - (taps - 1)
    groups = CONV_ROWS // SUBLANES
    for c in range(tm // CONV_ROWS):
        accs = [jnp.zeros((SUBLANES, ch), F32) for _ in range(groups)]
        for w in range(taps):
            s = (off + w) % SUBLANES
            wb = wb_ref[w]
            for g in range(groups):
                base = c * CONV_ROWS + g * SUBLANES + off + w - s
                x = (ext_ref[pl.ds(base, SUBLANES), :] if s == 0
                     else sh_ref[s - 1, pl.ds(base, SUBLANES), :])
                accs[g] = accs[g] + x * wb
        acc = jnp.concatenate(accs, axis=0)
        act = _ln_swish(acc + b_ref[...], g_ref[...], beta_ref[...])
        conv_ref[c * CONV_ROWS:(c + 1) * CONV_ROWS, :] = act.astype(BF16)
    aw = attn_ref.shape[1]
    o_ref[...] = (h_ref[...]
                  + jnp.dot(attn_ref[...].astype(BF16), wo_ref[0:aw, :], preferred_element_type=F32)
                  + jnp.dot(conv_ref[...], wo_ref[aw:aw + ch, :], preferred_element_type=F32))


def _conv_out_prompt(u, attn, h, dw_w, dw_b, ln_g, ln_b, wo, s_len, tm=256, halo=32):
    t, ch = u.shape
    d = h.shape[1]
    aw = attn.shape[1]
    taps = dw_w.shape[0]
    assert taps - 1 <= halo and s_len % tm == 0 and tm % halo == 0 and halo % SUBLANES == 0
    hb = tm // halo
    vec = pl.BlockSpec((1, ch), lambda i: (0, 0))
    return pl.pallas_call(
        functools.partial(_conv_out_prompt_body, tiles_per_seq=s_len // tm),
        grid=(t // tm,),
        in_specs=[
            pl.BlockSpec((tm, ch), lambda i: (i, 0)),
            pl.BlockSpec((halo, ch), lambda i: (jnp.maximum(i * hb - 1, 0), 0)),
            pl.BlockSpec((tm, aw), lambda i: (i, 0)),
            pl.BlockSpec((tm, d), lambda i: (i, 0)),
            pl.BlockSpec((taps, ch), lambda i: (0, 0)),
            vec, vec, vec,
            pl.BlockSpec((aw + ch, d), lambda i: (0, 0)),
        ],
        out_specs=pl.BlockSpec((tm, d), lambda i: (i, 0)),
        out_shape=jax.ShapeDtypeStruct((t, d), F32),
        scratch_shapes=[pltpu.VMEM((halo + tm, ch), F32),
                        pltpu.VMEM((SUBLANES - 1, halo + tm - SUBLANES, ch), F32),
                        pltpu.VMEM((taps, SUBLANES, ch), F32),
                        pltpu.VMEM((tm, ch), BF16)],
        compiler_params=pltpu.CompilerParams(
            dimension_semantics=("parallel",), vmem_limit_bytes=VMEM_LIMIT),
        name="conv_out_prompt",
    )(u, u, attn, h, dw_w, dw_b, ln_g, ln_b, wo)


def _conv_out_sample_body(u_ref, st_ref, attn_ref, h_ref, w_ref, b_ref, g_ref, beta_ref, wo_ref,
                          o_ref, ns_ref, ext_ref, conv_ref, at_ref):
    n_state, batch, ch = st_ref.shape
    t_len = u_ref.shape[0]
    taps = w_ref.shape[0]
    ext_ref[0:n_state] = st_ref[...]
    ext_ref[n_state:n_state + t_len] = u_ref[...]
    ns_ref[...] = ext_ref[t_len:t_len + n_state]
    for t in range(t_len):
        acc = jnp.zeros((batch, ch), F32)
        for w in range(taps):
            acc = acc + ext_ref[t + w] * w_ref[w:w + 1, :]
        act = _ln_swish(acc + b_ref[...], g_ref[...], beta_ref[...])
        conv_ref[t * batch:(t + 1) * batch, :] = act.astype(BF16)
        at_ref[t * batch:(t + 1) * batch, :] = attn_ref[t].astype(BF16)
    aw = attn_ref.shape[2]
    proj = (jnp.dot(at_ref[...], wo_ref[0:aw, :], preferred_element_type=F32)
            + jnp.dot(conv_ref[...], wo_ref[aw:aw + ch, :], preferred_element_type=F32))
    for t in range(t_len):
        o_ref[t] = h_ref[t] + proj[t * batch:(t + 1) * batch, :]


def _conv_out_sample(u_t, state_t, attn_t, h_t, dw_w, dw_b, ln_g, ln_b, wo):
    t_len, batch, ch = u_t.shape
    d = h_t.shape[2]
    n_state = state_t.shape[0]
    assert dw_w.shape[0] == n_state + 1
    return pl.pallas_call(
        _conv_out_sample_body,
        out_shape=[jax.ShapeDtypeStruct((t_len, batch, d), F32),
                   jax.ShapeDtypeStruct((n_state, batch, ch), F32)],
        scratch_shapes=[pltpu.VMEM((n_state + t_len, batch, ch), F32),
                        pltpu.VMEM((t_len * batch, ch), BF16),
                        pltpu.VMEM((t_len * batch, attn_t.shape[2]), BF16)],
        compiler_params=pltpu.CompilerParams(vmem_limit_bytes=VMEM_LIMIT),
        name="conv_out_sample",
    )(u_t, state_t, attn_t, h_t, dw_w, dw_b, ln_g, ln_b, wo)


def _rope_tables(positions):
    half = HEAD_DIM // 2
    inv = ROPE_THETA ** (-np.arange(half, dtype=np.float64) / half)
    ang = np.asarray(positions, np.float64)[:, None] * inv[None, :]
    cos, sin = np.cos(ang), np.sin(ang)
    reps = LANES // HEAD_DIM
    cos_t = np.tile(np.concatenate([cos, cos], axis=1), (1, reps))
    sin_t = np.tile(np.concatenate([-sin, sin], axis=1), (1, reps))
    return jnp.asarray(cos_t, F32), jnp.asarray(sin_t, F32)


def kernel(x_prompt, x_sample, cache_k, cache_v, state_conv, ln_ffn1, ffn1_w_gate, ffn1_w_up,
           ffn1_w_down, ln_mix, w_in, q_norm, k_norm, conv_dw_w, conv_dw_b, conv_ln_g, conv_ln_b,
           w_out, ln_ffn2, ffn2_w_gate, ffn2_w_up, ffn2_w_down):
    batch, s_len, d_model = x_prompt.shape
    dec_batch, t_len, _ = x_sample.shape
    depth, _, buf_len, n_heads, head_dim = cache_k.shape
    assert depth == 1 and head_dim == HEAD_DIM
    width = n_heads * head_dim
    ch = conv_dw_w.shape[2]

    xp = x_prompt.reshape(batch * s_len, d_model)
    xs = x_sample.reshape(dec_batch * t_len, d_model)
    ts = dec_batch * t_len

    bf = lambda a: a[0].astype(BF16)
    wi, wo = bf(w_in), bf(w_out)
    reps = width // HEAD_DIM
    gq = jnp.tile(q_norm, (1, reps))
    gk = jnp.tile(k_norm, (1, reps))
    heads_per_tile = MXU_DIM // HEAD_DIM
    pm = jnp.asarray(np.kron(np.eye(heads_per_tile), np.ones((HEAD_DIM, HEAD_DIM))), BF16)

    cos_p, sin_p = _rope_tables(np.arange(s_len))
    cos_s, sin_s = _rope_tables(np.tile(PAST_LEN + np.arange(t_len), dec_batch))

    hp, hs = _ffn(xp, xs, ln_ffn1, ffn1_w_gate[0], ffn1_w_up[0], ffn1_w_down[0])

    qp, kp, vp, up, kt_p, vt_p = _inproj(hp, ln_mix, wi, gq, gk, cos_p, sin_p, pm, tm=512,
                                         seq_len=s_len)
    qs, ks, vs, us = _inproj(hs, ln_mix, wi, gq, gk, cos_s, sin_s, pm, tm=ts)

    attn_p = _prompt_attention(qp, kp, vp, batch, s_len)
    kt = jnp.transpose(cache_k[0], (0, 2, 3, 1)).reshape(dec_batch, width, buf_len)
    vt = jnp.transpose(cache_v[0], (0, 2, 3, 1)).reshape(dec_batch, width, buf_len)
    attn_s = _sample_attention(qs, ks, vs, kt, vt, t_len)

    hp = _conv_out_prompt(up, attn_p, hp, conv_dw_w[0], conv_dw_b, conv_ln_g, conv_ln_b, wo, s_len)
    state_t = jnp.transpose(state_conv[0], (1, 0, 2))
    time_major = lambda a: jnp.transpose(a.reshape(dec_batch, t_len, a.shape[-1]), (1, 0, 2))
    hs_t, new_state_t = _conv_out_sample(time_major(us), state_t, time_major(attn_s),
                                         time_major(hs), conv_dw_w[0], conv_dw_b,
                                         conv_ln_g, conv_ln_b, wo)
    hs = jnp.transpose(hs_t, (1, 0, 2)).reshape(ts, d_model)

    yp, ys = _ffn(hp, hs, ln_ffn2, ffn2_w_gate[0], ffn2_w_up[0], ffn2_w_down[0])

    n_state = state_conv.shape[2]
    seq_major = lambda a: jnp.transpose(a.reshape(batch, n_heads, head_dim, s_len), (0, 3, 1, 2))[None]
    kv_s = (1, dec_batch, t_len, n_heads, head_dim)
    return (yp.reshape(batch, s_len, d_model),
            ys.reshape(dec_batch, t_len, d_model),
            seq_major(kt_p), seq_major(vt_p),
            up.reshape(batch, s_len, ch)[:, s_len - n_state:][None],
            ks.reshape(kv_s), vs.reshape(kv_s),
            jnp.transpose(new_state_t, (1, 0, 2))[None])
```

```python
import functools

import numpy as np
import jax
import jax.numpy as jnp
from jax import lax
from jax.experimental import pallas as pl
from jax.experimental.pallas import tpu as pltpu

F32 = jnp.float32
BF16 = jnp.bfloat16

HEAD_DIM = 64
PAST_LEN = 16384
DILATED = ((128, 1), (512, 4), (2048, 16))
Q_BLOCK = 128
ROPE_THETA = 10000.0
EPS = 1e-6
FFN_RES = 0.5
NEG = -1e30

LANES = 128
MXU_DIM = 256
VMEM_LIMIT = 56 * 1024 * 1024
FFN_VMEM_LIMIT = 60 * 1024 * 1024

NT_DIMS = (((1,), (1,)), ((), ()))


def _silu(x):
    return x * jax.nn.sigmoid(x)


def _rms_rows(x, gain):
    ms = jnp.mean(x * x, axis=-1, keepdims=True)
    return x * lax.rsqrt(ms + EPS) * gain


FFN_ROW_CHUNK = 512


def _ffn_body(xp_ref, xs_ref, g_ref, wg_ref, wu_ref, wd_ref, op_ref, os_ref,
              h_ref, wgb_ref, wub_ref, wdb_ref):
    tp = xp_ref.shape[0]
    ts = xs_ref.shape[0]

    @pl.when(pl.program_id(1) == 0)
    def _():
        for r0 in range(0, tp, FFN_ROW_CHUNK):
            x = xp_ref[r0:r0 + FFN_ROW_CHUNK, :]
            h_ref[r0:r0 + FFN_ROW_CHUNK, :] = _rms_rows(x, g_ref[...]).astype(BF16)
            op_ref[r0:r0 + FFN_ROW_CHUNK, :] = x
        x = xs_ref[...]
        h_ref[tp:tp + ts, :] = _rms_rows(x, g_ref[...]).astype(BF16)
        os_ref[...] = x

    wgb_ref[...] = wg_ref[...].astype(BF16)
    wub_ref[...] = wu_ref[...].astype(BF16)
    wdb_ref[...] = wd_ref[...].astype(BF16)

    def half_step(h):
        g = jnp.dot(h, wgb_ref[...], preferred_element_type=F32)
        u = jnp.dot(h, wub_ref[...], preferred_element_type=F32)
        a = (_silu(g) * u * FFN_RES).astype(BF16)
        return jnp.dot(a, wdb_ref[...], preferred_element_type=F32)

    chunk = (tp + ts) // 2
    op_ref[0:chunk, :] += half_step(h_ref[0:chunk, :])
    res = half_step(h_ref[chunk:tp + ts, :])
    op_ref[chunk:tp, :] += res[0:tp - chunk, :]
    os_ref[...] += res[tp - chunk:chunk, :]


def _ffn(xp, xs, gain, wg, wu, wd, tiles=8, tf=256):
    tp_all, d = xp.shape
    ts_all = xs.shape[0]
    dff = wg.shape[1]
    tp, ts = tp_all // tiles, ts_all // tiles
    assert tp % FFN_ROW_CHUNK == 0 and ts % 16 == 0 and dff % tf == 0 and (tp + ts) % 32 == 0
    row = lambda rows: pl.BlockSpec((rows, d), lambda i, f: (i, 0))
    row_in = lambda rows: pl.BlockSpec(
        (rows, d), lambda i, f: (jnp.where(f == 0, i, jnp.minimum(i + 1, tiles - 1)), 0))
    return pl.pallas_call(
        _ffn_body,
        grid=(tiles, dff // tf),
        in_specs=[
            row_in(tp), row_in(ts),
            pl.BlockSpec((1, d), lambda i, f: (0, 0)),
            pl.BlockSpec((d, tf), lambda i, f: (0, f)),
            pl.BlockSpec((d, tf), lambda i, f: (0, f)),
            pl.BlockSpec((tf, d), lambda i, f: (f, 0)),
        ],
        out_specs=[row(tp), row(ts)],
        out_shape=[jax.ShapeDtypeStruct((tp_all, d), F32), jax.ShapeDtypeStruct((ts_all, d), F32)],
        scratch_shapes=[pltpu.VMEM((tp + ts, d), BF16), pltpu.VMEM((d, tf), BF16),
                        pltpu.VMEM((d, tf), BF16), pltpu.VMEM((tf, d), BF16)],
        compiler_params=pltpu.CompilerParams(
            dimension_semantics=("parallel", "arbitrary"), vmem_limit_bytes=FFN_VMEM_LIMIT),
        name="ffn",
    )(xp, xs, gain, wg, wu, wd)


def _inproj_body(x_ref, g_ref, w_ref, gq_ref, gk_ref, cos_ref, sin_ref, pm_ref,
                 q_ref, k_ref, v_ref, u_ref, *rest, transposed_kv):
    if transposed_kv:
        kt_ref, vt_ref, hn_ref, z_ref = rest
    else:
        kt_ref = vt_ref = None
        hn_ref, z_ref = rest
    j = pl.program_id(1)
    tm, width = q_ref.shape
    lane = lax.broadcasted_iota(jnp.int32, (tm, LANES), 1)
    first_half = (lane % HEAD_DIM) < (HEAD_DIM // 2)

    def matmul(slot, c=None):
        sl = slice(None) if c is None else slice(c * MXU_DIM, (c + 1) * MXU_DIM)
        z_ref[slot, :, sl] = jnp.dot(hn_ref[...], w_ref[:, sl], preferred_element_type=F32)

    def norm_rope(slot, gain_ref, out_ref, t_ref, scale):
        pm = pm_ref[...]
        for c in range(width // MXU_DIM):
            sl = slice(c * MXU_DIM, (c + 1) * MXU_DIM)
            matmul(1 - slot, c)
            z = z_ref[slot, :, sl]
            zz = z * z
            hi = zz.astype(BF16)
            lo = (zz - hi.astype(F32)).astype(BF16)
            ss = (jnp.dot(hi, pm, preferred_element_type=F32)
                  + jnp.dot(lo, pm, preferred_element_type=F32))
            zn = z * lax.rsqrt(ss * (1.0 / HEAD_DIM) + EPS) * gain_ref[:, sl]
            for e in range(MXU_DIM // LANES):
                x = zn[:, e * LANES:(e + 1) * LANES]
                partner = jnp.where(first_half,
                                    pltpu.roll(x, LANES - HEAD_DIM // 2, 1),
                                    pltpu.roll(x, HEAD_DIM // 2, 1))
                y = (x * cos_ref[...] + partner * sin_ref[...]) * scale
                lo_lane = c * MXU_DIM + e * LANES
                out_ref[:, lo_lane:lo_lane + LANES] = y
                if t_ref is not None:
                    t_ref[0, lo_lane:lo_lane + LANES, :] = y.T

    @pl.when(j == 0)
    def _():
        hn_ref[...] = _rms_rows(x_ref[...], g_ref[...]).astype(BF16)
        matmul(0)

    @pl.when(j == 1)
    def _():
        norm_rope(0, gq_ref, q_ref, None, HEAD_DIM ** -0.5)

    @pl.when(j == 2)
    def _():
        norm_rope(1, gk_ref, k_ref, kt_ref, 1.0)

    @pl.when(j == 3)
    def _():
        for c in range(width // MXU_DIM):
            sl = slice(c * MXU_DIM, (c + 1) * MXU_DIM)
            matmul(1, c)
            v = z_ref[0, :, sl]
            v_ref[:, sl] = v
            if vt_ref is not None:
                vt_ref[0, sl, :] = v.T

    @pl.when(j == 4)
    def _():
        matmul(0)

    @pl.when(j == 5)
    def _():
        u_ref[...] = z_ref[1] * jax.nn.sigmoid(z_ref[0])


def _inproj(x, gain, w, gq, gk, cos_t, sin_t, pm, tm, seq_len=None):
    t, d = x.shape
    groups = 5
    width = w.shape[1] // groups
    period_blocks = cos_t.shape[0] // tm
    tab_spec = pl.BlockSpec((tm, LANES), lambda i, j: (i % period_blocks, 0))
    def tile_at(i, j, ready):
        return jnp.where(j >= ready, i, jnp.maximum(i - 1, 0))

    def rows_spec(ready):
        return pl.BlockSpec((tm, width), lambda i, j: (tile_at(i, j, ready), 0))

    ready_q, ready_k, ready_v, ready_u = 1, 2, 3, groups
    out_specs = [rows_spec(ready_q), rows_spec(ready_k), rows_spec(ready_v), rows_spec(ready_u)]
    out_shape = [jax.ShapeDtypeStruct((t, width), F32)] * 4
    if seq_len is not None:
        tiles_per_seq = seq_len // tm

        def cols_spec(ready):
            def index_map(i, j):
                tile = tile_at(i, j, ready)
                return (tile // tiles_per_seq, 0, tile % tiles_per_seq)
            return pl.BlockSpec((1, width, tm), index_map)

        out_specs += [cols_spec(ready_k), cols_spec(ready_v)]
        out_shape += [jax.ShapeDtypeStruct((t // seq_len, width, seq_len), F32)] * 2
    return pl.pallas_call(
        functools.partial(_inproj_body, transposed_kv=seq_len is not None),
        grid=(t // tm, groups + 1),
        in_specs=[
            pl.BlockSpec((tm, d), lambda i, j: (jnp.where(j == 0, i, jnp.minimum(i + 1, t // tm - 1)), 0)),
            pl.BlockSpec((1, d), lambda i, j: (0, 0)),
            pl.BlockSpec((d, width), lambda i, j: (0, jnp.minimum(j, groups - 1))),
            pl.BlockSpec((1, width), lambda i, j: (0, 0)),
            pl.BlockSpec((1, width), lambda i, j: (0, 0)),
            tab_spec, tab_spec,
            pl.BlockSpec((MXU_DIM, MXU_DIM), lambda i, j: (0, 0)),
        ],
        out_specs=out_specs,
        out_shape=out_shape,
        scratch_shapes=[pltpu.VMEM((tm, d), BF16), pltpu.VMEM((2, tm, width), F32)],
        compiler_params=pltpu.CompilerParams(
            dimension_semantics=("arbitrary", "arbitrary"), vmem_limit_bytes=VMEM_LIMIT),
        name="inproj",
    )(x, gain, w, gq, gk, cos_t, sin_t, pm)


def _pattn_body(q_ref, k_ref, v_ref, o_ref, kt, on, ls, *, dils, span):
    s_len = q_ref.shape[0]
    nblk = s_len // Q_BLOCK

    def block_rows(n, d, bpc):
        if d == 1:
            return pl.ds(pl.multiple_of(n * Q_BLOCK, Q_BLOCK), Q_BLOCK)
        return pl.ds(n // bpc + (n % bpc) * (Q_BLOCK * d), Q_BLOCK, stride=d)

    for p, d in enumerate(dils):
        bpc = (s_len // d) // Q_BLOCK
        for n in range(nblk):
            kt[p, n] = k_ref[block_rows(n, d, bpc), :].T.astype(BF16)

    lane = lax.broadcasted_iota(jnp.int32, (Q_BLOCK, LANES), 1)
    head0 = lane < HEAD_DIM
    qi = lax.broadcasted_iota(jnp.int32, (Q_BLOCK, Q_BLOCK), 0)
    ki = lax.broadcasted_iota(jnp.int32, (Q_BLOCK, Q_BLOCK), 1)
    causal = ki <= qi
    band = (qi + Q_BLOCK - ki) <= span
    one = jnp.ones((), BF16)
    zero = jnp.zeros((), BF16)

    for p, d in enumerate(dils):
        bpc = (s_len // d) // Q_BLOCK
        use_prev = bpc > 1

        def block(n, carry, p=p, d=d, bpc=bpc, use_prev=use_prev):
            rows = block_rows(n, d, bpc)
            q = q_ref[rows, :].astype(BF16)
            vc = v_ref[rows, :].astype(BF16)
            ktc = kt[p, n]
            if use_prev:
                n_prev = jnp.maximum(n - 1, 0)
                ktp = kt[p, n_prev]
                vp = v_ref[block_rows(n_prev, d, bpc), :].astype(BF16)
                prev_ok = band & ((n % bpc) != 0)
            res = []
            for h in range(2):
                own = head0 if h == 0 else ~head0
                qh = jnp.where(own, q, zero)
                sc = jnp.where(causal, jnp.dot(qh, ktc, preferred_element_type=F32), NEG)
                if use_prev:
                    sp = jnp.where(prev_ok, jnp.dot(qh, ktp, preferred_element_type=F32), NEG)
                    m = jnp.max(jnp.maximum(sc, sp), axis=1, keepdims=True)
                else:
                    m = jnp.max(sc, axis=1, keepdims=True)
                o = jnp.dot(jnp.exp(sc - m).astype(BF16), jnp.where(own, vc, one),
                            preferred_element_type=F32)
                if use_prev:
                    o = o + jnp.dot(jnp.exp(sp - m).astype(BF16), jnp.where(own, vp, one),
                                    preferred_element_type=F32)
                res.append((m, o))
            (m0, o0), (m1, o1) = res
            l = pltpu.roll(jnp.where(head0, o1, o0), HEAD_DIM, 1)
            on[p, rows, :] = jnp.where(head0, o0, o1) / l
            ls[p, rows, :] = jnp.where(head0, m0, m1) + jnp.log(l)
            return carry

        lax.fori_loop(0, nblk, block, 0, unroll=8)

    def merge(c, carry):
        rows = pl.ds(pl.multiple_of(c * Q_BLOCK, Q_BLOCK), Q_BLOCK)
        lses = [ls[p, rows, :] for p in range(len(dils))]
        m = lses[0]
        for lp in lses[1:]:
            m = jnp.maximum(m, lp)
        num = jnp.zeros((Q_BLOCK, LANES), F32)
        den = jnp.zeros((Q_BLOCK, LANES), F32)
        for p, lp in enumerate(lses):
            e = jnp.exp(lp - m)
            num = num + on[p, rows, :] * e
            den = den + e
        o_ref[rows, :] = num / den
        return carry

    lax.fori_loop(0, nblk, merge, 0, unroll=2)


def _prompt_attention(q, k, v, batch, s_len):
    t, width = q.shape
    dils = tuple(d for _, d in DILATED)
    spans = {w // d for w, d in DILATED}
    assert len(spans) == 1 and dils[0] == 1
    span = spans.pop()
    assert span == Q_BLOCK and all(s_len % (d * Q_BLOCK) == 0 for d in dils)
    npat = len(dils)
    nblk = s_len // Q_BLOCK
    spec = pl.BlockSpec((s_len, LANES), lambda b, hp: (b, hp))
    tok_f = pltpu.VMEM((npat, s_len, LANES), F32)
    return pl.pallas_call(
        functools.partial(_pattn_body, dils=dils, span=span),
        grid=(batch, width // LANES),
        in_specs=[spec, spec, spec],
        out_specs=spec,
        out_shape=jax.ShapeDtypeStruct((t, width), F32),
        scratch_shapes=[pltpu.VMEM((npat, nblk, LANES, Q_BLOCK), BF16), tok_f, tok_f],
        compiler_params=pltpu.CompilerParams(
            dimension_semantics=("parallel", "parallel"), vmem_limit_bytes=VMEM_LIMIT),
        name="prompt_attn",
    )(q, k, v)


def _sattn_body(q_ref, kn_ref, vn_ref, kt_ref, vt_ref, mc_ref, mn_ref, o_ref):
    t_len, width = q_ref.shape
    n_heads = width // HEAD_DIM
    rows = n_heads * t_len
    q = q_ref[...]
    q_rep = jnp.concatenate([q] * n_heads, axis=0)
    row_h = lax.broadcasted_iota(jnp.int32, (rows, width), 0) // t_len
    lane_h = lax.broadcasted_iota(jnp.int32, (rows, width), 1) // HEAD_DIM
    own = row_h == lane_h
    q_exp = jnp.where(own, q_rep, 0.0).astype(BF16)

    pad = jnp.zeros((mn_ref.shape[1] - t_len, width), F32)
    kn = jnp.concatenate([kn_ref[...], pad], axis=0).astype(BF16)
    vn = jnp.concatenate([vn_ref[...], pad], axis=0).astype(BF16)

    s = jnp.dot(q_exp, kt_ref[0].astype(BF16), preferred_element_type=F32)
    sn = lax.dot_general(q_exp, kn, NT_DIMS, preferred_element_type=F32)
    mc = mc_ref[...]
    mn = mn_ref[...]
    s = jnp.where(mc > 0, s, NEG)
    sn = jnp.where(mn > 0, sn, NEG)
    m = jnp.maximum(jnp.max(s, axis=1, keepdims=True), jnp.max(sn, axis=1, keepdims=True))
    p = mc * jnp.exp(s - m)
    pn = mn * jnp.exp(sn - m)
    l = jnp.sum(p, axis=1, keepdims=True) + jnp.sum(pn, axis=1, keepdims=True)
    o = lax.dot_general(p.astype(BF16), vt_ref[0].astype(BF16), NT_DIMS,
                        preferred_element_type=F32)
    o = o + jnp.dot(pn.astype(BF16), vn, preferred_element_type=F32)
    o = jnp.where(own, o / l, 0.0)
    out = o[0:t_len]
    for h in range(1, n_heads):
        out = out + o[h * t_len:(h + 1) * t_len]
    o_ref[...] = out


def _sample_mult(t_len, buf_len):
    mult = np.zeros((t_len, buf_len + t_len), np.float32)
    for w, d in DILATED:
        for t in range(t_len):
            for j in range(w // d + 1):
                idx = buf_len + t - d * j
                if idx >= 0:
                    mult[t, idx] += 1.0
    return mult


def _sample_attention(q, kn, vn, kt, vt, t_len):
    t, width = q.shape
    batch, _, buf_len = kt.shape
    n_heads = width // HEAD_DIM
    mult = _sample_mult(t_len, buf_len)
    mc = np.tile(mult[:, :buf_len], (n_heads, 1))
    mn = np.zeros((n_heads * t_len, LANES), np.float32)
    mn[:, :t_len] = np.tile(mult[:, buf_len:], (n_heads, 1))
    row_spec = pl.BlockSpec((t_len, width), lambda b: (b, 0))
    cache_spec = pl.BlockSpec((1, width, buf_len), lambda b: (b, 0, 0))
    return pl.pallas_call(
        _sattn_body,
        grid=(batch,),
        in_specs=[row_spec, row_spec, row_spec, cache_spec, cache_spec,
                  pl.BlockSpec(mc.shape, lambda b: (0, 0)),
                  pl.BlockSpec(mn.shape, lambda b: (0, 0))],
        out_specs=row_spec,
        out_shape=jax.ShapeDtypeStruct((t, width), F32),
        compiler_params=pltpu.CompilerParams(
            dimension_semantics=("parallel",), vmem_limit_bytes=VMEM_LIMIT),
        name="sample_attn",
    )(q, kn, vn, kt, vt, jnp.asarray(mc), jnp.asarray(mn))


def _ln_swish(y, g, b):
    mu = jnp.mean(y, axis=-1, keepdims=True)
    yc = y - mu
    var = jnp.mean(yc * yc, axis=-1, keepdims=True)
    return _silu(yc * lax.rsqrt(var + EPS) * g + b)


CONV_ROWS = 16
SUBLANES = 8


def _conv_out_prompt_body(ucur_ref, uprev_ref, attn_ref, h_ref, w_ref, b_ref, g_ref, beta_ref,
                          wo_ref, o_ref, ext_ref, sh_ref, wb_ref, conv_ref, *, tiles_per_seq):
    tm, ch = ucur_ref.shape
    halo = uprev_ref.shape[0]
    taps = w_ref.shape[0]
    first = (pl.program_id(0) % tiles_per_seq) == 0
    ext_ref[0:halo, :] = jnp.where(first, 0.0, uprev_ref[...])
    ext_ref[halo:halo + tm, :] = ucur_ref[...]
    sh_rows = sh_ref.shape[1]
    for s in range(1, SUBLANES):
        sh_ref[s - 1] = ext_ref[pl.ds(s, sh_rows), :]
    for w in range(taps):
        wb_ref[w] = jnp.broadcast_to(w_ref[w:w + 1, :], (SUBLANES, ch))
    off = halo - (taps - 1)
    groups = CONV_ROWS // SUBLANES

    for c in range(tm // CONV_ROWS):
        accs = [jnp.zeros((SUBLANES, ch), F32) for _ in range(groups)]
        for w in range(taps):
            s = (off + w) % SUBLANES
            wb = wb_ref[w]
            for g in range(groups):
                base = c * CONV_ROWS + g * SUBLANES + off + w - s
                x = (ext_ref[pl.ds(base, SUBLANES), :] if s == 0
                     else sh_ref[s - 1, pl.ds(base, SUBLANES), :])
                accs[g] = accs[g] + x * wb
        acc = jnp.concatenate(accs, axis=0)
        act = _ln_swish(acc + b_ref[...], g_ref[...], beta_ref[...])
        conv_ref[c * CONV_ROWS:(c + 1) * CONV_ROWS, :] = act.astype(BF16)
    aw = attn_ref.shape[1]
    o_ref[...] = (h_ref[...]
                  + jnp.dot(attn_ref[...].astype(BF16), wo_ref[0:aw, :], preferred_element_type=F32)
                  + jnp.dot(conv_ref[...], wo_ref[aw:aw + ch, :], preferred_element_type=F32))


def _conv_out_prompt(u, attn, h, dw_w, dw_b, ln_g, ln_b, wo, s_len, tm=256, halo=32):
    t, ch = u.shape
    d = h.shape[1]
    aw = attn.shape[1]
    taps = dw_w.shape[0]
    assert taps - 1 <= halo and s_len % tm == 0 and tm % halo == 0 and halo % SUBLANES == 0
    hb = tm // halo
    vec = pl.BlockSpec((1, ch), lambda i: (0, 0))
    return pl.pallas_call(
        functools.partial(_conv_out_prompt_body, tiles_per_seq=s_len // tm),
        grid=(t // tm,),
        in_specs=[
            pl.BlockSpec((tm, ch), lambda i: (i, 0)),
            pl.BlockSpec((halo, ch), lambda i: (jnp.maximum(i * hb - 1, 0), 0)),
            pl.BlockSpec((tm, aw), lambda i: (i, 0)),
            pl.BlockSpec((tm, d), lambda i: (i, 0)),
            pl.BlockSpec((taps, ch), lambda i: (0, 0)),
            vec, vec, vec,
            pl.BlockSpec((aw + ch, d), lambda i: (0, 0)),
        ],
        out_specs=pl.BlockSpec((tm, d), lambda i: (i, 0)),
        out_shape=jax.ShapeDtypeStruct((t, d), F32),
        scratch_shapes=[pltpu.VMEM((halo + tm, ch), F32),
                        pltpu.VMEM((SUBLANES - 1, halo + tm - SUBLANES, ch), F32),
                        pltpu.VMEM((taps, SUBLANES, ch), F32),
                        pltpu.VMEM((tm, ch), BF16)],
        compiler_params=pltpu.CompilerParams(
            dimension_semantics=("parallel",), vmem_limit_bytes=VMEM_LIMIT),
        name="conv_out_prompt",
    )(u, u, attn, h, dw_w, dw_b, ln_g, ln_b, wo)


def _conv_out_sample_body(u_ref, st_ref, attn_ref, h_ref, w_ref, b_ref, g_ref, beta_ref, wo_ref,
                          o_ref, ns_ref, ext_ref, conv_ref, at_ref):
    n_state, batch, ch = st_ref.shape
    t_len = u_ref.shape[0]
    taps = w_ref.shape[0]
    ext_ref[0:n_state] = st_ref[...]
    ext_ref[n_state:n_state + t_len] = u_ref[...]
    ns_ref[...] = ext_ref[t_len:t_len + n_state]
    for t in range(t_len):
        acc = jnp.zeros((batch, ch), F32)
        for w in range(taps):
            acc = acc + ext_ref[t + w] * w_ref[w:w + 1, :]
        act = _ln_swish(acc + b_ref[...], g_ref[...], beta_ref[...])
        conv_ref[t * batch:(t + 1) * batch, :] = act.astype(BF16)
        at_ref[t * batch:(t + 1) * batch, :] = attn_ref[t].astype(BF16)
    aw = attn_ref.shape[2]
    proj = (jnp.dot(at_ref[...], wo_ref[0:aw, :], preferred_element_type=F32)
            + jnp.dot(conv_ref[...], wo_ref[aw:aw + ch, :], preferred_element_type=F32))
    for t in range(t_len):
        o_ref[t] = h_ref[t] + proj[t * batch:(t + 1) * batch, :]


def _conv_out_sample(u_t, state_t, attn_t, h_t, dw_w, dw_b, ln_g, ln_b, wo):
    t_len, batch, ch = u_t.shape
    d = h_t.shape[2]
    n_state = state_t.shape[0]
    assert dw_w.shape[0] == n_state + 1
    return pl.pallas_call(
        _conv_out_sample_body,
        out_shape=[jax.ShapeDtypeStruct((t_len, batch, d), F32),
                   jax.ShapeDtypeStruct((n_state, batch, ch), F32)],
        scratch_shapes=[pltpu.VMEM((n_state + t_len, batch, ch), F32),
                        pltpu.VMEM((t_len * batch, ch), BF16),
                        pltpu.VMEM((t_len * batch, attn_t.shape[2]), BF16)],
        compiler_params=pltpu.CompilerParams(vmem_limit_bytes=VMEM_LIMIT),
        name="conv_out_sample",
    )(u_t, state_t, attn_t, h_t, dw_w, dw_b, ln_g, ln_b, wo)


def _rope_tables(positions):
    half = HEAD_DIM // 2
    inv = ROPE_THETA ** (-np.arange(half, dtype=np.float64) / half)
    ang = np.asarray(positions, np.float64)[:, None] * inv[None, :]
    cos, sin = np.cos(ang), np.sin(ang)
    reps = LANES // HEAD_DIM
    cos_t = np.tile(np.concatenate([cos, cos], axis=1), (1, reps))
    sin_t = np.tile(np.concatenate([-sin, sin], axis=1), (1, reps))
    return jnp.asarray(cos_t, F32), jnp.asarray(sin_t, F32)


def kernel(x_prompt, x_sample, cache_k, cache_v, state_conv, ln_ffn1, ffn1_w_gate, ffn1_w_up,
           ffn1_w_down, ln_mix, w_in, q_norm, k_norm, conv_dw_w, conv_dw_b, conv_ln_g, conv_ln_b,
           w_out, ln_ffn2, ffn2_w_gate, ffn2_w_up, ffn2_w_down):
    batch, s_len, d_model = x_prompt.shape
    dec_batch, t_len, _ = x_sample.shape
    depth, _, buf_len, n_heads, head_dim = cache_k.shape
    assert depth == 1 and head_dim == HEAD_DIM
    width = n_heads * head_dim
    ch = conv_dw_w.shape[2]

    xp = x_prompt.reshape(batch * s_len, d_model)
    xs = x_sample.reshape(dec_batch * t_len, d_model)
    ts = dec_batch * t_len

    bf = lambda a: a[0].astype(BF16)
    wi, wo = bf(w_in), bf(w_out)
    reps = width // HEAD_DIM
    gq = jnp.tile(q_norm, (1, reps))
    gk = jnp.tile(k_norm, (1, reps))
    heads_per_tile = MXU_DIM // HEAD_DIM
    pm = jnp.asarray(np.kron(np.eye(heads_per_tile), np.ones((HEAD_DIM, HEAD_DIM))), BF16)

    cos_p, sin_p = _rope_tables(np.arange(s_len))
    cos_s, sin_s = _rope_tables(np.tile(PAST_LEN + np.arange(t_len), dec_batch))

    hp, hs = _ffn(xp, xs, ln_ffn1, ffn1_w_gate[0], ffn1_w_up[0], ffn1_w_down[0])

    qp, kp, vp, up, kt_p, vt_p = _inproj(hp, ln_mix, wi, gq, gk, cos_p, sin_p, pm, tm=512,
                                         seq_len=s_len)
    qs, ks, vs, us = _inproj(hs, ln_mix, wi, gq, gk, cos_s, sin_s, pm, tm=ts)

    attn_p = _prompt_attention(qp, kp, vp, batch, s_len)
    kt = jnp.transpose(cache_k[0], (0, 2, 3, 1)).reshape(dec_batch, width, buf_len)
    vt = jnp.transpose(cache_v[0], (0, 2, 3, 1)).reshape(dec_batch, width, buf_len)
    attn_s = _sample_attention(qs, ks, vs, kt, vt, t_len)

    hp = _conv_out_prompt(up, attn_p, hp, conv_dw_w[0], conv_dw_b, conv_ln_g, conv_ln_b, wo, s_len)
    state_t = jnp.transpose(state_conv[0], (1, 0, 2))
    time_major = lambda a: jnp.transpose(a.reshape(dec_batch, t_len, a.shape[-1]), (1, 0, 2))
    hs_t, new_state_t = _conv_out_sample(time_major(us), state_t, time_major(attn_s),
                                         time_major(hs), conv_dw_w[0], conv_dw_b,
                                         conv_ln_g, conv_ln_b, wo)
    hs = jnp.transpose(hs_t, (1, 0, 2)).reshape(ts, d_model)

    yp, ys = _ffn(hp, hs, ln_ffn2, ffn2_w_gate[0], ffn2_w_up[0], ffn2_w_down[0])

    n_state = state_conv.shape[2]
    seq_major = lambda a: jnp.transpose(a.reshape(batch, n_heads, head_dim, s_len), (0, 3, 1, 2))[None]
    kv_s = (1, dec_batch, t_len, n_heads, head_dim)
    return (yp.reshape(batch, s_len, d_model),
            ys.reshape(dec_batch, t_len, d_model),
            seq_major(kt_p), seq_major(vt_p),
            up.reshape(batch, s_len, ch)[:, s_len - n_state:][None],
            ks.reshape(kv_s), vs.reshape(kv_s),
            jnp.transpose(new_state_t, (1, 0, 2))[None])
```

```python
import functools

import numpy as np
import jax
import jax.numpy as jnp
from jax import lax
from jax.experimental import pallas as pl
from jax.experimental.pallas import tpu as pltpu

F32 = jnp.float32
BF16 = jnp.bfloat16

HEAD_DIM = 64
PAST_LEN = 16384
DILATED = ((128, 1), (512, 4), (2048, 16))
Q_BLOCK = 128
ROPE_THETA = 10000.0
EPS = 1e-6
FFN_RES = 0.5
NEG = -1e30

LANES = 128
MXU_DIM = 256
VMEM_LIMIT = 56 * 1024 * 1024
FFN_VMEM_LIMIT = 60 * 1024 * 1024

NT_DIMS = (((1,), (1,)), ((), ()))


def _silu(x):
    return x * jax.nn.sigmoid(x)


def _rms_rows(x, gain):
    ms = jnp.mean(x * x, axis=-1, keepdims=True)
    return x * lax.rsqrt(ms + EPS) * gain


FFN_ROW_CHUNK = 512


def _ffn_body(xp_ref, xs_ref, g_ref, wg_ref, wu_ref, wd_ref, op_ref, os_ref,
              h_ref, wgb_ref, wub_ref, wdb_ref):
    tp = xp_ref.shape[0]
    ts = xs_ref.shape[0]

    @pl.when(pl.program_id(1) == 0)
    def _():
        for r0 in range(0, tp, FFN_ROW_CHUNK):
            x = xp_ref[r0:r0 + FFN_ROW_CHUNK, :]
            h_ref[r0:r0 + FFN_ROW_CHUNK, :] = _rms_rows(x, g_ref[...]).astype(BF16)
            op_ref[r0:r0 + FFN_ROW_CHUNK, :] = x
        x = xs_ref[...]
        h_ref[tp:tp + ts, :] = _rms_rows(x, g_ref[...]).astype(BF16)
        os_ref[...] = x

    wgb_ref[...] = wg_ref[...].astype(BF16)
    wub_ref[...] = wu_ref[...].astype(BF16)
    wdb_ref[...] = wd_ref[...].astype(BF16)

    def half_step(h):
        g = jnp.dot(h, wgb_ref[...], preferred_element_type=F32)
        u = jnp.dot(h, wub_ref[...], preferred_element_type=F32)
        a = (_silu(g) * u * FFN_RES).astype(BF16)
        return jnp.dot(a, wdb_ref[...], preferred_element_type=F32)

    chunk = (tp + ts) // 2
    op_ref[0:chunk, :] += half_step(h_ref[0:chunk, :])
    res = half_step(h_ref[chunk:tp + ts, :])
    op_ref[chunk:tp, :] += res[0:tp - chunk, :]
    os_ref[...] += res[tp - chunk:chunk, :]


def _ffn(xp, xs, gain, wg, wu, wd, tiles=8, tf=256):
    tp_all, d = xp.shape
    ts_all = xs.shape[0]
    dff = wg.shape[1]
    tp, ts = tp_all // tiles, ts_all // tiles
    assert tp % FFN_ROW_CHUNK == 0 and ts % 16 == 0 and dff % tf == 0 and (tp + ts) % 32 == 0
    row = lambda rows: pl.BlockSpec((rows, d), lambda i, f: (i, 0))
    row_in = lambda rows: pl.BlockSpec(
        (rows, d), lambda i, f: (jnp.where(f == 0, i, jnp.minimum(i + 1, tiles - 1)), 0))
    return pl.pallas_call(
        _ffn_body,
        grid=(tiles, dff // tf),
        in_specs=[
            row_in(tp), row_in(ts),
            pl.BlockSpec((1, d), lambda i, f: (0, 0)),
            pl.BlockSpec((d, tf), lambda i, f: (0, f)),
            pl.BlockSpec((d, tf), lambda i, f: (0, f)),
            pl.BlockSpec((tf, d), lambda i, f: (f, 0)),
        ],
        out_specs=[row(tp), row(ts)],
        out_shape=[jax.ShapeDtypeStruct((tp_all, d), F32), jax.ShapeDtypeStruct((ts_all, d), F32)],
        scratch_shapes=[pltpu.VMEM((tp + ts, d), BF16), pltpu.VMEM((d, tf), BF16),
                        pltpu.VMEM((d, tf), BF16), pltpu.VMEM((tf, d), BF16)],
        compiler_params=pltpu.CompilerParams(
            dimension_semantics=("parallel", "arbitrary"), vmem_limit_bytes=FFN_VMEM_LIMIT),
        name="ffn",
    )(xp, xs, gain, wg, wu, wd)


def _inproj_body(x_ref, g_ref, w_ref, gq_ref, gk_ref, cos_ref, sin_ref, pm_ref,
                 q_ref, k_ref, v_ref, u_ref, *rest, transposed_kv):
    if transposed_kv:
        kt_ref, vt_ref, hn_ref, z_ref = rest
    else:
        kt_ref = vt_ref = None
        hn_ref, z_ref = rest
    j = pl.program_id(1)
    tm, width = q_ref.shape
    lane = lax.broadcasted_iota(jnp.int32, (tm, LANES), 1)
    first_half = (lane % HEAD_DIM) < (HEAD_DIM // 2)

    def matmul(slot, c=None):
        sl = slice(None) if c is None else slice(c * MXU_DIM, (c + 1) * MXU_DIM)
        z_ref[slot, :, sl] = jnp.dot(hn_ref[...], w_ref[:, sl], preferred_element_type=F32)

    def norm_rope(slot, gain_ref, out_ref, t_ref, scale):
        pm = pm_ref[...]
        for c in range(width // MXU_DIM):
            sl = slice(c * MXU_DIM, (c + 1) * MXU_DIM)
            matmul(1 - slot, c)
            z = z_ref[slot, :, sl]
            zz = z * z
            hi = zz.astype(BF16)
            lo = (zz - hi.astype(F32)).astype(BF16)
            ss = (jnp.dot(hi, pm, preferred_element_type=F32)
                  + jnp.dot(lo, pm, preferred_element_type=F32))
            zn = z * lax.rsqrt(ss * (1.0 / HEAD_DIM) + EPS) * gain_ref[:, sl]
            for e in range(MXU_DIM // LANES):
                x = zn[:, e * LANES:(e + 1) * LANES]
                partner = jnp.where(first_half,
                                    pltpu.roll(x, LANES - HEAD_DIM // 2, 1),
                                    pltpu.roll(x, HEAD_DIM // 2, 1))
                y = (x * cos_ref[...] + partner * sin_ref[...]) * scale
                lo_lane = c * MXU_DIM + e * LANES
                out_ref[:, lo_lane:lo_lane + LANES] = y
                if t_ref is not None:
                    t_ref[0, lo_lane:lo_lane + LANES, :] = y.T

    @pl.when(j == 0)
    def _():
        hn_ref[...] = _rms_rows(x_ref[...], g_ref[...]).astype(BF16)
        matmul(0)

    @pl.when(j == 1)
    def _():
        norm_rope(0, gq_ref, q_ref, None, HEAD_DIM ** -0.5)

    @pl.when(j == 2)
    def _():
        norm_rope(1, gk_ref, k_ref, kt_ref, 1.0)

    @pl.when(j == 3)
    def _():
        for c in range(width // MXU_DIM):
            sl = slice(c * MXU_DIM, (c + 1) * MXU_DIM)
            matmul(1, c)
            v = z_ref[0, :, sl]
            v_ref[:, sl] = v
            if vt_ref is not None:
                vt_ref[0, sl, :] = v.T

    @pl.when(j == 4)
    def _():
        matmul(0)

    @pl.when(j == 5)
    def _():
        u_ref[...] = z_ref[1] * jax.nn.sigmoid(z_ref[0])


def _inproj(x, gain, w, gq, gk, cos_t, sin_t, pm, tm, seq_len=None):
    t, d = x.shape
    groups = 5
    width = w.shape[1] // groups
    period_blocks = cos_t.shape[0] // tm
    tab_spec = pl.BlockSpec((tm, LANES), lambda i, j: (i % period_blocks, 0))
    def tile_at(i, j, ready):
        return jnp.where(j >= ready, i, jnp.maximum(i - 1, 0))

    def rows_spec(ready):
        return pl.BlockSpec((tm, width), lambda i, j: (tile_at(i, j, ready), 0))

    ready_q, ready_k, ready_v, ready_u = 1, 2, 3, groups
    out_specs = [rows_spec(ready_q), rows_spec(ready_k), rows_spec(ready_v), rows_spec(ready_u)]
    out_shape = [jax.ShapeDtypeStruct((t, width), F32)] * 4
    if seq_len is not None:
        tiles_per_seq = seq_len // tm

        def cols_spec(ready):
            def index_map(i, j):
                tile = tile_at(i, j, ready)
                return (tile // tiles_per_seq, 0, tile % tiles_per_seq)
            return pl.BlockSpec((1, width, tm), index_map)

        out_specs += [cols_spec(ready_k), cols_spec(ready_v)]
        out_shape += [jax.ShapeDtypeStruct((t // seq_len, width, seq_len), F32)] * 2
    return pl.pallas_call(
        functools.partial(_inproj_body, transposed_kv=seq_len is not None),
        grid=(t // tm, groups + 1),
        in_specs=[
            pl.BlockSpec((tm, d), lambda i, j: (jnp.where(j == 0, i, jnp.minimum(i + 1, t // tm - 1)), 0)),
            pl.BlockSpec((1, d), lambda i, j: (0, 0)),
            pl.BlockSpec((d, width), lambda i, j: (0, jnp.minimum(j, groups - 1))),
            pl.BlockSpec((1, width), lambda i, j: (0, 0)),
            pl.BlockSpec((1, width), lambda i, j: (0, 0)),
            tab_spec, tab_spec,
            pl.BlockSpec((MXU_DIM, MXU_DIM), lambda i, j: (0, 0)),
        ],
        out_specs=out_specs,
        out_shape=out_shape,
        scratch_shapes=[pltpu.VMEM((tm, d), BF16), pltpu.VMEM((2, tm, width), F32)],
        compiler_params=pltpu.CompilerParams(
            dimension_semantics=("arbitrary", "arbitrary"), vmem_limit_bytes=VMEM_LIMIT),
        name="inproj",
    )(x, gain, w, gq, gk, cos_t, sin_t, pm)


def _pattn_body(q_ref, k_ref, v_ref, o_ref, kt, on, ls, dils, span):
    s_len = q_ref.shape[0]
    nblk = s_len // Q_BLOCK

    def block_rows(n, d, bpc):
        if d == 1:
            return pl.ds(pl.multiple_of(n * Q_BLOCK, Q_BLOCK), Q_BLOCK)
        return pl.ds(n // bpc + (n % bpc) * (Q_BLOCK * d), Q_BLOCK, stride=d)

    for p, d in enumerate(dils):
        bpc = (s_len // d) // Q_BLOCK
        for n in range(nblk):
            kt[p, n] = k_ref[block_rows(n, d, bpc), :].T.astype(BF16)

    lane = lax.broadcasted_iota(jnp.int32, (Q_BLOCK, LANES), 1)
    head0 = lane < HEAD_DIM
    qi = lax.broadcasted_iota(jnp.int32, (Q_BLOCK, Q_BLOCK), 0)
    ki = lax.broadcasted_iota(jnp.int32, (Q_BLOCK, Q_BLOCK), 1)
    causal = ki <= qi
    band = (qi + Q_BLOCK - ki) <= span
    one = jnp.ones((), BF16)
    zero = jnp.zeros((), BF16)

    for p, d in enumerate(dils):
        bpc = (s_len // d) // Q_BLOCK
        use_prev = bpc > 1

        def block(n, carry, p=p, d=d, bpc=bpc, use_prev=use_prev):
            rows = block_rows(n, d, bpc)
            q = q_ref[rows, :].astype(BF16)
            vc = v_ref[rows, :].astype(BF16)
            ktc = kt[p, n]
            if use_prev:
                n_prev = jnp.maximum(n - 1, 0)
                ktp = kt[p, n_prev]
                vp = v_ref[block_rows(n_prev, d, bpc), :].astype(BF16)
                prev_ok = band & ((n % bpc) != 0)
            res = []
            for h in range(2):
                own = head0 if h == 0 else ~head0
                qh = jnp.where(own, q, zero)
                sc = jnp.where(causal, jnp.dot(qh, ktc, preferred_element_type=F32), NEG)
                if use_prev:
                    sp = jnp.where(prev_ok, jnp.dot(qh, ktp, preferred_element_type=F32), NEG)
                    m = jnp.max(jnp.maximum(sc, sp), axis=1, keepdims=True)
                else:
                    m = jnp.max(sc, axis=1, keepdims=True)
                o = jnp.dot(jnp.exp(sc - m).astype(BF16), jnp.where(own, vc, one),
                            preferred_element_type=F32)
                if use_prev:
                    o = o + jnp.dot(jnp.exp(sp - m).astype(BF16), jnp.where(own, vp, one),
                                    preferred_element_type=F32)
                res.append((m, o))
            (m0, o0), (m1, o1) = res
            l = pltpu.roll(jnp.where(head0, o1, o0), HEAD_DIM, 1)
            on[p, rows, :] = jnp.where(head0, o0, o1) / l
            ls[p, rows, :] = jnp.where(head0, m0, m1) + jnp.log(l)
            return carry

        lax.fori_loop(0, nblk, block, 0, unroll=8)

    def merge(c, carry):
        rows = pl.ds(pl.multiple_of(c * Q_BLOCK, Q_BLOCK), Q_BLOCK)
        lses = [ls[p, rows, :] for p in range(len(dils))]
        m = lses[0]
        for lp in lses[1:]:
            m = jnp.maximum(m, lp)
        num = jnp.zeros((Q_BLOCK, LANES), F32)
        den = jnp.zeros((Q_BLOCK, LANES), F32)
        for p, lp in enumerate(lses):
            e = jnp.exp(lp - m)
            num = num + on[p, rows, :] * e
            den = den + e
        o_ref[rows, :] = (num / den).astype(o_ref.dtype)
        return carry

    lax.fori_loop(0, nblk, merge, 0, unroll=2)


SAMPLE_KEY_CHUNK = 512


def _sattn_body(q_ref, kn_ref, vn_ref, kt_ref, vt_ref, mc_ref, mn_ref, o_ref, s_ref):
    t_len, width = q_ref.shape
    buf_len = kt_ref.shape[2]
    n_heads = width // HEAD_DIM
    rows = n_heads * t_len
    q = q_ref[...]
    q_rep = jnp.concatenate([q] * n_heads, axis=0)
    row_h = lax.broadcasted_iota(jnp.int32, (rows, width), 0) // t_len
    lane_h = lax.broadcasted_iota(jnp.int32, (rows, width), 1) // HEAD_DIM
    own = row_h == lane_h
    q_exp = jnp.where(own, q_rep, 0.0).astype(BF16)

    pad = jnp.zeros((mn_ref.shape[1] - t_len, width), F32)
    kn = jnp.concatenate([kn_ref[...], pad], axis=0).astype(BF16)
    vn = jnp.concatenate([vn_ref[...], pad], axis=0).astype(BF16)

    chunks = [slice(c, c + SAMPLE_KEY_CHUNK) for c in range(0, buf_len, SAMPLE_KEY_CHUNK)]
    m = None
    for sl in chunks:
        s = jnp.dot(q_exp, kt_ref[0, :, sl].astype(BF16), preferred_element_type=F32)
        s = jnp.where(mc_ref[:, sl] > 0, s, NEG)
        s_ref[:, sl] = s
        mx = jnp.max(s, axis=1, keepdims=True)
        m = mx if m is None else jnp.maximum(m, mx)
    sn = lax.dot_general(q_exp, kn, NT_DIMS, preferred_element_type=F32)
    mn = mn_ref[...]
    sn = jnp.where(mn > 0, sn, NEG)
    m = jnp.maximum(m, jnp.max(sn, axis=1, keepdims=True))
    pn = mn * jnp.exp(sn - m)
    l = jnp.sum(pn, axis=1, keepdims=True)
    o = jnp.dot(pn.astype(BF16), vn, preferred_element_type=F32)
    for sl in chunks:
        p = mc_ref[:, sl] * jnp.exp(s_ref[:, sl] - m)
        l = l + jnp.sum(p, axis=1, keepdims=True)
        o = o + lax.dot_general(p.astype(BF16), vt_ref[0, :, sl].astype(BF16), NT_DIMS,
                                preferred_element_type=F32)
    o = jnp.where(own, o / l, 0.0)
    out = o[0:t_len]
    for h in range(1, n_heads):
        out = out + o[h * t_len:(h + 1) * t_len]
    o_ref[...] = out


def _attn_body(q_ref, k_ref, v_ref, qs_ref, kn_ref, vn_ref, kt_ref, vt_ref, mc_ref, mn_ref,
               o_ref, os_ref, kt, on, ls, s_ref, *, dils, span):
    _pattn_body(q_ref, k_ref, v_ref, o_ref, kt, on, ls, dils, span)
    _sattn_body(qs_ref, kn_ref, vn_ref, kt_ref, vt_ref, mc_ref, mn_ref, os_ref, s_ref)


def _attention(q, k, v, batch, s_len, qs, kn, vn, cache_kt, cache_vt, t_len):
    t, width = q.shape
    dils = tuple(d for _, d in DILATED)
    spans = {w // d for w, d in DILATED}
    assert len(spans) == 1 and dils[0] == 1
    span = spans.pop()
    assert span == Q_BLOCK and all(s_len % (d * Q_BLOCK) == 0 for d in dils)
    npat = len(dils)
    nblk = s_len // Q_BLOCK
    pairs = width // LANES
    dec_batch, _, buf_len = cache_kt.shape
    assert dec_batch == batch * pairs and buf_len % SAMPLE_KEY_CHUNK == 0
    n_heads = width // HEAD_DIM
    mult = _sample_mult(t_len, buf_len)
    mc = np.tile(mult[:, :buf_len], (n_heads, 1))
    mn = np.zeros((n_heads * t_len, LANES), np.float32)
    mn[:, :t_len] = np.tile(mult[:, buf_len:], (n_heads, 1))

    spec = pl.BlockSpec((s_len, LANES), lambda b, hp: (b, hp))
    row_spec = pl.BlockSpec((t_len, width), lambda b, hp: (b * pairs + hp, 0))
    cache_spec = pl.BlockSpec((1, width, buf_len), lambda b, hp: (b * pairs + hp, 0, 0))
    const = lambda shape: pl.BlockSpec(shape, lambda b, hp: (0, 0))
    tok_f = pltpu.VMEM((npat, s_len, LANES), F32)
    return pl.pallas_call(
        functools.partial(_attn_body, dils=dils, span=span),
        grid=(batch, pairs),
        in_specs=[spec, spec, spec, row_spec, row_spec, row_spec, cache_spec, cache_spec,
                  const(mc.shape), const(mn.shape)],
        out_specs=[spec, row_spec],
        out_shape=[jax.ShapeDtypeStruct((t, width), BF16),
                   jax.ShapeDtypeStruct((dec_batch * t_len, width), F32)],
        scratch_shapes=[pltpu.VMEM((npat, nblk, LANES, Q_BLOCK), BF16), tok_f, tok_f,
                        pltpu.VMEM((n_heads * t_len, buf_len), F32)],
        compiler_params=pltpu.CompilerParams(
            dimension_semantics=("parallel", "parallel"), vmem_limit_bytes=VMEM_LIMIT),
        name="attention",
    )(q, k, v, qs, kn, vn, cache_kt, cache_vt, jnp.asarray(mc), jnp.asarray(mn))


def _sample_mult(t_len, buf_len):
    mult = np.zeros((t_len, buf_len + t_len), np.float32)
    for w, d in DILATED:
        for t in range(t_len):
            for j in range(w // d + 1):
                idx = buf_len + t - d * j
                if idx >= 0:
                    mult[t, idx] += 1.0
    return mult


def _ln_swish(y, g, b):
    mu = jnp.mean(y, axis=-1, keepdims=True)
    yc = y - mu
    var = jnp.mean(yc * yc, axis=-1, keepdims=True)
    return _silu(yc * lax.rsqrt(var + EPS) * g + b)


CONV_ROWS = 16
SUBLANES = 8


def _conv_out_prompt_body(ucur_ref, uprev_ref, attn_ref, h_ref, w_ref, b_ref, g_ref, beta_ref,
                          wo_ref, o_ref, ext_ref, sh_ref, wb_ref, conv_ref, *, tiles_per_seq):
    tm, ch = ucur_ref.shape
    halo = uprev_ref.shape[0]
    taps = w_ref.shape[0]
    first = (pl.program_id(0) % tiles_per_seq) == 0
    ext_ref[0:halo, :] = jnp.where(first, 0.0, uprev_ref[...])
    ext_ref[halo:halo + tm, :] = ucur_ref[...]
    sh_rows = sh_ref.shape[1]
    for s in range(1, SUBLANES):
        sh_ref[s - 1] = ext_ref[pl.ds(s, sh_rows), :]
    for w in range(taps):
        wb_ref[w] = jnp.broadcast_to(w_ref[w:w + 1, :], (SUBLANES, ch))
    off = halo - (taps - 1)
    groups = CONV_ROWS // SUBLANES

    for c in range(tm // CONV_ROWS):
        accs = [jnp.zeros((SUBLANES, ch), F32) for _ in range(groups)]
        for w in range(taps):
            s = (off + w) % SUBLANES
            wb = wb_ref[w]
            for g in range(groups):
                base = c * CONV_ROWS + g * SUBLANES + off + w - s
                x = (ext_ref[pl.ds(base, SUBLANES), :] if s == 0
                     else sh_ref[s - 1, pl.ds(base, SUBLANES), :])
                accs[g] = accs[g] + x * wb
        acc = jnp.concatenate(accs, axis=0)
        act = _ln_swish(acc + b_ref[...], g_ref[...], beta_ref[...])
        conv_ref[c * CONV_ROWS:(c + 1) * CONV_ROWS, :] = act.astype(BF16)
    aw = attn_ref.shape[1]
    o_ref[...] = (h_ref[...]
                  + jnp.dot(attn_ref[...], wo_ref[0:aw, :], preferred_element_type=F32)
                  + jnp.dot(conv_ref[...], wo_ref[aw:aw + ch, :], preferred_element_type=F32))


def _conv_out_prompt(u, attn, h, dw_w, dw_b, ln_g, ln_b, wo, s_len, tm=256, halo=32):
    t, ch = u.shape
    d = h.shape[1]
    aw = attn.shape[1]
    taps = dw_w.shape[0]
    assert taps - 1 <= halo and s_len % tm == 0 and tm % halo == 0 and halo % SUBLANES == 0
    hb = tm // halo
    vec = pl.BlockSpec((1, ch), lambda i: (0, 0))
    return pl.pallas_call(
        functools.partial(_conv_out_prompt_body, tiles_per_seq=s_len // tm),
        grid=(t // tm,),
        in_specs=[
            pl.BlockSpec((tm, ch), lambda i: (i, 0)),
            pl.BlockSpec((halo, ch), lambda i: (jnp.maximum(i * hb - 1, 0), 0)),
            pl.BlockSpec((tm, aw), lambda i: (i, 0)),
            pl.BlockSpec((tm, d), lambda i: (i, 0)),
            pl.BlockSpec((taps, ch), lambda i: (0, 0)),
            vec, vec, vec,
            pl.BlockSpec((aw + ch, d), lambda i: (0, 0)),
        ],
        out_specs=pl.BlockSpec((tm, d), lambda i: (i, 0)),
        out_shape=jax.ShapeDtypeStruct((t, d), F32),
        scratch_shapes=[pltpu.VMEM((halo + tm, ch), F32),
                        pltpu.VMEM((SUBLANES - 1, halo + tm - SUBLANES, ch), F32),
                        pltpu.VMEM((taps, SUBLANES, ch), F32),
                        pltpu.VMEM((tm, ch), BF16)],
        compiler_params=pltpu.CompilerParams(
            dimension_semantics=("parallel",), vmem_limit_bytes=VMEM_LIMIT),
        name="conv_out_prompt",
    )(u, u, attn, h, dw_w, dw_b, ln_g, ln_b, wo)


def _conv_out_sample_body(u_ref, st_ref, attn_ref, h_ref, w_ref, b_ref, g_ref, beta_ref, wo_ref,
                          o_ref, ns_ref, ext_ref, conv_ref, at_ref):
    n_state, batch, ch = st_ref.shape
    t_len = u_ref.shape[0]
    taps = w_ref.shape[0]
    ext_ref[0:n_state] = st_ref[...]
    ext_ref[n_state:n_state + t_len] = u_ref[...]
    ns_ref[...] = ext_ref[t_len:t_len + n_state]
    for t in range(t_len):
        acc = jnp.zeros((batch, ch), F32)
        for w in range(taps):
            acc = acc + ext_ref[t + w] * w_ref[w:w + 1, :]
        act = _ln_swish(acc + b_ref[...], g_ref[...], beta_ref[...])
        conv_ref[t * batch:(t + 1) * batch, :] = act.astype(BF16)
        at_ref[t * batch:(t + 1) * batch, :] = attn_ref[t].astype(BF16)
    aw = attn_ref.shape[2]
    proj = (jnp.dot(at_ref[...], wo_ref[0:aw, :], preferred_element_type=F32)
            + jnp.dot(conv_ref[...], wo_ref[aw:aw + ch, :], preferred_element_type=F32))
    for t in range(t_len):
        o_ref[t] = h_ref[t] + proj[t * batch:(t + 1) * batch, :]


def _conv_out_sample(u_t, state_t, attn_t, h_t, dw_w, dw_b, ln_g, ln_b, wo):
    t_len, batch, ch = u_t.shape
    d = h_t.shape[2]
    n_state = state_t.shape[0]
    assert dw_w.shape[0] == n_state + 1
    return pl.pallas_call(
        _conv_out_sample_body,
        out_shape=[jax.ShapeDtypeStruct((t_len, batch, d), F32),
                   jax.ShapeDtypeStruct((n_state, batch, ch), F32)],
        scratch_shapes=[pltpu.VMEM((n_state + t_len, batch, ch), F32),
                        pltpu.VMEM((t_len * batch, ch), BF16),
                        pltpu.VMEM((t_len * batch, attn_t.shape[2]), BF16)],
        compiler_params=pltpu.CompilerParams(vmem_limit_bytes=VMEM_LIMIT),
        name="conv_out_sample",
    )(u_t, state_t, attn_t, h_t, dw_w, dw_b, ln_g, ln_b, wo)


def _rope_tables(positions):
    half = HEAD_DIM // 2
    inv = ROPE_THETA ** (-np.arange(half, dtype=np.float64) / half)
    ang = np.asarray(positions, np.float64)[:, None] * inv[None, :]
    cos, sin = np.cos(ang), np.sin(ang)
    reps = LANES // HEAD_DIM
    cos_t = np.tile(np.concatenate([cos, cos], axis=1), (1, reps))
    sin_t = np.tile(np.concatenate([-sin, sin], axis=1), (1, reps))
    return jnp.asarray(cos_t, F32), jnp.asarray(sin_t, F32)


def kernel(x_prompt, x_sample, cache_k, cache_v, state_conv, ln_ffn1, ffn1_w_gate, ffn1_w_up,
           ffn1_w_down, ln_mix, w_in, q_norm, k_norm, conv_dw_w, conv_dw_b, conv_ln_g, conv_ln_b,
           w_out, ln_ffn2, ffn2_w_gate, ffn2_w_up, ffn2_w_down):
    batch, s_len, d_model = x_prompt.shape
    dec_batch, t_len, _ = x_sample.shape
    depth, _, buf_len, n_heads, head_dim = cache_k.shape
    assert depth == 1 and head_dim == HEAD_DIM
    width = n_heads * head_dim
    ch = conv_dw_w.shape[2]

    xp = x_prompt.reshape(batch * s_len, d_model)
    xs = x_sample.reshape(dec_batch * t_len, d_model)
    ts = dec_batch * t_len

    bf = lambda a: a[0].astype(BF16)
    wi, wo = bf(w_in), bf(w_out)
    reps = width // HEAD_DIM
    gq = jnp.tile(q_norm, (1, reps))
    gk = jnp.tile(k_norm, (1, reps))
    heads_per_tile = MXU_DIM // HEAD_DIM
    pm = jnp.asarray(np.kron(np.eye(heads_per_tile), np.ones((HEAD_DIM, HEAD_DIM))), BF16)

    cos_p, sin_p = _rope_tables(np.arange(s_len))
    cos_s, sin_s = _rope_tables(np.tile(PAST_LEN + np.arange(t_len), dec_batch))

    hp, hs = _ffn(xp, xs, ln_ffn1, ffn1_w_gate[0], ffn1_w_up[0], ffn1_w_down[0])

    qp, kp, vp, up, kt_p, vt_p = _inproj(hp, ln_mix, wi, gq, gk, cos_p, sin_p, pm, tm=512,
                                         seq_len=s_len)
    qs, ks, vs, us = _inproj(hs, ln_mix, wi, gq, gk, cos_s, sin_s, pm, tm=ts)

    kt = jnp.transpose(cache_k[0], (0, 2, 3, 1)).reshape(dec_batch, width, buf_len)
    vt = jnp.transpose(cache_v[0], (0, 2, 3, 1)).reshape(dec_batch, width, buf_len)
    attn_p, attn_s = _attention(qp, kp, vp, batch, s_len, qs, ks, vs, kt, vt, t_len)

    hp = _conv_out_prompt(up, attn_p, hp, conv_dw_w[0], conv_dw_b, conv_ln_g, conv_ln_b, wo, s_len)
    state_t = jnp.transpose(state_conv[0], (1, 0, 2))
    time_major = lambda a: jnp.transpose(a.reshape(dec_batch, t_len, a.shape[-1]), (1, 0, 2))
    hs_t, new_state_t = _conv_out_sample(time_major(us), state_t, time_major(attn_s),
                                         time_major(hs), conv_dw_w[0], conv_dw_b,
                                         conv_ln_g, conv_ln_b, wo)
    hs = jnp.transpose(hs_t, (1, 0, 2)).reshape(ts, d_model)

    yp, ys = _ffn(hp, hs, ln_ffn2, ffn2_w_gate[0], ffn2_w_up[0], ffn2_w_down[0])

    n_state = state_conv.shape[2]
    seq_major = lambda a: jnp.transpose(a.reshape(batch, n_heads, head_dim, s_len), (0, 3, 1, 2))[None]
    kv_s = (1, dec_batch, t_len, n_heads, head_dim)
    return (yp.reshape(batch, s_len, d_model),
            ys.reshape(dec_batch, t_len, d_model),
            seq_major(kt_p), seq_major(vt_p),
            up.reshape(batch, s_len, ch)[:, s_len - n_state:][None],
            ks.reshape(kv_s), vs.reshape(kv_s),
            jnp.transpose(new_state_t, (1, 0, 2))[None])
```

```python
import functools

import numpy as np
import jax
import jax.numpy as jnp
from jax import lax
from jax.experimental import pallas as pl
from jax.experimental.pallas import tpu as pltpu

F32 = jnp.float32
BF16 = jnp.bfloat16

HEAD_DIM = 64
PAST_LEN = 16384
DILATED = ((128, 1), (512, 4), (2048, 16))
Q_BLOCK = 128
ROPE_THETA = 10000.0
EPS = 1e-6
FFN_RES = 0.5
NEG = -1e30

LANES = 128
MXU_DIM = 256
VMEM_LIMIT = 56 * 1024 * 1024
FFN_VMEM_LIMIT = 60 * 1024 * 1024
INPROJ_TILES = 16

NT_DIMS = (((1,), (1,)), ((), ()))


def _silu(x):
    return x * jax.nn.sigmoid(x)


def _rms_rows(x, gain):
    ms = jnp.mean(x * x, axis=-1, keepdims=True)
    return x * lax.rsqrt(ms + EPS) * gain


FFN_ROW_CHUNK = 512


def _ffn_body(xp_ref, xs_ref, g_ref, wg_ref, wu_ref, wd_ref, op_ref, os_ref,
              h_ref, wgb_ref, wub_ref, wdb_ref):
    tp = xp_ref.shape[0]
    ts = xs_ref.shape[0]

    @pl.when(pl.program_id(1) == 0)
    def _():
        for r0 in range(0, tp, FFN_ROW_CHUNK):
            x = xp_ref[r0:r0 + FFN_ROW_CHUNK, :]
            h_ref[r0:r0 + FFN_ROW_CHUNK, :] = _rms_rows(x, g_ref[...]).astype(BF16)
            op_ref[r0:r0 + FFN_ROW_CHUNK, :] = x
        x = xs_ref[...]
        h_ref[tp:tp + ts, :] = _rms_rows(x, g_ref[...]).astype(BF16)
        os_ref[...] = x

    wgb_ref[...] = wg_ref[...].astype(BF16)
    wub_ref[...] = wu_ref[...].astype(BF16)
    wdb_ref[...] = wd_ref[...].astype(BF16)

    def half_step(h):
        g = jnp.dot(h, wgb_ref[...], preferred_element_type=F32)
        u = jnp.dot(h, wub_ref[...], preferred_element_type=F32)
        a = (_silu(g) * u * FFN_RES).astype(BF16)
        return jnp.dot(a, wdb_ref[...], preferred_element_type=F32)

    chunk = (tp + ts) // 2
    op_ref[0:chunk, :] += half_step(h_ref[0:chunk, :])
    res = half_step(h_ref[chunk:tp + ts, :])
    op_ref[chunk:tp, :] += res[0:tp - chunk, :]
    os_ref[...] += res[tp - chunk:chunk, :]


def _ffn(xp, xs, gain, wg, wu, wd, tiles=8, tf=256):
    tp_all, d = xp.shape
    ts_all = xs.shape[0]
    dff = wg.shape[1]
    tp, ts = tp_all // tiles, ts_all // tiles
    assert tp % FFN_ROW_CHUNK == 0 and ts % 16 == 0 and dff % tf == 0 and (tp + ts) % 32 == 0
    row = lambda rows: pl.BlockSpec((rows, d), lambda i, f: (i, 0))
    row_in = lambda rows: pl.BlockSpec(
        (rows, d), lambda i, f: (jnp.where(f == 0, i, jnp.minimum(i + 1, tiles - 1)), 0))
    return pl.pallas_call(
        _ffn_body,
        grid=(tiles, dff // tf),
        in_specs=[
            row_in(tp), row_in(ts),
            pl.BlockSpec((1, d), lambda i, f: (0, 0)),
            pl.BlockSpec((d, tf), lambda i, f: (0, f)),
            pl.BlockSpec((d, tf), lambda i, f: (0, f)),
            pl.BlockSpec((tf, d), lambda i, f: (f, 0)),
        ],
        out_specs=[row(tp), row(ts)],
        out_shape=[jax.ShapeDtypeStruct((tp_all, d), F32), jax.ShapeDtypeStruct((ts_all, d), F32)],
        scratch_shapes=[pltpu.VMEM((tp + ts, d), BF16), pltpu.VMEM((d, tf), BF16),
                        pltpu.VMEM((d, tf), BF16), pltpu.VMEM((tf, d), BF16)],
        compiler_params=pltpu.CompilerParams(
            dimension_semantics=("parallel", "arbitrary"), vmem_limit_bytes=FFN_VMEM_LIMIT),
        name="ffn",
    )(xp, xs, gain, wg, wu, wd)


def _inproj_load(xp_ref, xs_ref, g_ref, hn_ref):
    tp = xp_ref.shape[0]
    hn_ref[0:tp, :] = _rms_rows(xp_ref[...], g_ref[...]).astype(BF16)
    hn_ref[tp:, :] = _rms_rows(xs_ref[...], g_ref[...]).astype(BF16)


def _inproj_matmul(hn_ref, w_ref, z_ref, slot, group, c=None):
    width = z_ref.shape[2]
    if c is None:
        lo, n = group * width, width
        z_ref[slot] = jnp.dot(hn_ref[...], w_ref[:, lo:lo + n], preferred_element_type=F32)
    else:
        lo = group * width + c * MXU_DIM
        z_ref[slot, :, c * MXU_DIM:(c + 1) * MXU_DIM] = jnp.dot(
            hn_ref[...], w_ref[:, lo:lo + MXU_DIM], preferred_element_type=F32)


def _inproj_qk_body(xp_ref, xs_ref, g_ref, w_ref, gq_ref, gk_ref, cos_ref, sin_ref, pm_ref,
                    qp_ref, qs_ref, kp_ref, ks_ref, kt_ref, hn_ref, z_ref):
    j = pl.program_id(1)
    tp = xp_ref.shape[0]
    tm, width = z_ref.shape[1], z_ref.shape[2]
    lane = lax.broadcasted_iota(jnp.int32, (tm, LANES), 1)
    first_half = (lane % HEAD_DIM) < (HEAD_DIM // 2)

    def norm_rope(slot, gain_ref, p_ref, s_ref, t_ref, scale, next_group):
        pm = pm_ref[...]
        for c in range(width // MXU_DIM):
            sl = slice(c * MXU_DIM, (c + 1) * MXU_DIM)
            if next_group is not None:
                _inproj_matmul(hn_ref, w_ref, z_ref, 1 - slot, next_group, c)
            z = z_ref[slot, :, sl]
            zz = z * z
            hi = zz.astype(BF16)
            lo = (zz - hi.astype(F32)).astype(BF16)
            ss = (jnp.dot(hi, pm, preferred_element_type=F32)
                  + jnp.dot(lo, pm, preferred_element_type=F32))
            zn = z * lax.rsqrt(ss * (1.0 / HEAD_DIM) + EPS) * gain_ref[:, sl]
            for e in range(MXU_DIM // LANES):
                x = zn[:, e * LANES:(e + 1) * LANES]
                partner = jnp.where(first_half,
                                    pltpu.roll(x, LANES - HEAD_DIM // 2, 1),
                                    pltpu.roll(x, HEAD_DIM // 2, 1))
                y = (x * cos_ref[...] + partner * sin_ref[...]) * scale
                lanes = slice(c * MXU_DIM + e * LANES, c * MXU_DIM + (e + 1) * LANES)
                p_ref[:, lanes] = y[0:tp]
                s_ref[:, lanes] = y[tp:tm]
                if t_ref is not None:
                    t_ref[0, lanes, :] = y[0:tp].T

    @pl.when(j == 0)
    def _():
        _inproj_load(xp_ref, xs_ref, g_ref, hn_ref)
        _inproj_matmul(hn_ref, w_ref, z_ref, 0, 0)

    @pl.when(j == 1)
    def _():
        norm_rope(0, gq_ref, qp_ref, qs_ref, None, HEAD_DIM ** -0.5, 1)

    @pl.when(j == 2)
    def _():
        norm_rope(1, gk_ref, kp_ref, ks_ref, kt_ref, 1.0, None)


def _inproj_vu_body(xp_ref, xs_ref, g_ref, w_ref, vp_ref, vs_ref, vt_ref, up_ref, us_ref,
                    hn_ref, z_ref):
    j = pl.program_id(1)
    tp = xp_ref.shape[0]
    tm, width = z_ref.shape[1], z_ref.shape[2]

    @pl.when(j == 0)
    def _():
        _inproj_load(xp_ref, xs_ref, g_ref, hn_ref)
        _inproj_matmul(hn_ref, w_ref, z_ref, 0, 0)

    @pl.when(j == 1)
    def _():
        for c in range(width // MXU_DIM):
            sl = slice(c * MXU_DIM, (c + 1) * MXU_DIM)
            _inproj_matmul(hn_ref, w_ref, z_ref, 1, 1, c)
            v = z_ref[0, :, sl]
            vp_ref[:, sl] = v[0:tp]
            vs_ref[:, sl] = v[tp:tm]
            vt_ref[0, sl, :] = v[0:tp].T

    @pl.when(j == 2)
    def _():
        _inproj_matmul(hn_ref, w_ref, z_ref, 0, 2)

    @pl.when(j == 3)
    def _():
        u = z_ref[1] * jax.nn.sigmoid(z_ref[0])
        up_ref[...] = u[0:tp]
        us_ref[...] = u[tp:tm]


def _inproj(xp, xs, gain, w_qk, w_vab, gq, gk, cos_t, sin_t, pm, seq_len, tiles=16):
    tp_all, d = xp.shape
    ts_all = xs.shape[0]
    tp, ts = tp_all // tiles, ts_all // tiles
    tm = tp + ts
    width = w_qk.shape[1] // 2
    tiles_per_seq = seq_len // tp
    assert ts % SUBLANES == 0 and tm % 16 == 0 and seq_len % tp == 0 and cos_t.shape[0] == tiles * tm

    x_in = lambda rows: pl.BlockSpec(
        (rows, d), lambda i, j: (jnp.where(j == 0, i, jnp.minimum(i + 1, tiles - 1)), 0))
    const = lambda shape: pl.BlockSpec(shape, lambda i, j: (0,) * len(shape))

    def tile_at(i, j, ready):
        return jnp.where(j >= ready, i, jnp.maximum(i - 1, 0))

    def rows_out(rows, ready):
        return pl.BlockSpec((rows, width), lambda i, j: (tile_at(i, j, ready), 0))

    def cols_out(ready):
        def index_map(i, j):
            tile = tile_at(i, j, ready)
            return (tile // tiles_per_seq, 0, tile % tiles_per_seq)
        return pl.BlockSpec((1, width, tp), index_map)

    f32 = lambda *shape: jax.ShapeDtypeStruct(shape, F32)
    scratch = [pltpu.VMEM((tm, d), BF16), pltpu.VMEM((2, tm, width), F32)]
    params = pltpu.CompilerParams(
        dimension_semantics=("arbitrary", "arbitrary"), vmem_limit_bytes=VMEM_LIMIT)
    tab_spec = pl.BlockSpec((tm, LANES), lambda i, j: (i, 0))

    qp, qs, kp, ks, kt = pl.pallas_call(
        _inproj_qk_body,
        grid=(tiles, 3),
        in_specs=[x_in(tp), x_in(ts), const((1, d)), const(w_qk.shape),
                  const((1, width)), const((1, width)), tab_spec, tab_spec, const(pm.shape)],
        out_specs=[rows_out(tp, 1), rows_out(ts, 1), rows_out(tp, 2), rows_out(ts, 2), cols_out(2)],
        out_shape=[f32(tp_all, width), f32(ts_all, width), f32(tp_all, width), f32(ts_all, width),
                   f32(tp_all // seq_len, width, seq_len)],
        scratch_shapes=scratch, compiler_params=params, name="inproj_qk",
    )(xp, xs, gain, w_qk, gq, gk, cos_t, sin_t, pm)

    vp, vs, vt, up, us = pl.pallas_call(
        _inproj_vu_body,
        grid=(tiles, 4),
        in_specs=[x_in(tp), x_in(ts), const((1, d)), const(w_vab.shape)],
        out_specs=[rows_out(tp, 1), rows_out(ts, 1), cols_out(1), rows_out(tp, 3), rows_out(ts, 3)],
        out_shape=[f32(tp_all, width), f32(ts_all, width), f32(tp_all // seq_len, width, seq_len),
                   f32(tp_all, width), f32(ts_all, width)],
        scratch_shapes=scratch, compiler_params=params, name="inproj_vu",
    )(xp, xs, gain, w_vab)
    return (qp, kp, vp, up, kt, vt), (qs, ks, vs, us)


def _pattn_body(q_ref, k_ref, v_ref, o_ref, kt, on, ls, dils, span):
    s_len = q_ref.shape[0]
    nblk = s_len // Q_BLOCK

    def block_rows(n, d, bpc):
        if d == 1:
            return pl.ds(pl.multiple_of(n * Q_BLOCK, Q_BLOCK), Q_BLOCK)
        return pl.ds(n // bpc + (n % bpc) * (Q_BLOCK * d), Q_BLOCK, stride=d)

    for p, d in enumerate(dils):
        bpc = (s_len // d) // Q_BLOCK
        for n in range(nblk):
            kt[p, n] = k_ref[block_rows(n, d, bpc), :].T.astype(BF16)

    lane = lax.broadcasted_iota(jnp.int32, (Q_BLOCK, LANES), 1)
    head0 = lane < HEAD_DIM
    qi = lax.broadcasted_iota(jnp.int32, (Q_BLOCK, Q_BLOCK), 0)
    ki = lax.broadcasted_iota(jnp.int32, (Q_BLOCK, Q_BLOCK), 1)
    causal = ki <= qi
    band = (qi + Q_BLOCK - ki) <= span
    one = jnp.ones((), BF16)
    zero = jnp.zeros((), BF16)

    for p, d in enumerate(dils):
        bpc = (s_len // d) // Q_BLOCK
        use_prev = bpc > 1

        def block(n, carry, p=p, d=d, bpc=bpc, use_prev=use_prev):
            rows = block_rows(n, d, bpc)
            q = q_ref[rows, :].astype(BF16)
            vc = v_ref[rows, :].astype(BF16)
            ktc = kt[p, n]
            if use_prev:
                n_prev = jnp.maximum(n - 1, 0)
                ktp = kt[p, n_prev]
                vp = v_ref[block_rows(n_prev, d, bpc), :].astype(BF16)
                prev_ok = band & ((n % bpc) != 0)
            res = []
            for h in range(2):
                own = head0 if h == 0 else ~head0
                qh = jnp.where(own, q, zero)
                sc = jnp.where(causal, jnp.dot(qh, ktc, preferred_element_type=F32), NEG)
                if use_prev:
                    sp = jnp.where(prev_ok, jnp.dot(qh, ktp, preferred_element_type=F32), NEG)
                    m = jnp.max(jnp.maximum(sc, sp), axis=1, keepdims=True)
                else:
                    m = jnp.max(sc, axis=1, keepdims=True)
                o = jnp.dot(jnp.exp(sc - m).astype(BF16), jnp.where(own, vc, one),
                            preferred_element_type=F32)
                if use_prev:
                    o = o + jnp.dot(jnp.exp(sp - m).astype(BF16), jnp.where(own, vp, one),
                                    preferred_element_type=F32)
                res.append((m, o))
            (m0, o0), (m1, o1) = res
            l = pltpu.roll(jnp.where(head0, o1, o0), HEAD_DIM, 1)
            on[p, rows, :] = jnp.where(head0, o0, o1) / l
            ls[p, rows, :] = jnp.where(head0, m0, m1) + jnp.log(l)
            return carry

        lax.fori_loop(0, nblk, block, 0, unroll=8)

    def merge(c, carry):
        rows = pl.ds(pl.multiple_of(c * Q_BLOCK, Q_BLOCK), Q_BLOCK)
        lses = [ls[p, rows, :] for p in range(len(dils))]
        m = lses[0]
        for lp in lses[1:]:
            m = jnp.maximum(m, lp)
        num = jnp.zeros((Q_BLOCK, LANES), F32)
        den = jnp.zeros((Q_BLOCK, LANES), F32)
        for p, lp in enumerate(lses):
            e = jnp.exp(lp - m)
            num = num + on[p, rows, :] * e
            den = den + e
        o_ref[rows, :] = (num / den).astype(o_ref.dtype)
        return carry

    lax.fori_loop(0, nblk, merge, 0, unroll=2)


SAMPLE_KEY_CHUNK = 512


def _sattn_body(q_ref, kn_ref, vn_ref, kt_ref, vt_ref, mc_ref, mn_ref, o_ref, s_ref):
    t_len, width = q_ref.shape
    buf_len = kt_ref.shape[2]
    n_heads = width // HEAD_DIM
    rows = n_heads * t_len
    q = q_ref[...]
    q_rep = jnp.concatenate([q] * n_heads, axis=0)
    row_h = lax.broadcasted_iota(jnp.int32, (rows, width), 0) // t_len
    lane_h = lax.broadcasted_iota(jnp.int32, (rows, width), 1) // HEAD_DIM
    own = row_h == lane_h
    q_exp = jnp.where(own, q_rep, 0.0).astype(BF16)

    pad = jnp.zeros((mn_ref.shape[1] - t_len, width), F32)
    kn = jnp.concatenate([kn_ref[...], pad], axis=0).astype(BF16)
    vn = jnp.concatenate([vn_ref[...], pad], axis=0).astype(BF16)

    chunks = [slice(c, c + SAMPLE_KEY_CHUNK) for c in range(0, buf_len, SAMPLE_KEY_CHUNK)]
    m = None
    for sl in chunks:
        s = jnp.dot(q_exp, kt_ref[0, :, sl].astype(BF16), preferred_element_type=F32)
        s = jnp.where(mc_ref[:, sl] > 0, s, NEG)
        s_ref[:, sl] = s
        mx = jnp.max(s, axis=1, keepdims=True)
        m = mx if m is None else jnp.maximum(m, mx)
    sn = lax.dot_general(q_exp, kn, NT_DIMS, preferred_element_type=F32)
    mn = mn_ref[...]
    sn = jnp.where(mn > 0, sn, NEG)
    m = jnp.maximum(m, jnp.max(sn, axis=1, keepdims=True))
    pn = mn * jnp.exp(sn - m)
    l = jnp.sum(pn, axis=1, keepdims=True)
    o = jnp.dot(pn.astype(BF16), vn, preferred_element_type=F32)
    for sl in chunks:
        p = mc_ref[:, sl] * jnp.exp(s_ref[:, sl] - m)
        l = l + jnp.sum(p, axis=1, keepdims=True)
        o = o + lax.dot_general(p.astype(BF16), vt_ref[0, :, sl].astype(BF16), NT_DIMS,
                                preferred_element_type=F32)
    o = jnp.where(own, o / l, 0.0)
    out = o[0:t_len]
    for h in range(1, n_heads):
        out = out + o[h * t_len:(h + 1) * t_len]
    o_ref[...] = out


def _attn_body(q_ref, k_ref, v_ref, qs_ref, kn_ref, vn_ref, kt_ref, vt_ref, mc_ref, mn_ref,
               o_ref, os_ref, kt, on, ls, s_ref, *, dils, span):
    _pattn_body(q_ref, k_ref, v_ref, o_ref, kt, on, ls, dils, span)
    _sattn_body(qs_ref, kn_ref, vn_ref, kt_ref, vt_ref, mc_ref, mn_ref, os_ref, s_ref)


def _attention(q, k, v, batch, s_len, qs, kn, vn, cache_kt, cache_vt, t_len):
    t, width = q.shape
    dils = tuple(d for _, d in DILATED)
    spans = {w // d for w, d in DILATED}
    assert len(spans) == 1 and dils[0] == 1
    span = spans.pop()
    assert span == Q_BLOCK and all(s_len % (d * Q_BLOCK) == 0 for d in dils)
    npat = len(dils)
    nblk = s_len // Q_BLOCK
    pairs = width // LANES
    dec_batch, _, buf_len = cache_kt.shape
    assert dec_batch == batch * pairs and buf_len % SAMPLE_KEY_CHUNK == 0
    n_heads = width // HEAD_DIM
    mult = _sample_mult(t_len, buf_len)
    mc = np.tile(mult[:, :buf_len], (n_heads, 1))
    mn = np.zeros((n_heads * t_len, LANES), np.float32)
    mn[:, :t_len] = np.tile(mult[:, buf_len:], (n_heads, 1))

    spec = pl.BlockSpec((s_len, LANES), lambda b, hp: (b, hp))
    row_spec = pl.BlockSpec((t_len, width), lambda b, hp: (b * pairs + hp, 0))
    cache_spec = pl.BlockSpec((1, width, buf_len), lambda b, hp: (b * pairs + hp, 0, 0))
    const = lambda shape: pl.BlockSpec(shape, lambda b, hp: (0, 0))
    tok_f = pltpu.VMEM((npat, s_len, LANES), F32)
    return pl.pallas_call(
        functools.partial(_attn_body, dils=dils, span=span),
        grid=(batch, pairs),
        in_specs=[spec, spec, spec, row_spec, row_spec, row_spec, cache_spec, cache_spec,
                  const(mc.shape), const(mn.shape)],
        out_specs=[spec, row_spec],
        out_shape=[jax.ShapeDtypeStruct((t, width), BF16),
                   jax.ShapeDtypeStruct((dec_batch * t_len, width), F32)],
        scratch_shapes=[pltpu.VMEM((npat, nblk, LANES, Q_BLOCK), BF16), tok_f, tok_f,
                        pltpu.VMEM((n_heads * t_len, buf_len), F32)],
        compiler_params=pltpu.CompilerParams(
            dimension_semantics=("parallel", "parallel"), vmem_limit_bytes=VMEM_LIMIT),
        name="attention",
    )(q, k, v, qs, kn, vn, cache_kt, cache_vt, jnp.asarray(mc), jnp.asarray(mn))


def _sample_mult(t_len, buf_len):
    mult = np.zeros((t_len, buf_len + t_len), np.float32)
    for w, d in DILATED:
        for t in range(t_len):
            for j in range(w // d + 1):
                idx = buf_len + t - d * j
                if idx >= 0:
                    mult[t, idx] += 1.0
    return mult


def _ln_swish(y, g, b):
    mu = jnp.mean(y, axis=-1, keepdims=True)
    yc = y - mu
    var = jnp.mean(yc * yc, axis=-1, keepdims=True)
    return _silu(yc * lax.rsqrt(var + EPS) * g + b)


CONV_ROWS = 16
SUBLANES = 8


def _conv_out_prompt_body(ucur_ref, uprev_ref, attn_ref, h_ref, w_ref, b_ref, g_ref, beta_ref,
                          wo_ref, o_ref, ext_ref, sh_ref, wb_ref, conv_ref, *, tiles_per_seq):
    tm, ch = ucur_ref.shape
    halo = uprev_ref.shape[0]
    taps = w_ref.shape[0]
    first = (pl.program_id(0) % tiles_per_seq) == 0
    ext_ref[0:halo, :] = jnp.where(first, 0.0, uprev_ref[...])
    ext_ref[halo:halo + tm, :] = ucur_ref[...]
    sh_rows = sh_ref.shape[1]
    for s in range(1, SUBLANES):
        sh_ref[s - 1] = ext_ref[pl.ds(s, sh_rows), :]
    for w in range(taps):
        wb_ref[w] = jnp.broadcast_to(w_ref[w:w + 1, :], (SUBLANES, ch))
    off = halo - (taps - 1)
    groups = CONV_ROWS // SUBLANES

    for c in range(tm // CONV_ROWS):
        accs = [jnp.zeros((SUBLANES, ch), F32) for _ in range(groups)]
        for w in range(taps):
            s = (off + w) % SUBLANES
            wb = wb_ref[w]
            for g in range(groups):
                base = c * CONV_ROWS + g * SUBLANES + off + w - s
                x = (ext_ref[pl.ds(base, SUBLANES), :] if s == 0
                     else sh_ref[s - 1, pl.ds(base, SUBLANES), :])
                accs[g] = accs[g] + x * wb
        acc = jnp.concatenate(accs, axis=0)
        act = _ln_swish(acc + b_ref[...], g_ref[...], beta_ref[...])
        conv_ref[c * CONV_ROWS:(c + 1) * CONV_ROWS, :] = act.astype(BF16)
    aw = attn_ref.shape[1]
    o_ref[...] = (h_ref[...]
                  + jnp.dot(attn_ref[...], wo_ref[0:aw, :], preferred_element_type=F32)
                  + jnp.dot(conv_ref[...], wo_ref[aw:aw + ch, :], preferred_element_type=F32))


def _conv_out_prompt(u, attn, h, dw_w, dw_b, ln_g, ln_b, wo, s_len, tm=256, halo=32):
    t, ch = u.shape
    d = h.shape[1]
    aw = attn.shape[1]
    taps = dw_w.shape[0]
    assert taps - 1 <= halo and s_len % tm == 0 and tm % halo == 0 and halo % SUBLANES == 0
    hb = tm // halo
    vec = pl.BlockSpec((1, ch), lambda i: (0, 0))
    return pl.pallas_call(
        functools.partial(_conv_out_prompt_body, tiles_per_seq=s_len // tm),
        grid=(t // tm,),
        in_specs=[
            pl.BlockSpec((tm, ch), lambda i: (i, 0)),
            pl.BlockSpec((halo, ch), lambda i: (jnp.maximum(i * hb - 1, 0), 0)),
            pl.BlockSpec((tm, aw), lambda i: (i, 0)),
            pl.BlockSpec((tm, d), lambda i: (i, 0)),
            pl.BlockSpec((taps, ch), lambda i: (0, 0)),
            vec, vec, vec,
            pl.BlockSpec((aw + ch, d), lambda i: (0, 0)),
        ],
        out_specs=pl.BlockSpec((tm, d), lambda i: (i, 0)),
        out_shape=jax.ShapeDtypeStruct((t, d), F32),
        scratch_shapes=[pltpu.VMEM((halo + tm, ch), F32),
                        pltpu.VMEM((SUBLANES - 1, halo + tm - SUBLANES, ch), F32),
                        pltpu.VMEM((taps, SUBLANES, ch), F32),
                        pltpu.VMEM((tm, ch), BF16)],
        compiler_params=pltpu.CompilerParams(
            dimension_semantics=("parallel",), vmem_limit_bytes=VMEM_LIMIT),
        name="conv_out_prompt",
    )(u, u, attn, h, dw_w, dw_b, ln_g, ln_b, wo)


def _conv_out_sample_body(u_ref, st_ref, attn_ref, h_ref, w_ref, b_ref, g_ref, beta_ref, wo_ref,
                          o_ref, ns_ref, ext_ref, conv_ref, at_ref):
    n_state, batch, ch = st_ref.shape
    t_len = u_ref.shape[0]
    taps = w_ref.shape[0]
    ext_ref[0:n_state] = st_ref[...]
    ext_ref[n_state:n_state + t_len] = u_ref[...]
    ns_ref[...] = ext_ref[t_len:t_len + n_state]
    for t in range(t_len):
        acc = jnp.zeros((batch, ch), F32)
        for w in range(taps):
            acc = acc + ext_ref[t + w] * w_ref[w:w + 1, :]
        act = _ln_swish(acc + b_ref[...], g_ref[...], beta_ref[...])
        conv_ref[t * batch:(t + 1) * batch, :] = act.astype(BF16)
        at_ref[t * batch:(t + 1) * batch, :] = attn_ref[t].astype(BF16)
    aw = attn_ref.shape[2]
    proj = (jnp.dot(at_ref[...], wo_ref[0:aw, :], preferred_element_type=F32)
            + jnp.dot(conv_ref[...], wo_ref[aw:aw + ch, :], preferred_element_type=F32))
    for t in range(t_len):
        o_ref[t] = h_ref[t] + proj[t * batch:(t + 1) * batch, :]


def _conv_out_sample(u_t, state_t, attn_t, h_t, dw_w, dw_b, ln_g, ln_b, wo):
    t_len, batch, ch = u_t.shape
    d = h_t.shape[2]
    n_state = state_t.shape[0]
    assert dw_w.shape[0] == n_state + 1
    return pl.pallas_call(
        _conv_out_sample_body,
        out_shape=[jax.ShapeDtypeStruct((t_len, batch, d), F32),
                   jax.ShapeDtypeStruct((n_state, batch, ch), F32)],
        scratch_shapes=[pltpu.VMEM((n_state + t_len, batch, ch), F32),
                        pltpu.VMEM((t_len * batch, ch), BF16),
                        pltpu.VMEM((t_len * batch, attn_t.shape[2]), BF16)],
        compiler_params=pltpu.CompilerParams(vmem_limit_bytes=VMEM_LIMIT),
        name="conv_out_sample",
    )(u_t, state_t, attn_t, h_t, dw_w, dw_b, ln_g, ln_b, wo)


def _rope_tables(positions):
    half = HEAD_DIM // 2
    inv = ROPE_THETA ** (-np.arange(half, dtype=np.float64) / half)
    ang = np.asarray(positions, np.float64)[:, None] * inv[None, :]
    cos, sin = np.cos(ang), np.sin(ang)
    reps = LANES // HEAD_DIM
    cos_t = np.tile(np.concatenate([cos, cos], axis=1), (1, reps))
    sin_t = np.tile(np.concatenate([-sin, sin], axis=1), (1, reps))
    return jnp.asarray(cos_t, F32), jnp.asarray(sin_t, F32)


def kernel(x_prompt, x_sample, cache_k, cache_v, state_conv, ln_ffn1, ffn1_w_gate, ffn1_w_up,
           ffn1_w_down, ln_mix, w_in, q_norm, k_norm, conv_dw_w, conv_dw_b, conv_ln_g, conv_ln_b,
           w_out, ln_ffn2, ffn2_w_gate, ffn2_w_up, ffn2_w_down):
    batch, s_len, d_model = x_prompt.shape
    dec_batch, t_len, _ = x_sample.shape
    depth, _, buf_len, n_heads, head_dim = cache_k.shape
    assert depth == 1 and head_dim == HEAD_DIM
    width = n_heads * head_dim
    ch = conv_dw_w.shape[2]

    xp = x_prompt.reshape(batch * s_len, d_model)
    xs = x_sample.reshape(dec_batch * t_len, d_model)
    ts = dec_batch * t_len

    bf = lambda a: a[0].astype(BF16)
    wi, wo = bf(w_in), bf(w_out)
    reps = width // HEAD_DIM
    gq = jnp.tile(q_norm, (1, reps))
    gk = jnp.tile(k_norm, (1, reps))
    heads_per_tile = MXU_DIM // HEAD_DIM
    pm = jnp.asarray(np.kron(np.eye(heads_per_tile), np.ones((HEAD_DIM, HEAD_DIM))), BF16)

    tiles = INPROJ_TILES
    pos_p = (np.arange(batch * s_len) % s_len).reshape(tiles, -1)
    pos_s = np.tile(PAST_LEN + np.arange(t_len), dec_batch).reshape(tiles, -1)
    cos_t, sin_t = _rope_tables(np.concatenate([pos_p, pos_s], axis=1).reshape(-1))

    hp, hs = _ffn(xp, xs, ln_ffn1, ffn1_w_gate[0], ffn1_w_up[0], ffn1_w_down[0])

    (qp, kp, vp, up, kt_p, vt_p), (qs, ks, vs, us) = _inproj(
        hp, hs, ln_mix, wi[:, :2 * width], wi[:, 2 * width:], gq, gk, cos_t, sin_t, pm,
        seq_len=s_len, tiles=tiles)

    kt = jnp.transpose(cache_k[0], (0, 2, 3, 1)).reshape(dec_batch, width, buf_len)
    vt = jnp.transpose(cache_v[0], (0, 2, 3, 1)).reshape(dec_batch, width, buf_len)
    attn_p, attn_s = _attention(qp, kp, vp, batch, s_len, qs, ks, vs, kt, vt, t_len)

    hp = _conv_out_prompt(up, attn_p, hp, conv_dw_w[0], conv_dw_b, conv_ln_g, conv_ln_b, wo, s_len)
    state_t = jnp.transpose(state_conv[0], (1, 0, 2))
    time_major = lambda a: jnp.transpose(a.reshape(dec_batch, t_len, a.shape[-1]), (1, 0, 2))
    hs_t, new_state_t = _conv_out_sample(time_major(us), state_t, time_major(attn_s),
                                         time_major(hs), conv_dw_w[0], conv_dw_b,
                                         conv_ln_g, conv_ln_b, wo)
    hs = jnp.transpose(hs_t, (1, 0, 2)).reshape(ts, d_model)

    yp, ys = _ffn(hp, hs, ln_ffn2, ffn2_w_gate[0], ffn2_w_up[0], ffn2_w_down[0])

    n_state = state_conv.shape[2]
    seq_major = lambda a: jnp.transpose(a.reshape(batch, n_heads, head_dim, s_len), (0, 3, 1, 2))[None]
    kv_s = (1, dec_batch, t_len, n_heads, head_dim)
    return (yp.reshape(batch, s_len, d_model),
            ys.reshape(dec_batch, t_len, d_model),
            seq_major(kt_p), seq_major(vt_p),
            up.reshape(batch, s_len, ch)[:, s_len - n_state:][None],
            ks.reshape(kv_s), vs.reshape(kv_s),
            jnp.transpose(new_state_t, (1, 0, 2))[None])
```

```python
import functools

import numpy as np
import jax
import jax.numpy as jnp
from jax import lax
from jax.experimental import pallas as pl
from jax.experimental.pallas import tpu as pltpu

F32 = jnp.float32
BF16 = jnp.bfloat16

HEAD_DIM = 64
PAST_LEN = 16384
DILATED = ((128, 1), (512, 4), (2048, 16))
Q_BLOCK = 128
ROPE_THETA = 10000.0
EPS = 1e-6
FFN_RES = 0.5
NEG = -1e30

LANES = 128
MXU_DIM = 256
VMEM_LIMIT = 56 * 1024 * 1024
FFN_VMEM_LIMIT = 60 * 1024 * 1024
INPROJ_TILES = 16

NT_DIMS = (((1,), (1,)), ((), ()))


def _silu(x):
    return x * jax.nn.sigmoid(x)


def _rms_rows(x, gain):
    ms = jnp.mean(x * x, axis=-1, keepdims=True)
    return x * lax.rsqrt(ms + EPS) * gain


FFN_ROW_CHUNK = 512


def _ffn_body(xp_ref, xs_ref, g_ref, wg_ref, wu_ref, wd_ref, op_ref, os_ref,
              h_ref, wgb_ref, wub_ref, wdb_ref):
    tp = xp_ref.shape[0]
    ts = xs_ref.shape[0]

    @pl.when(pl.program_id(1) == 0)
    def _():
        for r0 in range(0, tp, FFN_ROW_CHUNK):
            x = xp_ref[r0:r0 + FFN_ROW_CHUNK, :]
            h_ref[r0:r0 + FFN_ROW_CHUNK, :] = _rms_rows(x, g_ref[...]).astype(BF16)
            op_ref[r0:r0 + FFN_ROW_CHUNK, :] = x
        x = xs_ref[...]
        h_ref[tp:tp + ts, :] = _rms_rows(x, g_ref[...]).astype(BF16)
        os_ref[...] = x

    wgb_ref[...] = wg_ref[...].astype(BF16)
    wub_ref[...] = wu_ref[...].astype(BF16)
    wdb_ref[...] = wd_ref[...].astype(BF16)

    def half_step(h):
        g = jnp.dot(h, wgb_ref[...], preferred_element_type=F32)
        u = jnp.dot(h, wub_ref[...], preferred_element_type=F32)
        a = (_silu(g) * u * FFN_RES).astype(BF16)
        return jnp.dot(a, wdb_ref[...], preferred_element_type=F32)

    chunk = (tp + ts) // 2
    op_ref[0:chunk, :] += half_step(h_ref[0:chunk, :])
    res = half_step(h_ref[chunk:tp + ts, :])
    op_ref[chunk:tp, :] += res[0:tp - chunk, :]
    os_ref[...] += res[tp - chunk:chunk, :]


def _ffn(xp, xs, gain, wg, wu, wd, tiles=8, tf=256):
    tp_all, d = xp.shape
    ts_all = xs.shape[0]
    dff = wg.shape[1]
    tp, ts = tp_all // tiles, ts_all // tiles
    assert tp % FFN_ROW_CHUNK == 0 and ts % 16 == 0 and dff % tf == 0 and (tp + ts) % 32 == 0
    row = lambda rows: pl.BlockSpec((rows, d), lambda i, f: (i, 0))
    row_in = lambda rows: pl.BlockSpec(
        (rows, d), lambda i, f: (jnp.where(f == 0, i, jnp.minimum(i + 1, tiles - 1)), 0))
    return pl.pallas_call(
        _ffn_body,
        grid=(tiles, dff // tf),
        in_specs=[
            row_in(tp), row_in(ts),
            pl.BlockSpec((1, d), lambda i, f: (0, 0)),
            pl.BlockSpec((d, tf), lambda i, f: (0, f)),
            pl.BlockSpec((d, tf), lambda i, f: (0, f)),
            pl.BlockSpec((tf, d), lambda i, f: (f, 0)),
        ],
        out_specs=[row(tp), row(ts)],
        out_shape=[jax.ShapeDtypeStruct((tp_all, d), F32), jax.ShapeDtypeStruct((ts_all, d), F32)],
        scratch_shapes=[pltpu.VMEM((tp + ts, d), BF16), pltpu.VMEM((d, tf), BF16),
                        pltpu.VMEM((d, tf), BF16), pltpu.VMEM((tf, d), BF16)],
        compiler_params=pltpu.CompilerParams(
            dimension_semantics=("parallel", "arbitrary"), vmem_limit_bytes=FFN_VMEM_LIMIT),
        name="ffn",
    )(xp, xs, gain, wg, wu, wd)


def _inproj_load(xp_ref, xs_ref, g_ref, hn_ref):
    tp = xp_ref.shape[0]
    hn_ref[0:tp, :] = _rms_rows(xp_ref[...], g_ref[...]).astype(BF16)
    hn_ref[tp:, :] = _rms_rows(xs_ref[...], g_ref[...]).astype(BF16)


def _inproj_matmul(hn_ref, w_ref, z_ref, slot, group, c=None):
    width = z_ref.shape[2]
    if c is None:
        lo, n = group * width, width
        z_ref[slot] = jnp.dot(hn_ref[...], w_ref[:, lo:lo + n], preferred_element_type=F32)
    else:
        lo = group * width + c * MXU_DIM
        z_ref[slot, :, c * MXU_DIM:(c + 1) * MXU_DIM] = jnp.dot(
            hn_ref[...], w_ref[:, lo:lo + MXU_DIM], preferred_element_type=F32)


def _inproj_qk_body(xp_ref, xs_ref, g_ref, w_ref, gq_ref, gk_ref, cos_ref, sin_ref, pm_ref,
                    qp_ref, qs_ref, kp_ref, ks_ref, kt_ref, hn_ref, z_ref):
    j = pl.program_id(1)
    tp = xp_ref.shape[0]
    tm, width = z_ref.shape[1], z_ref.shape[2]
    lane = lax.broadcasted_iota(jnp.int32, (tm, LANES), 1)
    first_half = (lane % HEAD_DIM) < (HEAD_DIM // 2)

    def norm_rope(slot, gain_ref, p_ref, s_ref, t_ref, scale, next_group):
        pm = pm_ref[...]
        for c in range(width // MXU_DIM):
            sl = slice(c * MXU_DIM, (c + 1) * MXU_DIM)
            if next_group is not None:
                _inproj_matmul(hn_ref, w_ref, z_ref, 1 - slot, next_group, c)
            z = z_ref[slot, :, sl]
            zz = z * z
            hi = zz.astype(BF16)
            lo = (zz - hi.astype(F32)).astype(BF16)
            ss = (jnp.dot(hi, pm, preferred_element_type=F32)
                  + jnp.dot(lo, pm, preferred_element_type=F32))
            zn = z * lax.rsqrt(ss * (1.0 / HEAD_DIM) + EPS) * gain_ref[:, sl]
            for e in range(MXU_DIM // LANES):
                x = zn[:, e * LANES:(e + 1) * LANES]
                partner = jnp.where(first_half,
                                    pltpu.roll(x, LANES - HEAD_DIM // 2, 1),
                                    pltpu.roll(x, HEAD_DIM // 2, 1))
                y = (x * cos_ref[...] + partner * sin_ref[...]) * scale
                lanes = slice(c * MXU_DIM + e * LANES, c * MXU_DIM + (e + 1) * LANES)
                p_ref[:, lanes] = y[0:tp]
                s_ref[:, lanes] = y[tp:tm]
                if t_ref is not None:
                    t_ref[0, lanes, :] = y[0:tp].T

    @pl.when(j == 0)
    def _():
        _inproj_load(xp_ref, xs_ref, g_ref, hn_ref)
        _inproj_matmul(hn_ref, w_ref, z_ref, 0, 0)

    @pl.when(j == 1)
    def _():
        norm_rope(0, gq_ref, qp_ref, qs_ref, None, HEAD_DIM ** -0.5, 1)

    @pl.when(j == 2)
    def _():
        norm_rope(1, gk_ref, kp_ref, ks_ref, kt_ref, 1.0, None)


def _inproj_vu_body(xp_ref, xs_ref, g_ref, wv_ref, wa_ref, wb_ref, vp_ref, vs_ref, vt_ref,
                    up_ref, us_ref, hn_ref, z_ref):
    j = pl.program_id(1)
    tp = xp_ref.shape[0]
    tm, width = z_ref.shape[1], z_ref.shape[2]

    @pl.when(j == 0)
    def _():
        _inproj_load(xp_ref, xs_ref, g_ref, hn_ref)
        _inproj_matmul(hn_ref, wv_ref, z_ref, 0, 0)

    @pl.when(j == 1)
    def _():
        for c in range(width // MXU_DIM):
            sl = slice(c * MXU_DIM, (c + 1) * MXU_DIM)
            _inproj_matmul(hn_ref, wa_ref, z_ref, 1, 0, c)
            v = z_ref[0, :, sl]
            vp_ref[:, sl] = v[0:tp]
            vs_ref[:, sl] = v[tp:tm]
            vt_ref[0, sl, :] = v[0:tp].T

    @pl.when(j == 2)
    def _():
        _inproj_matmul(hn_ref, wb_ref, z_ref, 0, 0)

    @pl.when(j == 3)
    def _():
        u = z_ref[1] * jax.nn.sigmoid(z_ref[0])
        up_ref[...] = u[0:tp]
        us_ref[...] = u[tp:tm]


def _inproj(xp, xs, gain, w, gq, gk, cos_t, sin_t, pm, seq_len, tiles=16):
    tp_all, d = xp.shape
    ts_all = xs.shape[0]
    tp, ts = tp_all // tiles, ts_all // tiles
    tm = tp + ts
    width = w.shape[1] // 5
    tiles_per_seq = seq_len // tp
    assert ts % SUBLANES == 0 and tm % 16 == 0 and seq_len % tp == 0 and cos_t.shape[0] == tiles * tm

    x_in = lambda rows: pl.BlockSpec(
        (rows, d), lambda i, j: (jnp.where(j == 0, i, jnp.minimum(i + 1, tiles - 1)), 0))
    const = lambda shape: pl.BlockSpec(shape, lambda i, j: (0,) * len(shape))
    w_cols = lambda first, groups: pl.BlockSpec(
        (d, groups * width), lambda i, j: (0, first // groups))

    def tile_at(i, j, ready):
        return jnp.where(j >= ready, i, jnp.maximum(i - 1, 0))

    def rows_out(rows, ready):
        return pl.BlockSpec((rows, width), lambda i, j: (tile_at(i, j, ready), 0))

    def cols_out(ready):
        def index_map(i, j):
            tile = tile_at(i, j, ready)
            return (tile // tiles_per_seq, 0, tile % tiles_per_seq)
        return pl.BlockSpec((1, width, tp), index_map)

    f32 = lambda *shape: jax.ShapeDtypeStruct(shape, F32)
    scratch = [pltpu.VMEM((tm, d), BF16), pltpu.VMEM((2, tm, width), F32)]
    params = pltpu.CompilerParams(
        dimension_semantics=("arbitrary", "arbitrary"), vmem_limit_bytes=VMEM_LIMIT)
    tab_spec = pl.BlockSpec((tm, LANES), lambda i, j: (i, 0))

    qp, qs, kp, ks, kt = pl.pallas_call(
        _inproj_qk_body,
        grid=(tiles, 3),
        in_specs=[x_in(tp), x_in(ts), const((1, d)), w_cols(0, 2),
                  const((1, width)), const((1, width)), tab_spec, tab_spec, const(pm.shape)],
        out_specs=[rows_out(tp, 1), rows_out(ts, 1), rows_out(tp, 2), rows_out(ts, 2), cols_out(2)],
        out_shape=[f32(tp_all, width), f32(ts_all, width), f32(tp_all, width), f32(ts_all, width),
                   f32(tp_all // seq_len, width, seq_len)],
        scratch_shapes=scratch, compiler_params=params, name="inproj_qk",
    )(xp, xs, gain, w, gq, gk, cos_t, sin_t, pm)

    vp, vs, vt, up, us = pl.pallas_call(
        _inproj_vu_body,
        grid=(tiles, 4),
        in_specs=[x_in(tp), x_in(ts), const((1, d)), w_cols(2, 1), w_cols(3, 1), w_cols(4, 1)],
        out_specs=[rows_out(tp, 1), rows_out(ts, 1), cols_out(1), rows_out(tp, 3), rows_out(ts, 3)],
        out_shape=[f32(tp_all, width), f32(ts_all, width), f32(tp_all // seq_len, width, seq_len),
                   f32(tp_all, width), f32(ts_all, width)],
        scratch_shapes=scratch, compiler_params=params, name="inproj_vu",
    )(xp, xs, gain, w, w, w)
    return (qp, kp, vp, up, kt, vt), (qs, ks, vs, us)


def _pattn_body(q_ref, k_ref, v_ref, o_ref, kt, on, ls, dils, span):
    s_len = q_ref.shape[0]
    nblk = s_len // Q_BLOCK

    def block_rows(n, d, bpc):
        if d == 1:
            return pl.ds(pl.multiple_of(n * Q_BLOCK, Q_BLOCK), Q_BLOCK)
        return pl.ds(n // bpc + (n % bpc) * (Q_BLOCK * d), Q_BLOCK, stride=d)

    for p, d in enumerate(dils):
        bpc = (s_len // d) // Q_BLOCK
        for n in range(nblk):
            kt[p, n] = k_ref[block_rows(n, d, bpc), :].T.astype(BF16)

    lane = lax.broadcasted_iota(jnp.int32, (Q_BLOCK, LANES), 1)
    head0 = lane < HEAD_DIM
    qi = lax.broadcasted_iota(jnp.int32, (Q_BLOCK, Q_BLOCK), 0)
    ki = lax.broadcasted_iota(jnp.int32, (Q_BLOCK, Q_BLOCK), 1)
    causal = ki <= qi
    band = (qi + Q_BLOCK - ki) <= span
    one = jnp.ones((), BF16)
    zero = jnp.zeros((), BF16)

    for p, d in enumerate(dils):
        bpc = (s_len // d) // Q_BLOCK
        use_prev = bpc > 1

        def block(n, carry, p=p, d=d, bpc=bpc, use_prev=use_prev):
            rows = block_rows(n, d, bpc)
            q = q_ref[rows, :].astype(BF16)
            vc = v_ref[rows, :].astype(BF16)
            ktc = kt[p, n]
            if use_prev:
                n_prev = jnp.maximum(n - 1, 0)
                ktp = kt[p, n_prev]
                vp = v_ref[block_rows(n_prev, d, bpc), :].astype(BF16)
                prev_ok = band & ((n % bpc) != 0)
            res = []
            for h in range(2):
                own = head0 if h == 0 else ~head0
                qh = jnp.where(own, q, zero)
                sc = jnp.where(causal, jnp.dot(qh, ktc, preferred_element_type=F32), NEG)
                if use_prev:
                    sp = jnp.where(prev_ok, jnp.dot(qh, ktp, preferred_element_type=F32), NEG)
                    m = jnp.max(jnp.maximum(sc, sp), axis=1, keepdims=True)
                else:
                    m = jnp.max(sc, axis=1, keepdims=True)
                o = jnp.dot(jnp.exp(sc - m).astype(BF16), jnp.where(own, vc, one),
                            preferred_element_type=F32)
                if use_prev:
                    o = o + jnp.dot(jnp.exp(sp - m).astype(BF16), jnp.where(own, vp, one),
                                    preferred_element_type=F32)
                res.append((m, o))
            (m0, o0), (m1, o1) = res
            l = pltpu.roll(jnp.where(head0, o1, o0), HEAD_DIM, 1)
            on[p, rows, :] = jnp.where(head0, o0, o1) / l
            ls[p, rows, :] = jnp.where(head0, m0, m1) + jnp.log(l)
            return carry

        lax.fori_loop(0, nblk, block, 0, unroll=8)

    def merge(c, carry):
        rows = pl.ds(pl.multiple_of(c * Q_BLOCK, Q_BLOCK), Q_BLOCK)
        lses = [ls[p, rows, :] for p in range(len(dils))]
        m = lses[0]
        for lp in lses[1:]:
            m = jnp.maximum(m, lp)
        num = jnp.zeros((Q_BLOCK, LANES), F32)
        den = jnp.zeros((Q_BLOCK, LANES), F32)
        for p, lp in enumerate(lses):
            e = jnp.exp(lp - m)
            num = num + on[p, rows, :] * e
            den = den + e
        o_ref[rows, :] = (num / den).astype(o_ref.dtype)
        return carry

    lax.fori_loop(0, nblk, merge, 0, unroll=2)


SAMPLE_KEY_CHUNK = 512


def _sattn_body(q_ref, kn_ref, vn_ref, kt_ref, vt_ref, mc_ref, mn_ref, o_ref, s_ref):
    t_len, width = q_ref.shape
    buf_len = kt_ref.shape[2]
    n_heads = width // HEAD_DIM
    rows = n_heads * t_len
    q = q_ref[...]
    q_rep = jnp.concatenate([q] * n_heads, axis=0)
    row_h = lax.broadcasted_iota(jnp.int32, (rows, width), 0) // t_len
    lane_h = lax.broadcasted_iota(jnp.int32, (rows, width), 1) // HEAD_DIM
    own = row_h == lane_h
    q_exp = jnp.where(own, q_rep, 0.0).astype(BF16)

    pad = jnp.zeros((mn_ref.shape[1] - t_len, width), F32)
    kn = jnp.concatenate([kn_ref[...], pad], axis=0).astype(BF16)
    vn = jnp.concatenate([vn_ref[...], pad], axis=0).astype(BF16)

    chunks = [slice(c, c + SAMPLE_KEY_CHUNK) for c in range(0, buf_len, SAMPLE_KEY_CHUNK)]
    m = None
    for sl in chunks:
        s = jnp.dot(q_exp, kt_ref[0, :, sl].astype(BF16), preferred_element_type=F32)
        s = jnp.where(mc_ref[:, sl] > 0, s, NEG)
        s_ref[:, sl] = s
        mx = jnp.max(s, axis=1, keepdims=True)
        m = mx if m is None else jnp.maximum(m, mx)
    sn = lax.dot_general(q_exp, kn, NT_DIMS, preferred_element_type=F32)
    mn = mn_ref[...]
    sn = jnp.where(mn > 0, sn, NEG)
    m = jnp.maximum(m, jnp.max(sn, axis=1, keepdims=True))
    pn = mn * jnp.exp(sn - m)
    l = jnp.sum(pn, axis=1, keepdims=True)
    o = jnp.dot(pn.astype(BF16), vn, preferred_element_type=F32)
    for sl in chunks:
        p = mc_ref[:, sl] * jnp.exp(s_ref[:, sl] - m)
        l = l + jnp.sum(p, axis=1, keepdims=True)
        o = o + lax.dot_general(p.astype(BF16), vt_ref[0, :, sl].astype(BF16), NT_DIMS,
                                preferred_element_type=F32)
    o = jnp.where(own, o / l, 0.0)
    out = o[0:t_len]
    for h in range(1, n_heads):
        out = out + o[h * t_len:(h + 1) * t_len]
    o_ref[...] = out


def _attn_body(q_ref, k_ref, v_ref, qs_ref, kn_ref, vn_ref, kt_ref, vt_ref, mc_ref, mn_ref,
               o_ref, os_ref, kt, on, ls, s_ref, *, dils, span):
    _pattn_body(q_ref, k_ref, v_ref, o_ref, kt, on, ls, dils, span)
    _sattn_body(qs_ref, kn_ref, vn_ref, kt_ref, vt_ref, mc_ref, mn_ref, os_ref, s_ref)


def _attention(q, k, v, batch, s_len, qs, kn, vn, cache_kt, cache_vt, t_len):
    t, width = q.shape
    dils = tuple(d for _, d in DILATED)
    spans = {w // d for w, d in DILATED}
    assert len(spans) == 1 and dils[0] == 1
    span = spans.pop()
    assert span == Q_BLOCK and all(s_len % (d * Q_BLOCK) == 0 for d in dils)
    npat = len(dils)
    nblk = s_len // Q_BLOCK
    pairs = width // LANES
    dec_batch, _, buf_len = cache_kt.shape
    assert dec_batch == batch * pairs and buf_len % SAMPLE_KEY_CHUNK == 0
    n_heads = width // HEAD_DIM
    mult = _sample_mult(t_len, buf_len)
    mc = np.tile(mult[:, :buf_len], (n_heads, 1))
    mn = np.zeros((n_heads * t_len, LANES), np.float32)
    mn[:, :t_len] = np.tile(mult[:, buf_len:], (n_heads, 1))

    spec = pl.BlockSpec((s_len, LANES), lambda b, hp: (b, hp))
    row_spec = pl.BlockSpec((t_len, width), lambda b, hp: (b * pairs + hp, 0))
    cache_spec = pl.BlockSpec((1, width, buf_len), lambda b, hp: (b * pairs + hp, 0, 0))
    const = lambda shape: pl.BlockSpec(shape, lambda b, hp: (0, 0))
    tok_f = pltpu.VMEM((npat, s_len, LANES), F32)
    return pl.pallas_call(
        functools.partial(_attn_body, dils=dils, span=span),
        grid=(batch, pairs),
        in_specs=[spec, spec, spec, row_spec, row_spec, row_spec, cache_spec, cache_spec,
                  const(mc.shape), const(mn.shape)],
        out_specs=[spec, row_spec],
        out_shape=[jax.ShapeDtypeStruct((t, width), BF16),
                   jax.ShapeDtypeStruct((dec_batch * t_len, width), F32)],
        scratch_shapes=[pltpu.VMEM((npat, nblk, LANES, Q_BLOCK), BF16), tok_f, tok_f,
                        pltpu.VMEM((n_heads * t_len, buf_len), F32)],
        compiler_params=pltpu.CompilerParams(
            dimension_semantics=("parallel", "parallel"), vmem_limit_bytes=VMEM_LIMIT),
        name="attention",
    )(q, k, v, qs, kn, vn, cache_kt, cache_vt, jnp.asarray(mc), jnp.asarray(mn))


def _sample_mult(t_len, buf_len):
    mult = np.zeros((t_len, buf_len + t_len), np.float32)
    for w, d in DILATED:
        for t in range(t_len):
            for j in range(w // d + 1):
                idx = buf_len + t - d * j
                if idx >= 0:
                    mult[t, idx] += 1.0
    return mult


def _ln_swish(y, g, b):
    mu = jnp.mean(y, axis=-1, keepdims=True)
    yc = y - mu
    var = jnp.mean(yc * yc, axis=-1, keepdims=True)
    return _silu(yc * lax.rsqrt(var + EPS) * g + b)


CONV_ROWS = 16
SUBLANES = 8


def _conv_out_prompt_body(ucur_ref, uprev_ref, attn_ref, h_ref, w_ref, b_ref, g_ref, beta_ref,
                          wo_ref, o_ref, ext_ref, sh_ref, wb_ref, conv_ref, *, tiles_per_seq):
    tm, ch = ucur_ref.shape
    halo = uprev_ref.shape[0]
    taps = w_ref.shape[0]
    first = (pl.program_id(0) % tiles_per_seq) == 0
    ext_ref[0:halo, :] = jnp.where(first, 0.0, uprev_ref[...])
    ext_ref[halo:halo + tm, :] = ucur_ref[...]
    sh_rows = sh_ref.shape[1]
    for s in range(1, SUBLANES):
        sh_ref[s - 1] = ext_ref[pl.ds(s, sh_rows), :]
    for w in range(taps):
        wb_ref[w] = jnp.broadcast_to(w_ref[w:w + 1, :], (SUBLANES, ch))
    off = halo - (taps - 1)
    groups = CONV_ROWS // SUBLANES

    for c in range(tm // CONV_ROWS):
        accs = [jnp.zeros((SUBLANES, ch), F32) for _ in range(groups)]
        for w in range(taps):
            s = (off + w) % SUBLANES
            wb = wb_ref[w]
            for g in range(groups):
                base = c * CONV_ROWS + g * SUBLANES + off + w - s
                x = (ext_ref[pl.ds(base, SUBLANES), :] if s == 0
                     else sh_ref[s - 1, pl.ds(base, SUBLANES), :])
                accs[g] = accs[g] + x * wb
        acc = jnp.concatenate(accs, axis=0)
        act = _ln_swish(acc + b_ref[...], g_ref[...], beta_ref[...])
        conv_ref[c * CONV_ROWS:(c + 1) * CONV_ROWS, :] = act.astype(BF16)
    aw = attn_ref.shape[1]
    o_ref[...] = (h_ref[...]
                  + jnp.dot(attn_ref[...], wo_ref[0:aw, :], preferred_element_type=F32)
                  + jnp.dot(conv_ref[...], wo_ref[aw:aw + ch, :], preferred_element_type=F32))


def _conv_out_prompt(u, attn, h, dw_w, dw_b, ln_g, ln_b, wo, s_len, tm=256, halo=32):
    t, ch = u.shape
    d = h.shape[1]
    aw = attn.shape[1]
    taps = dw_w.shape[0]
    assert taps - 1 <= halo and s_len % tm == 0 and tm % halo == 0 and halo % SUBLANES == 0
    hb = tm // halo
    vec = pl.BlockSpec((1, ch), lambda i: (0, 0))
    return pl.pallas_call(
        functools.partial(_conv_out_prompt_body, tiles_per_seq=s_len // tm),
        grid=(t // tm,),
        in_specs=[
            pl.BlockSpec((tm, ch), lambda i: (i, 0)),
            pl.BlockSpec((halo, ch), lambda i: (jnp.maximum(i * hb - 1, 0), 0)),
            pl.BlockSpec((tm, aw), lambda i: (i, 0)),
            pl.BlockSpec((tm, d), lambda i: (i, 0)),
            pl.BlockSpec((taps, ch), lambda i: (0, 0)),
            vec, vec, vec,
            pl.BlockSpec((aw + ch, d), lambda i: (0, 0)),
        ],
        out_specs=pl.BlockSpec((tm, d), lambda i: (i, 0)),
        out_shape=jax.ShapeDtypeStruct((t, d), F32),
        scratch_shapes=[pltpu.VMEM((halo + tm, ch), F32),
                        pltpu.VMEM((SUBLANES - 1, halo + tm - SUBLANES, ch), F32),
                        pltpu.VMEM((taps, SUBLANES, ch), F32),
                        pltpu.VMEM((tm, ch), BF16)],
        compiler_params=pltpu.CompilerParams(
            dimension_semantics=("parallel",), vmem_limit_bytes=VMEM_LIMIT),
        name="conv_out_prompt",
    )(u, u, attn, h, dw_w, dw_b, ln_g, ln_b, wo)


def _conv_out_sample_body(u_ref, st_ref, attn_ref, h_ref, w_ref, b_ref, g_ref, beta_ref, wo_ref,
                          o_ref, ns_ref, ext_ref, conv_ref, at_ref):
    n_state, batch, ch = st_ref.shape
    t_len = u_ref.shape[0]
    taps = w_ref.shape[0]
    ext_ref[0:n_state] = st_ref[...]
    ext_ref[n_state:n_state + t_len] = u_ref[...]
    ns_ref[...] = ext_ref[t_len:t_len + n_state]
    for t in range(t_len):
        acc = jnp.zeros((batch, ch), F32)
        for w in range(taps):
            acc = acc + ext_ref[t + w] * w_ref[w:w + 1, :]
        act = _ln_swish(acc + b_ref[...], g_ref[...], beta_ref[...])
        conv_ref[t * batch:(t + 1) * batch, :] = act.astype(BF16)
        at_ref[t * batch:(t + 1) * batch, :] = attn_ref[t].astype(BF16)
    aw = attn_ref.shape[2]
    proj = (jnp.dot(at_ref[...], wo_ref[0:aw, :], preferred_element_type=F32)
            + jnp.dot(conv_ref[...], wo_ref[aw:aw + ch, :], preferred_element_type=F32))
    for t in range(t_len):
        o_ref[t] = h_ref[t] + proj[t * batch:(t + 1) * batch, :]


def _conv_out_sample(u_t, state_t, attn_t, h_t, dw_w, dw_b, ln_g, ln_b, wo):
    t_len, batch, ch = u_t.shape
    d = h_t.shape[2]
    n_state = state_t.shape[0]
    assert dw_w.shape[0] == n_state + 1
    return pl.pallas_call(
        _conv_out_sample_body,
        out_shape=[jax.ShapeDtypeStruct((t_len, batch, d), F32),
                   jax.ShapeDtypeStruct((n_state, batch, ch), F32)],
        scratch_shapes=[pltpu.VMEM((n_state + t_len, batch, ch), F32),
                        pltpu.VMEM((t_len * batch, ch), BF16),
                        pltpu.VMEM((t_len * batch, attn_t.shape[2]), BF16)],
        compiler_params=pltpu.CompilerParams(vmem_limit_bytes=VMEM_LIMIT),
        name="conv_out_sample",
    )(u_t, state_t, attn_t, h_t, dw_w, dw_b, ln_g, ln_b, wo)


def _rope_tables(positions):
    half = HEAD_DIM // 2
    inv = ROPE_THETA ** (-np.arange(half, dtype=np.float64) / half)
    ang = np.asarray(positions, np.float64)[:, None] * inv[None, :]
    cos, sin = np.cos(ang), np.sin(ang)
    reps = LANES // HEAD_DIM
    cos_t = np.tile(np.concatenate([cos, cos], axis=1), (1, reps))
    sin_t = np.tile(np.concatenate([-sin, sin], axis=1), (1, reps))
    return jnp.asarray(cos_t, F32), jnp.asarray(sin_t, F32)


def kernel(x_prompt, x_sample, cache_k, cache_v, state_conv, ln_ffn1, ffn1_w_gate, ffn1_w_up,
           ffn1_w_down, ln_mix, w_in, q_norm, k_norm, conv_dw_w, conv_dw_b, conv_ln_g, conv_ln_b,
           w_out, ln_ffn2, ffn2_w_gate, ffn2_w_up, ffn2_w_down):
    batch, s_len, d_model = x_prompt.shape
    dec_batch, t_len, _ = x_sample.shape
    depth, _, buf_len, n_heads, head_dim = cache_k.shape
    assert depth == 1 and head_dim == HEAD_DIM
    width = n_heads * head_dim
    ch = conv_dw_w.shape[2]

    xp = x_prompt.reshape(batch * s_len, d_model)
    xs = x_sample.reshape(dec_batch * t_len, d_model)
    ts = dec_batch * t_len

    wi = w_in[0].astype(BF16)
    wo = w_out[0].astype(BF16)
    reps = width // HEAD_DIM
    gq = jnp.tile(q_norm, (1, reps))
    gk = jnp.tile(k_norm, (1, reps))
    heads_per_tile = MXU_DIM // HEAD_DIM
    pm = jnp.asarray(np.kron(np.eye(heads_per_tile), np.ones((HEAD_DIM, HEAD_DIM))), BF16)

    tiles = INPROJ_TILES
    pos_p = (np.arange(batch * s_len) % s_len).reshape(tiles, -1)
    pos_s = np.tile(PAST_LEN + np.arange(t_len), dec_batch).reshape(tiles, -1)
    cos_t, sin_t = _rope_tables(np.concatenate([pos_p, pos_s], axis=1).reshape(-1))

    hp, hs = _ffn(xp, xs, ln_ffn1, ffn1_w_gate[0], ffn1_w_up[0], ffn1_w_down[0])

    (qp, kp, vp, up, kt_p, vt_p), (qs, ks, vs, us) = _inproj(
        hp, hs, ln_mix, wi, gq, gk, cos_t, sin_t, pm, seq_len=s_len, tiles=tiles)

    kt = jnp.transpose(cache_k[0], (0, 2, 3, 1)).reshape(dec_batch, width, buf_len)
    vt = jnp.transpose(cache_v[0], (0, 2, 3, 1)).reshape(dec_batch, width, buf_len)
    attn_p, attn_s = _attention(qp, kp, vp, batch, s_len, qs, ks, vs, kt, vt, t_len)

    hp = _conv_out_prompt(up, attn_p, hp, conv_dw_w[0], conv_dw_b, conv_ln_g, conv_ln_b, wo, s_len)
    state_t = jnp.transpose(state_conv[0], (1, 0, 2))
    time_major = lambda a: jnp.transpose(a.reshape(dec_batch, t_len, a.shape[-1]), (1, 0, 2))
    hs_t, new_state_t = _conv_out_sample(time_major(us), state_t, time_major(attn_s),
                                         time_major(hs), conv_dw_w[0], conv_dw_b,
                                         conv_ln_g, conv_ln_b, wo)
    hs = jnp.transpose(hs_t, (1, 0, 2)).reshape(ts, d_model)

    yp, ys = _ffn(hp, hs, ln_ffn2, ffn2_w_gate[0], ffn2_w_up[0], ffn2_w_down[0])

    n_state = state_conv.shape[2]
    seq_major = lambda a: jnp.transpose(a.reshape(batch, n_heads, head_dim, s_len), (0, 3, 1, 2))[None]
    kv_s = (1, dec_batch, t_len, n_heads, head_dim)
    return (yp.reshape(batch, s_len, d_model),
            ys.reshape(dec_batch, t_len, d_model),
            seq_major(kt_p), seq_major(vt_p),
            up.reshape(batch, s_len, ch)[:, s_len - n_state:][None],
            ks.reshape(kv_s), vs.reshape(kv_s),
            jnp.transpose(new_state_t, (1, 0, 2))[None])
```

```python
import functools

import numpy as np
import jax
import jax.numpy as jnp
from jax import lax
from jax.experimental import pallas as pl
from jax.experimental.pallas import tpu as pltpu

F32 = jnp.float32
BF16 = jnp.bfloat16

HEAD_DIM = 64
PAST_LEN = 16384
DILATED = ((128, 1), (512, 4), (2048, 16))
Q_BLOCK = 128
ROPE_THETA = 10000.0
EPS = 1e-6
FFN_RES = 0.5
NEG = -1e30

LANES = 128
MXU_DIM = 256
BF16_ROWS = 16
VMEM_LIMIT = 56 * 1024 * 1024
FFN_VMEM_LIMIT = 60 * 1024 * 1024
INPROJ_TILES = 16

NT_DIMS = (((1,), (1,)), ((), ()))


def _silu(x):
    return x * jax.nn.sigmoid(x)


def _rms_rows(x, gain):
    ms = jnp.mean(x * x, axis=-1, keepdims=True)
    return x * lax.rsqrt(ms + EPS) * gain


FFN_ROW_CHUNK = 512


def _ffn_body(xp_ref, xs_ref, g_ref, wg_ref, wu_ref, wd_ref, *rest, n_casts):
    cast_in, rest = rest[:n_casts], rest[n_casts:]
    op_ref, os_ref = rest[:2]
    cast_out, (h_ref, wgb_ref, wub_ref, wdb_ref) = rest[2:2 + n_casts], rest[2 + n_casts:]
    tp = xp_ref.shape[0]
    ts = xs_ref.shape[0]

    for src, dst in zip(cast_in, cast_out):
        dst[...] = src[...].astype(BF16)

    @pl.when(pl.program_id(1) == 0)
    def _():
        for r0 in range(0, tp, FFN_ROW_CHUNK):
            x = xp_ref[r0:r0 + FFN_ROW_CHUNK, :]
            h_ref[r0:r0 + FFN_ROW_CHUNK, :] = _rms_rows(x, g_ref[...]).astype(BF16)
            op_ref[r0:r0 + FFN_ROW_CHUNK, :] = x
        x = xs_ref[...]
        h_ref[tp:tp + ts, :] = _rms_rows(x, g_ref[...]).astype(BF16)
        os_ref[...] = x

    wgb_ref[...] = wg_ref[...].astype(BF16)
    wub_ref[...] = wu_ref[...].astype(BF16)
    wdb_ref[...] = wd_ref[...].astype(BF16)

    def half_step(h):
        g = jnp.dot(h, wgb_ref[...], preferred_element_type=F32)
        u = jnp.dot(h, wub_ref[...], preferred_element_type=F32)
        a = (_silu(g) * u * FFN_RES).astype(BF16)
        return jnp.dot(a, wdb_ref[...], preferred_element_type=F32)

    chunk = (tp + ts) // 2
    op_ref[0:chunk, :] += half_step(h_ref[0:chunk, :])
    res = half_step(h_ref[chunk:tp + ts, :])
    op_ref[chunk:tp, :] += res[0:tp - chunk, :]
    os_ref[...] += res[tp - chunk:chunk, :]


def _ffn(xp, xs, gain, wg, wu, wd, casts=(), tiles=8, tf=256):
    tp_all, d = xp.shape
    ts_all = xs.shape[0]
    dff = wg.shape[1]
    tp, ts = tp_all // tiles, ts_all // tiles
    assert tp % FFN_ROW_CHUNK == 0 and ts % 16 == 0 and dff % tf == 0 and (tp + ts) % 32 == 0
    row = lambda rows: pl.BlockSpec((rows, d), lambda i, f: (i, 0))
    row_in = lambda rows: pl.BlockSpec(
        (rows, d), lambda i, f: (jnp.where(f == 0, i, jnp.minimum(i + 1, tiles - 1)), 0))
    steps = dff // tf
    cast_specs, cast_shapes = [], []
    for m in casts:
        slabs = m.shape[0] // BF16_ROWS
        assert m.shape[0] % BF16_ROWS == 0 and slabs <= tiles * steps
        cast_specs.append(pl.BlockSpec(
            (BF16_ROWS, m.shape[1]),
            lambda i, f, slabs=slabs: (jnp.minimum(i * steps + f, slabs - 1), 0)))
        cast_shapes.append(jax.ShapeDtypeStruct(m.shape, BF16))
    outs = pl.pallas_call(
        functools.partial(_ffn_body, n_casts=len(casts)),
        grid=(tiles, steps),
        in_specs=[
            row_in(tp), row_in(ts),
            pl.BlockSpec((1, d), lambda i, f: (0, 0)),
            pl.BlockSpec((d, tf), lambda i, f: (0, f)),
            pl.BlockSpec((d, tf), lambda i, f: (0, f)),
            pl.BlockSpec((tf, d), lambda i, f: (f, 0)),
        ] + cast_specs,
        out_specs=[row(tp), row(ts)] + cast_specs,
        out_shape=[jax.ShapeDtypeStruct((tp_all, d), F32), jax.ShapeDtypeStruct((ts_all, d), F32)]
        + cast_shapes,
        scratch_shapes=[pltpu.VMEM((tp + ts, d), BF16), pltpu.VMEM((d, tf), BF16),
                        pltpu.VMEM((d, tf), BF16), pltpu.VMEM((tf, d), BF16)],
        compiler_params=pltpu.CompilerParams(
            dimension_semantics=("arbitrary", "arbitrary"), vmem_limit_bytes=FFN_VMEM_LIMIT),
        name="ffn",
    )(xp, xs, gain, wg, wu, wd, *casts)
    return outs[0], outs[1], tuple(outs[2:])


def _inproj_load(xp_ref, xs_ref, g_ref, hn_ref):
    tp = xp_ref.shape[0]
    hn_ref[0:tp, :] = _rms_rows(xp_ref[...], g_ref[...]).astype(BF16)
    hn_ref[tp:, :] = _rms_rows(xs_ref[...], g_ref[...]).astype(BF16)


def _inproj_matmul(hn_ref, w_ref, z_ref, slot, group, c=None):
    width = z_ref.shape[2]
    if c is None:
        lo, n = group * width, width
        z_ref[slot] = jnp.dot(hn_ref[...], w_ref[:, lo:lo + n], preferred_element_type=F32)
    else:
        lo = group * width + c * MXU_DIM
        z_ref[slot, :, c * MXU_DIM:(c + 1) * MXU_DIM] = jnp.dot(
            hn_ref[...], w_ref[:, lo:lo + MXU_DIM], preferred_element_type=F32)


def _inproj_qk_body(xp_ref, xs_ref, g_ref, w_ref, gq_ref, gk_ref, cos_ref, sin_ref, pm_ref,
                    qp_ref, qs_ref, kp_ref, ks_ref, kt_ref, hn_ref, z_ref):
    j = pl.program_id(1)
    tp = xp_ref.shape[0]
    tm, width = z_ref.shape[1], z_ref.shape[2]
    lane = lax.broadcasted_iota(jnp.int32, (tm, LANES), 1)
    first_half = (lane % HEAD_DIM) < (HEAD_DIM // 2)

    def norm_rope(slot, gain_ref, p_ref, s_ref, t_ref, scale, next_group):
        pm = pm_ref[...]
        for c in range(width // MXU_DIM):
            sl = slice(c * MXU_DIM, (c + 1) * MXU_DIM)
            if next_group is not None:
                _inproj_matmul(hn_ref, w_ref, z_ref, 1 - slot, next_group, c)
            z = z_ref[slot, :, sl]
            zz = z * z
            hi = zz.astype(BF16)
            lo = (zz - hi.astype(F32)).astype(BF16)
            ss = (jnp.dot(hi, pm, preferred_element_type=F32)
                  + jnp.dot(lo, pm, preferred_element_type=F32))
            zn = z * lax.rsqrt(ss * (1.0 / HEAD_DIM) + EPS) * gain_ref[:, sl]
            for e in range(MXU_DIM // LANES):
                x = zn[:, e * LANES:(e + 1) * LANES]
                partner = jnp.where(first_half,
                                    pltpu.roll(x, LANES - HEAD_DIM // 2, 1),
                                    pltpu.roll(x, HEAD_DIM // 2, 1))
                y = (x * cos_ref[...] + partner * sin_ref[...]) * scale
                lanes = slice(c * MXU_DIM + e * LANES, c * MXU_DIM + (e + 1) * LANES)
                p_ref[:, lanes] = y[0:tp]
                s_ref[:, lanes] = y[tp:tm]
                if t_ref is not None:
                    t_ref[0, lanes, :] = y[0:tp].T

    @pl.when(j == 0)
    def _():
        _inproj_load(xp_ref, xs_ref, g_ref, hn_ref)
        _inproj_matmul(hn_ref, w_ref, z_ref, 0, 0)

    @pl.when(j == 1)
    def _():
        norm_rope(0, gq_ref, qp_ref, qs_ref, None, HEAD_DIM ** -0.5, 1)

    @pl.when(j == 2)
    def _():
        norm_rope(1, gk_ref, kp_ref, ks_ref, kt_ref, 1.0, None)


def _inproj_vu_body(xp_ref, xs_ref, g_ref, wv_ref, wa_ref, wb_ref, vp_ref, vs_ref, vt_ref,
                    up_ref, us_ref, hn_ref, z_ref):
    j = pl.program_id(1)
    tp = xp_ref.shape[0]
    tm, width = z_ref.shape[1], z_ref.shape[2]

    @pl.when(j == 0)
    def _():
        _inproj_load(xp_ref, xs_ref, g_ref, hn_ref)
        _inproj_matmul(hn_ref, wv_ref, z_ref, 0, 0)

    @pl.when(j == 1)
    def _():
        for c in range(width // MXU_DIM):
            sl = slice(c * MXU_DIM, (c + 1) * MXU_DIM)
            _inproj_matmul(hn_ref, wa_ref, z_ref, 1, 0, c)
            v = z_ref[0, :, sl]
            vp_ref[:, sl] = v[0:tp]
            vs_ref[:, sl] = v[tp:tm]
            vt_ref[0, sl, :] = v[0:tp].T

    @pl.when(j == 2)
    def _():
        _inproj_matmul(hn_ref, wb_ref, z_ref, 0, 0)

    @pl.when(j == 3)
    def _():
        u = z_ref[1] * jax.nn.sigmoid(z_ref[0])
        up_ref[...] = u[0:tp]
        us_ref[...] = u[tp:tm]


def _inproj(xp, xs, gain, w, gq, gk, cos_t, sin_t, pm, seq_len, tiles=16):
    tp_all, d = xp.shape
    ts_all = xs.shape[0]
    tp, ts = tp_all // tiles, ts_all // tiles
    tm = tp + ts
    width = w.shape[1] // 5
    tiles_per_seq = seq_len // tp
    assert ts % SUBLANES == 0 and tm % 16 == 0 and seq_len % tp == 0 and cos_t.shape[0] == tiles * tm

    x_in = lambda rows: pl.BlockSpec(
        (rows, d), lambda i, j: (jnp.where(j == 0, i, jnp.minimum(i + 1, tiles - 1)), 0))
    const = lambda shape: pl.BlockSpec(shape, lambda i, j: (0,) * len(shape))
    w_cols = lambda first, groups: pl.BlockSpec(
        (d, groups * width), lambda i, j: (0, first // groups))

    def tile_at(i, j, ready):
        return jnp.where(j >= ready, i, jnp.maximum(i - 1, 0))

    def rows_out(rows, ready):
        return pl.BlockSpec((rows, width), lambda i, j: (tile_at(i, j, ready), 0))

    def cols_out(ready):
        def index_map(i, j):
            tile = tile_at(i, j, ready)
            return (tile // tiles_per_seq, 0, tile % tiles_per_seq)
        return pl.BlockSpec((1, width, tp), index_map)

    f32 = lambda *shape: jax.ShapeDtypeStruct(shape, F32)
    scratch = [pltpu.VMEM((tm, d), BF16), pltpu.VMEM((2, tm, width), F32)]
    params = pltpu.CompilerParams(
        dimension_semantics=("arbitrary", "arbitrary"), vmem_limit_bytes=VMEM_LIMIT)
    tab_spec = pl.BlockSpec((tm, LANES), lambda i, j: (i, 0))

    qp, qs, kp, ks, kt = pl.pallas_call(
        _inproj_qk_body,
        grid=(tiles, 3),
        in_specs=[x_in(tp), x_in(ts), const((1, d)), w_cols(0, 2),
                  const((1, width)), const((1, width)), tab_spec, tab_spec, const(pm.shape)],
        out_specs=[rows_out(tp, 1), rows_out(ts, 1), rows_out(tp, 2), rows_out(ts, 2), cols_out(2)],
        out_shape=[f32(tp_all, width), f32(ts_all, width), f32(tp_all, width), f32(ts_all, width),
                   f32(tp_all // seq_len, width, seq_len)],
        scratch_shapes=scratch, compiler_params=params, name="inproj_qk",
    )(xp, xs, gain, w, gq, gk, cos_t, sin_t, pm)

    vp, vs, vt, up, us = pl.pallas_call(
        _inproj_vu_body,
        grid=(tiles, 4),
        in_specs=[x_in(tp), x_in(ts), const((1, d)), w_cols(2, 1), w_cols(3, 1), w_cols(4, 1)],
        out_specs=[rows_out(tp, 1), rows_out(ts, 1), cols_out(1), rows_out(tp, 3), rows_out(ts, 3)],
        out_shape=[f32(tp_all, width), f32(ts_all, width), f32(tp_all // seq_len, width, seq_len),
                   f32(tp_all, width), f32(ts_all, width)],
        scratch_shapes=scratch, compiler_params=params, name="inproj_vu",
    )(xp, xs, gain, w, w, w)
    return (qp, kp, vp, up, kt, vt), (qs, ks, vs, us)


def _pattn_body(q_ref, k_ref, v_ref, o_ref, kt, on, ls, dils, span):
    s_len = q_ref.shape[0]
    nblk = s_len // Q_BLOCK

    def block_rows(n, d, bpc):
        if d == 1:
            return pl.ds(pl.multiple_of(n * Q_BLOCK, Q_BLOCK), Q_BLOCK)
        return pl.ds(n // bpc + (n % bpc) * (Q_BLOCK * d), Q_BLOCK, stride=d)

    for p, d in enumerate(dils):
        bpc = (s_len // d) // Q_BLOCK
        for n in range(nblk):
            kt[p, n] = k_ref[block_rows(n, d, bpc), :].T.astype(BF16)

    lane = lax.broadcasted_iota(jnp.int32, (Q_BLOCK, LANES), 1)
    head0 = lane < HEAD_DIM
    qi = lax.broadcasted_iota(jnp.int32, (Q_BLOCK, Q_BLOCK), 0)
    ki = lax.broadcasted_iota(jnp.int32, (Q_BLOCK, Q_BLOCK), 1)
    causal = ki <= qi
    band = (qi + Q_BLOCK - ki) <= span
    one = jnp.ones((), BF16)
    zero = jnp.zeros((), BF16)

    for p, d in enumerate(dils):
        bpc = (s_len // d) // Q_BLOCK
        use_prev = bpc > 1

        def block(n, carry, p=p, d=d, bpc=bpc, use_prev=use_prev):
            rows = block_rows(n, d, bpc)
            q = q_ref[rows, :].astype(BF16)
            vc = v_ref[rows, :].astype(BF16)
            ktc = kt[p, n]
            if use_prev:
                n_prev = jnp.maximum(n - 1, 0)
                ktp = kt[p, n_prev]
                vp = v_ref[block_rows(n_prev, d, bpc), :].astype(BF16)
                prev_ok = band & ((n % bpc) != 0)
            res = []
            for h in range(2):
                own = head0 if h == 0 else ~head0
                qh = jnp.where(own, q, zero)
                sc = jnp.where(causal, jnp.dot(qh, ktc, preferred_element_type=F32), NEG)
                if use_prev:
                    sp = jnp.where(prev_ok, jnp.dot(qh, ktp, preferred_element_type=F32), NEG)
                    m = jnp.max(jnp.maximum(sc, sp), axis=1, keepdims=True)
                else:
                    m = jnp.max(sc, axis=1, keepdims=True)
                o = jnp.dot(jnp.exp(sc - m).astype(BF16), jnp.where(own, vc, one),
                            preferred_element_type=F32)
                if use_prev:
                    o = o + jnp.dot(jnp.exp(sp - m).astype(BF16), jnp.where(own, vp, one),
                                    preferred_element_type=F32)
                res.append((m, o))
            (m0, o0), (m1, o1) = res
            l = pltpu.roll(jnp.where(head0, o1, o0), HEAD_DIM, 1)
            on[p, rows, :] = jnp.where(head0, o0, o1) / l
            ls[p, rows, :] = jnp.where(head0, m0, m1) + jnp.log(l)
            return carry

        lax.fori_loop(0, nblk, block, 0, unroll=8)

    def merge(c, carry):
        rows = pl.ds(pl.multiple_of(c * Q_BLOCK, Q_BLOCK), Q_BLOCK)
        lses = [ls[p, rows, :] for p in range(len(dils))]
        m = lses[0]
        for lp in lses[1:]:
            m = jnp.maximum(m, lp)
        num = jnp.zeros((Q_BLOCK, LANES), F32)
        den = jnp.zeros((Q_BLOCK, LANES), F32)
        for p, lp in enumerate(lses):
            e = jnp.exp(lp - m)
            num = num + on[p, rows, :] * e
            den = den + e
        o_ref[rows, :] = (num / den).astype(o_ref.dtype)
        return carry

    lax.fori_loop(0, nblk, merge, 0, unroll=2)


SAMPLE_KEY_CHUNK = 512


def _sattn_body(q_ref, kn_ref, vn_ref, kt_ref, vt_ref, mc_ref, mn_ref, o_ref, s_ref):
    t_len, width = q_ref.shape
    buf_len = kt_ref.shape[2]
    n_heads = width // HEAD_DIM
    rows = n_heads * t_len
    q = q_ref[...]
    q_rep = jnp.concatenate([q] * n_heads, axis=0)
    row_h = lax.broadcasted_iota(jnp.int32, (rows, width), 0) // t_len
    lane_h = lax.broadcasted_iota(jnp.int32, (rows, width), 1) // HEAD_DIM
    own = row_h == lane_h
    q_exp = jnp.where(own, q_rep, 0.0).astype(BF16)

    pad = jnp.zeros((mn_ref.shape[1] - t_len, width), F32)
    kn = jnp.concatenate([kn_ref[...], pad], axis=0).astype(BF16)
    vn = jnp.concatenate([vn_ref[...], pad], axis=0).astype(BF16)

    chunks = [slice(c, c + SAMPLE_KEY_CHUNK) for c in range(0, buf_len, SAMPLE_KEY_CHUNK)]
    m = None
    for sl in chunks:
        s = jnp.dot(q_exp, kt_ref[0, :, sl].astype(BF16), preferred_element_type=F32)
        s = jnp.where(mc_ref[:, sl] > 0, s, NEG)
        s_ref[:, sl] = s
        mx = jnp.max(s, axis=1, keepdims=True)
        m = mx if m is None else jnp.maximum(m, mx)
    sn = lax.dot_general(q_exp, kn, NT_DIMS, preferred_element_type=F32)
    mn = mn_ref[...]
    sn = jnp.where(mn > 0, sn, NEG)
    m = jnp.maximum(m, jnp.max(sn, axis=1, keepdims=True))
    pn = mn * jnp.exp(sn - m)
    l = jnp.sum(pn, axis=1, keepdims=True)
    o = jnp.dot(pn.astype(BF16), vn, preferred_element_type=F32)
    for sl in chunks:
        p = mc_ref[:, sl] * jnp.exp(s_ref[:, sl] - m)
        l = l + jnp.sum(p, axis=1, keepdims=True)
        o = o + lax.dot_general(p.astype(BF16), vt_ref[0, :, sl].astype(BF16), NT_DIMS,
                                preferred_element_type=F32)
    o = jnp.where(own, o / l, 0.0)
    out = o[0:t_len]
    for h in range(1, n_heads):
        out = out + o[h * t_len:(h + 1) * t_len]
    o_ref[...] = out


def _attn_body(q_ref, k_ref, v_ref, qs_ref, kn_ref, vn_ref, kt_ref, vt_ref, mc_ref, mn_ref,
               o_ref, os_ref, kt, on, ls, s_ref, *, dils, span):
    _pattn_body(q_ref, k_ref, v_ref, o_ref, kt, on, ls, dils, span)
    _sattn_body(qs_ref, kn_ref, vn_ref, kt_ref, vt_ref, mc_ref, mn_ref, os_ref, s_ref)


def _attention(q, k, v, batch, s_len, qs, kn, vn, cache_kt, cache_vt, t_len):
    t, width = q.shape
    dils = tuple(d for _, d in DILATED)
    spans = {w // d for w, d in DILATED}
    assert len(spans) == 1 and dils[0] == 1
    span = spans.pop()
    assert span == Q_BLOCK and all(s_len % (d * Q_BLOCK) == 0 for d in dils)
    npat = len(dils)
    nblk = s_len // Q_BLOCK
    pairs = width // LANES
    dec_batch, _, buf_len = cache_kt.shape
    assert dec_batch == batch * pairs and buf_len % SAMPLE_KEY_CHUNK == 0
    n_heads = width // HEAD_DIM
    mult = _sample_mult(t_len, buf_len)
    mc = np.tile(mult[:, :buf_len], (n_heads, 1))
    mn = np.zeros((n_heads * t_len, LANES), np.float32)
    mn[:, :t_len] = np.tile(mult[:, buf_len:], (n_heads, 1))

    spec = pl.BlockSpec((s_len, LANES), lambda b, hp: (b, hp))
    row_spec = pl.BlockSpec((t_len, width), lambda b, hp: (b * pairs + hp, 0))
    cache_spec = pl.BlockSpec((1, width, buf_len), lambda b, hp: (b * pairs + hp, 0, 0))
    const = lambda shape: pl.BlockSpec(shape, lambda b, hp: (0, 0))
    tok_f = pltpu.VMEM((npat, s_len, LANES), F32)
    return pl.pallas_call(
        functools.partial(_attn_body, dils=dils, span=span),
        grid=(batch, pairs),
        in_specs=[spec, spec, spec, row_spec, row_spec, row_spec, cache_spec, cache_spec,
                  const(mc.shape), const(mn.shape)],
        out_specs=[spec, row_spec],
        out_shape=[jax.ShapeDtypeStruct((t, width), BF16),
                   jax.ShapeDtypeStruct((dec_batch * t_len, width), F32)],
        scratch_shapes=[pltpu.VMEM((npat, nblk, LANES, Q_BLOCK), BF16), tok_f, tok_f,
                        pltpu.VMEM((n_heads * t_len, buf_len), F32)],
        compiler_params=pltpu.CompilerParams(
            dimension_semantics=("parallel", "parallel"), vmem_limit_bytes=VMEM_LIMIT),
        name="attention",
    )(q, k, v, qs, kn, vn, cache_kt, cache_vt, jnp.asarray(mc), jnp.asarray(mn))


def _sample_mult(t_len, buf_len):
    mult = np.zeros((t_len, buf_len + t_len), np.float32)
    for w, d in DILATED:
        for t in range(t_len):
            for j in range(w // d + 1):
                idx = buf_len + t - d * j
                if idx >= 0:
                    mult[t, idx] += 1.0
    return mult


def _ln_swish(y, g, b):
    mu = jnp.mean(y, axis=-1, keepdims=True)
    yc = y - mu
    var = jnp.mean(yc * yc, axis=-1, keepdims=True)
    return _silu(yc * lax.rsqrt(var + EPS) * g + b)


CONV_ROWS = 16
SUBLANES = 8


def _conv_out_prompt_body(ucur_ref, uprev_ref, attn_ref, h_ref, w_ref, b_ref, g_ref, beta_ref,
                          wo_ref, o_ref, ext_ref, sh_ref, wb_ref, conv_ref, *, tiles_per_seq):
    tm, ch = ucur_ref.shape
    halo = uprev_ref.shape[0]
    taps = w_ref.shape[0]
    first = (pl.program_id(0) % tiles_per_seq) == 0
    ext_ref[0:halo, :] = jnp.where(first, 0.0, uprev_ref[...])
    ext_ref[halo:halo + tm, :] = ucur_ref[...]
    sh_rows = sh_ref.shape[1]
    for s in range(1, SUBLANES):
        sh_ref[s - 1] = ext_ref[pl.ds(s, sh_rows), :]
    for w in range(taps):
        wb_ref[w] = jnp.broadcast_to(w_ref[w:w + 1, :], (SUBLANES, ch))
    off = halo - (taps - 1)
    groups = CONV_ROWS // SUBLANES

    for c in range(tm // CONV_ROWS):
        accs = [jnp.zeros((SUBLANES, ch), F32) for _ in range(groups)]
        for w in range(taps):
            s = (off + w) % SUBLANES
            wb = wb_ref[w]
            for g in range(groups):
                base = c * CONV_ROWS + g * SUBLANES + off + w - s
                x = (ext_ref[pl.ds(base, SUBLANES), :] if s == 0
                     else sh_ref[s - 1, pl.ds(base, SUBLANES), :])
                accs[g] = accs[g] + x * wb
        acc = jnp.concatenate(accs, axis=0)
        act = _ln_swish(acc + b_ref[...], g_ref[...], beta_ref[...])
        conv_ref[c * CONV_ROWS:(c + 1) * CONV_ROWS, :] = act.astype(BF16)
    aw = attn_ref.shape[1]
    o_ref[...] = (h_ref[...]
                  + jnp.dot(attn_ref[...], wo_ref[0:aw, :], preferred_element_type=F32)
                  + jnp.dot(conv_ref[...], wo_ref[aw:aw + ch, :], preferred_element_type=F32))


def _conv_out_prompt(u, attn, h, dw_w, dw_b, ln_g, ln_b, wo, s_len, tm=256, halo=32):
    t, ch = u.shape
    d = h.shape[1]
    aw = attn.shape[1]
    taps = dw_w.shape[0]
    assert taps - 1 <= halo and s_len % tm == 0 and tm % halo == 0 and halo % SUBLANES == 0
    hb = tm // halo
    vec = pl.BlockSpec((1, ch), lambda i: (0, 0))
    return pl.pallas_call(
        functools.partial(_conv_out_prompt_body, tiles_per_seq=s_len // tm),
        grid=(t // tm,),
        in_specs=[
            pl.BlockSpec((tm, ch), lambda i: (i, 0)),
            pl.BlockSpec((halo, ch), lambda i: (jnp.maximum(i * hb - 1, 0), 0)),
            pl.BlockSpec((tm, aw), lambda i: (i, 0)),
            pl.BlockSpec((tm, d), lambda i: (i, 0)),
            pl.BlockSpec((taps, ch), lambda i: (0, 0)),
            vec, vec, vec,
            pl.BlockSpec((aw + ch, d), lambda i: (0, 0)),
        ],
        out_specs=pl.BlockSpec((tm, d), lambda i: (i, 0)),
        out_shape=jax.ShapeDtypeStruct((t, d), F32),
        scratch_shapes=[pltpu.VMEM((halo + tm, ch), F32),
                        pltpu.VMEM((SUBLANES - 1, halo + tm - SUBLANES, ch), F32),
                        pltpu.VMEM((taps, SUBLANES, ch), F32),
                        pltpu.VMEM((tm, ch), BF16)],
        compiler_params=pltpu.CompilerParams(
            dimension_semantics=("parallel",), vmem_limit_bytes=VMEM_LIMIT),
        name="conv_out_prompt",
    )(u, u, attn, h, dw_w, dw_b, ln_g, ln_b, wo)


def _conv_out_sample_body(u_ref, st_ref, attn_ref, h_ref, w_ref, b_ref, g_ref, beta_ref, wo_ref,
                          o_ref, ns_ref, ext_ref, conv_ref, at_ref):
    n_state, batch, ch = st_ref.shape
    t_len = u_ref.shape[0]
    taps = w_ref.shape[0]
    ext_ref[0:n_state] = st_ref[...]
    ext_ref[n_state:n_state + t_len] = u_ref[...]
    ns_ref[...] = ext_ref[t_len:t_len + n_state]
    for t in range(t_len):
        acc = jnp.zeros((batch, ch), F32)
        for w in range(taps):
            acc = acc + ext_ref[t + w] * w_ref[w:w + 1, :]
        act = _ln_swish(acc + b_ref[...], g_ref[...], beta_ref[...])
        conv_ref[t * batch:(t + 1) * batch, :] = act.astype(BF16)
        at_ref[t * batch:(t + 1) * batch, :] = attn_ref[t].astype(BF16)
    aw = attn_ref.shape[2]
    proj = (jnp.dot(at_ref[...], wo_ref[0:aw, :], preferred_element_type=F32)
            + jnp.dot(conv_ref[...], wo_ref[aw:aw + ch, :], preferred_element_type=F32))
    for t in range(t_len):
        o_ref[t] = h_ref[t] + proj[t * batch:(t + 1) * batch, :]


def _conv_out_sample(u_t, state_t, attn_t, h_t, dw_w, dw_b, ln_g, ln_b, wo):
    t_len, batch, ch = u_t.shape
    d = h_t.shape[2]
    n_state = state_t.shape[0]
    assert dw_w.shape[0] == n_state + 1
    return pl.pallas_call(
        _conv_out_sample_body,
        out_shape=[jax.ShapeDtypeStruct((t_len, batch, d), F32),
                   jax.ShapeDtypeStruct((n_state, batch, ch), F32)],
        scratch_shapes=[pltpu.VMEM((n_state + t_len, batch, ch), F32),
                        pltpu.VMEM((t_len * batch, ch), BF16),
                        pltpu.VMEM((t_len * batch, attn_t.shape[2]), BF16)],
        compiler_params=pltpu.CompilerParams(vmem_limit_bytes=VMEM_LIMIT),
        name="conv_out_sample",
    )(u_t, state_t, attn_t, h_t, dw_w, dw_b, ln_g, ln_b, wo)


def _rope_tables(positions):
    half = HEAD_DIM // 2
    inv = ROPE_THETA ** (-np.arange(half, dtype=np.float64) / half)
    ang = np.asarray(positions, np.float64)[:, None] * inv[None, :]
    cos, sin = np.cos(ang), np.sin(ang)
    reps = LANES // HEAD_DIM
    cos_t = np.tile(np.concatenate([cos, cos], axis=1), (1, reps))
    sin_t = np.tile(np.concatenate([-sin, sin], axis=1), (1, reps))
    return jnp.asarray(cos_t, F32), jnp.asarray(sin_t, F32)


def kernel(x_prompt, x_sample, cache_k, cache_v, state_conv, ln_ffn1, ffn1_w_gate, ffn1_w_up,
           ffn1_w_down, ln_mix, w_in, q_norm, k_norm, conv_dw_w, conv_dw_b, conv_ln_g, conv_ln_b,
           w_out, ln_ffn2, ffn2_w_gate, ffn2_w_up, ffn2_w_down):
    batch, s_len, d_model = x_prompt.shape
    dec_batch, t_len, _ = x_sample.shape
    depth, _, buf_len, n_heads, head_dim = cache_k.shape
    assert depth == 1 and head_dim == HEAD_DIM
    width = n_heads * head_dim
    ch = conv_dw_w.shape[2]

    xp = x_prompt.reshape(batch * s_len, d_model)
    xs = x_sample.reshape(dec_batch * t_len, d_model)
    ts = dec_batch * t_len

    reps = width // HEAD_DIM
    gq = jnp.tile(q_norm, (1, reps))
    gk = jnp.tile(k_norm, (1, reps))
    heads_per_tile = MXU_DIM // HEAD_DIM
    pm = jnp.asarray(np.kron(np.eye(heads_per_tile), np.ones((HEAD_DIM, HEAD_DIM))), BF16)

    tiles = INPROJ_TILES
    pos_p = (np.arange(batch * s_len) % s_len).reshape(tiles, -1)
    pos_s = np.tile(PAST_LEN + np.arange(t_len), dec_batch).reshape(tiles, -1)
    cos_t, sin_t = _rope_tables(np.concatenate([pos_p, pos_s], axis=1).reshape(-1))

    hp, hs, (wi, wo) = _ffn(xp, xs, ln_ffn1, ffn1_w_gate[0], ffn1_w_up[0], ffn1_w_down[0],
                            casts=(w_in[0], w_out[0]))

    (qp, kp, vp, up, kt_p, vt_p), (qs, ks, vs, us) = _inproj(
        hp, hs, ln_mix, wi, gq, gk, cos_t, sin_t, pm, seq_len=s_len, tiles=tiles)

    kt = jnp.transpose(cache_k[0], (0, 2, 3, 1)).reshape(dec_batch, width, buf_len)
    vt = jnp.transpose(cache_v[0], (0, 2, 3, 1)).reshape(dec_batch, width, buf_len)
    attn_p, attn_s = _attention(qp, kp, vp, batch, s_len, qs, ks, vs, kt, vt, t_len)

    hp = _conv_out_prompt(up, attn_p, hp, conv_dw_w[0], conv_dw_b, conv_ln_g, conv_ln_b, wo, s_len)
    state_t = jnp.transpose(state_conv[0], (1, 0, 2))
    time_major = lambda a: jnp.transpose(a.reshape(dec_batch, t_len, a.shape[-1]), (1, 0, 2))
    hs_t, new_state_t = _conv_out_sample(time_major(us), state_t, time_major(attn_s),
                                         time_major(hs), conv_dw_w[0], conv_dw_b,
                                         conv_ln_g, conv_ln_b, wo)
    hs = jnp.transpose(hs_t, (1, 0, 2)).reshape(ts, d_model)

    yp, ys, _ = _ffn(hp, hs, ln_ffn2, ffn2_w_gate[0], ffn2_w_up[0], ffn2_w_down[0])

    n_state = state_conv.shape[2]
    seq_major = lambda a: jnp.transpose(a.reshape(batch, n_heads, head_dim, s_len), (0, 3, 1, 2))[None]
    kv_s = (1, dec_batch, t_len, n_heads, head_dim)
    return (yp.reshape(batch, s_len, d_model),
            ys.reshape(dec_batch, t_len, d_model),
            seq_major(kt_p), seq_major(vt_p),
            up.reshape(batch, s_len, ch)[:, s_len - n_state:][None],
            ks.reshape(kv_s), vs.reshape(kv_s),
            jnp.transpose(new_state_t, (1, 0, 2))[None])
```

```python
import functools

import numpy as np
import jax
import jax.numpy as jnp
from jax import lax
from jax.experimental import pallas as pl
from jax.experimental.pallas import tpu as pltpu

F32 = jnp.float32
BF16 = jnp.bfloat16

HEAD_DIM = 64
PAST_LEN = 16384
DILATED = ((128, 1), (512, 4), (2048, 16))
Q_BLOCK = 128
ROPE_THETA = 10000.0
EPS = 1e-6
FFN_RES = 0.5
NEG = -1e30

LANES = 128
MXU_DIM = 256
BF16_ROWS = 16
VMEM_LIMIT = 56 * 1024 * 1024
FFN_VMEM_LIMIT = 60 * 1024 * 1024
INPROJ_TILES = 16

NT_DIMS = (((1,), (1,)), ((), ()))


def _silu(x):
    return x * jax.nn.sigmoid(x)


def _rms_rows(x, gain):
    ms = jnp.mean(x * x, axis=-1, keepdims=True)
    return x * lax.rsqrt(ms + EPS) * gain


FFN_ROW_CHUNK = 512


def _ffn_body(xp_ref, xs_ref, g_ref, wg_ref, wu_ref, wd_ref, *rest, n_casts):
    cast_in, rest = rest[:n_casts], rest[n_casts:]
    op_ref, os_ref = rest[:2]
    cast_out, (h_ref, wgb_ref, wub_ref, wdb_ref) = rest[2:2 + n_casts], rest[2 + n_casts:]
    tp = xp_ref.shape[0]
    ts = xs_ref.shape[0]

    for src, dst in zip(cast_in, cast_out):
        dst[...] = src[...].astype(BF16)

    @pl.when(pl.program_id(1) == 0)
    def _():
        for r0 in range(0, tp, FFN_ROW_CHUNK):
            x = xp_ref[r0:r0 + FFN_ROW_CHUNK, :]
            h_ref[r0:r0 + FFN_ROW_CHUNK, :] = _rms_rows(x, g_ref[...]).astype(BF16)
            op_ref[r0:r0 + FFN_ROW_CHUNK, :] = x
        x = xs_ref[...]
        h_ref[tp:tp + ts, :] = _rms_rows(x, g_ref[...]).astype(BF16)
        os_ref[...] = x

    wgb_ref[...] = wg_ref[...].astype(BF16)
    wub_ref[...] = wu_ref[...].astype(BF16)
    wdb_ref[...] = wd_ref[...].astype(BF16)

    def half_step(h):
        g = jnp.dot(h, wgb_ref[...], preferred_element_type=F32)
        u = jnp.dot(h, wub_ref[...], preferred_element_type=F32)
        a = (_silu(g) * u * FFN_RES).astype(BF16)
        return jnp.dot(a, wdb_ref[...], preferred_element_type=F32)

    chunk = (tp + ts) // 2
    op_ref[0:chunk, :] += half_step(h_ref[0:chunk, :])
    res = half_step(h_ref[chunk:tp + ts, :])
    op_ref[chunk:tp, :] += res[0:tp - chunk, :]
    os_ref[...] += res[tp - chunk:chunk, :]


def _ffn(xp, xs, gain, wg, wu, wd, casts=(), tiles=8, tf=256):
    tp_all, d = xp.shape
    ts_all = xs.shape[0]
    dff = wg.shape[1]
    tp, ts = tp_all // tiles, ts_all // tiles
    assert tp % FFN_ROW_CHUNK == 0 and ts % 16 == 0 and dff % tf == 0 and (tp + ts) % 32 == 0
    row = lambda rows: pl.BlockSpec((rows, d), lambda i, f: (i, 0))
    row_in = lambda rows: pl.BlockSpec(
        (rows, d), lambda i, f: (jnp.where(f == 0, i, jnp.minimum(i + 1, tiles - 1)), 0))
    steps = dff // tf
    cast_specs, cast_shapes = [], []
    for m in casts:
        slabs = m.shape[0] // BF16_ROWS
        assert m.shape[0] % BF16_ROWS == 0 and slabs <= tiles * steps
        cast_specs.append(pl.BlockSpec(
            (BF16_ROWS, m.shape[1]),
            lambda i, f, slabs=slabs: (jnp.minimum(i * steps + f, slabs - 1), 0)))
        cast_shapes.append(jax.ShapeDtypeStruct(m.shape, BF16))
    outs = pl.pallas_call(
        functools.partial(_ffn_body, n_casts=len(casts)),
        grid=(tiles, steps),
        in_specs=[
            row_in(tp), row_in(ts),
            pl.BlockSpec((1, d), lambda i, f: (0, 0)),
            pl.BlockSpec((d, tf), lambda i, f: (0, f)),
            pl.BlockSpec((d, tf), lambda i, f: (0, f)),
            pl.BlockSpec((tf, d), lambda i, f: (f, 0)),
        ] + cast_specs,
        out_specs=[row(tp), row(ts)] + cast_specs,
        out_shape=[jax.ShapeDtypeStruct((tp_all, d), F32), jax.ShapeDtypeStruct((ts_all, d), F32)]
        + cast_shapes,
        scratch_shapes=[pltpu.VMEM((tp + ts, d), BF16), pltpu.VMEM((d, tf), BF16),
                        pltpu.VMEM((d, tf), BF16), pltpu.VMEM((tf, d), BF16)],
        compiler_params=pltpu.CompilerParams(
            dimension_semantics=("arbitrary", "arbitrary"), vmem_limit_bytes=FFN_VMEM_LIMIT),
        name="ffn",
    )(xp, xs, gain, wg, wu, wd, *casts)
    return outs[0], outs[1], tuple(outs[2:])


def _inproj_load(xp_ref, xs_ref, g_ref, hn_ref):
    tp = xp_ref.shape[0]
    hn_ref[0:tp, :] = _rms_rows(xp_ref[...], g_ref[...]).astype(BF16)
    hn_ref[tp:, :] = _rms_rows(xs_ref[...], g_ref[...]).astype(BF16)


def _inproj_matmul(hn_ref, w_ref, z_ref, slot, group, c=None):
    width = z_ref.shape[2]
    if c is None:
        lo, n = group * width, width
        z_ref[slot] = jnp.dot(hn_ref[...], w_ref[:, lo:lo + n], preferred_element_type=F32)
    else:
        lo = group * width + c * MXU_DIM
        z_ref[slot, :, c * MXU_DIM:(c + 1) * MXU_DIM] = jnp.dot(
            hn_ref[...], w_ref[:, lo:lo + MXU_DIM], preferred_element_type=F32)


def _inproj_qk_body(xp_ref, xs_ref, g_ref, w_ref, gq_ref, gk_ref, cos_ref, sin_ref, pm_ref,
                    qp_ref, qs_ref, kp_ref, ks_ref, kt_ref, hn_ref, z_ref):
    j = pl.program_id(1)
    tp = xp_ref.shape[0]
    tm, width = z_ref.shape[1], z_ref.shape[2]
    lane = lax.broadcasted_iota(jnp.int32, (tm, LANES), 1)
    first_half = (lane % HEAD_DIM) < (HEAD_DIM // 2)

    def norm_rope(slot, gain_ref, p_ref, s_ref, t_ref, scale, next_group):
        pm = pm_ref[...]
        for c in range(width // MXU_DIM):
            sl = slice(c * MXU_DIM, (c + 1) * MXU_DIM)
            if next_group is not None:
                _inproj_matmul(hn_ref, w_ref, z_ref, 1 - slot, next_group, c)
            z = z_ref[slot, :, sl]
            zz = z * z
            hi = zz.astype(BF16)
            lo = (zz - hi.astype(F32)).astype(BF16)
            ss = (jnp.dot(hi, pm, preferred_element_type=F32)
                  + jnp.dot(lo, pm, preferred_element_type=F32))
            zn = z * lax.rsqrt(ss * (1.0 / HEAD_DIM) + EPS) * gain_ref[:, sl]
            for e in range(MXU_DIM // LANES):
                x = zn[:, e * LANES:(e + 1) * LANES]
                partner = jnp.where(first_half,
                                    pltpu.roll(x, LANES - HEAD_DIM // 2, 1),
                                    pltpu.roll(x, HEAD_DIM // 2, 1))
                y = x * cos_ref[...] + partner * sin_ref[...]
                if scale != 1.0:
                    y = y * scale
                lanes = slice(c * MXU_DIM + e * LANES, c * MXU_DIM + (e + 1) * LANES)
                p_ref[:, lanes] = y[0:tp]
                s_ref[:, lanes] = y[tp:tm]
                if t_ref is not None:
                    t_ref[0, lanes, :] = y[0:tp].T

    @pl.when(j == 0)
    def _():
        _inproj_load(xp_ref, xs_ref, g_ref, hn_ref)
        _inproj_matmul(hn_ref, w_ref, z_ref, 0, 1)

    @pl.when(j == 1)
    def _():
        norm_rope(0, gk_ref, kp_ref, ks_ref, kt_ref, 1.0, 0)

    @pl.when(j == 2)
    def _():
        norm_rope(1, gq_ref, qp_ref, qs_ref, None, HEAD_DIM ** -0.5, None)


def _inproj_vu_body(xp_ref, xs_ref, g_ref, wv_ref, wa_ref, wb_ref, vp_ref, vs_ref, vt_ref,
                    up_ref, us_ref, hn_ref, z_ref):
    j = pl.program_id(1)
    tp = xp_ref.shape[0]
    tm, width = z_ref.shape[1], z_ref.shape[2]

    @pl.when(j == 0)
    def _():
        _inproj_load(xp_ref, xs_ref, g_ref, hn_ref)
        _inproj_matmul(hn_ref, wv_ref, z_ref, 0, 0)

    @pl.when(j == 1)
    def _():
        for c in range(width // MXU_DIM):
            sl = slice(c * MXU_DIM, (c + 1) * MXU_DIM)
            _inproj_matmul(hn_ref, wa_ref, z_ref, 1, 0, c)
            v = z_ref[0, :, sl]
            vp_ref[:, sl] = v[0:tp]
            vs_ref[:, sl] = v[tp:tm]
            vt_ref[0, sl, :] = v[0:tp].T

    @pl.when(j == 2)
    def _():
        _inproj_matmul(hn_ref, wb_ref, z_ref, 0, 0)

    @pl.when(j == 3)
    def _():
        u = z_ref[1] * jax.nn.sigmoid(z_ref[0])
        up_ref[...] = u[0:tp]
        us_ref[...] = u[tp:tm]


def _inproj(xp, xs, gain, w, gq, gk, cos_t, sin_t, pm, seq_len, tiles=16):
    tp_all, d = xp.shape
    ts_all = xs.shape[0]
    tp, ts = tp_all // tiles, ts_all // tiles
    tm = tp + ts
    width = w.shape[1] // 5
    tiles_per_seq = seq_len // tp
    assert ts % SUBLANES == 0 and tm % 16 == 0 and seq_len % tp == 0 and cos_t.shape[0] == tiles * tm

    x_in = lambda rows: pl.BlockSpec(
        (rows, d), lambda i, j: (jnp.where(j == 0, i, jnp.minimum(i + 1, tiles - 1)), 0))
    const = lambda shape: pl.BlockSpec(shape, lambda i, j: (0,) * len(shape))
    w_cols = lambda first, groups: pl.BlockSpec(
        (d, groups * width), lambda i, j: (0, first // groups))

    def tile_at(i, j, ready):
        return jnp.where(j >= ready, i, jnp.maximum(i - 1, 0))

    def rows_out(rows, ready):
        return pl.BlockSpec((rows, width), lambda i, j: (tile_at(i, j, ready), 0))

    def cols_out(ready):
        def index_map(i, j):
            tile = tile_at(i, j, ready)
            return (tile // tiles_per_seq, 0, tile % tiles_per_seq)
        return pl.BlockSpec((1, width, tp), index_map)

    f32 = lambda *shape: jax.ShapeDtypeStruct(shape, F32)
    scratch = [pltpu.VMEM((tm, d), BF16), pltpu.VMEM((2, tm, width), F32)]
    params = pltpu.CompilerParams(
        dimension_semantics=("arbitrary", "arbitrary"), vmem_limit_bytes=VMEM_LIMIT)
    tab_spec = pl.BlockSpec((tm, LANES), lambda i, j: (i, 0))

    qp, qs, kp, ks, kt = pl.pallas_call(
        _inproj_qk_body,
        grid=(tiles, 3),
        in_specs=[x_in(tp), x_in(ts), const((1, d)), w_cols(0, 2),
                  const((1, width)), const((1, width)), tab_spec, tab_spec, const(pm.shape)],
        out_specs=[rows_out(tp, 2), rows_out(ts, 2), rows_out(tp, 1), rows_out(ts, 1), cols_out(1)],
        out_shape=[f32(tp_all, width), f32(ts_all, width), f32(tp_all, width), f32(ts_all, width),
                   f32(tp_all // seq_len, width, seq_len)],
        scratch_shapes=scratch, compiler_params=params, name="inproj_qk",
    )(xp, xs, gain, w, gq, gk, cos_t, sin_t, pm)

    vp, vs, vt, up, us = pl.pallas_call(
        _inproj_vu_body,
        grid=(tiles, 4),
        in_specs=[x_in(tp), x_in(ts), const((1, d)), w_cols(2, 1), w_cols(3, 1), w_cols(4, 1)],
        out_specs=[rows_out(tp, 1), rows_out(ts, 1), cols_out(1), rows_out(tp, 3), rows_out(ts, 3)],
        out_shape=[f32(tp_all, width), f32(ts_all, width), f32(tp_all // seq_len, width, seq_len),
                   f32(tp_all, width), f32(ts_all, width)],
        scratch_shapes=scratch, compiler_params=params, name="inproj_vu",
    )(xp, xs, gain, w, w, w)
    return (qp, kp, vp, up, kt, vt), (qs, ks, vs, us)


def _pattn_body(q_ref, k_ref, v_ref, o_ref, kt, on, ls, dils, span):
    s_len = q_ref.shape[0]
    nblk = s_len // Q_BLOCK

    def block_rows(n, d, bpc):
        if d == 1:
            return pl.ds(pl.multiple_of(n * Q_BLOCK, Q_BLOCK), Q_BLOCK)
        return pl.ds(n // bpc + (n % bpc) * (Q_BLOCK * d), Q_BLOCK, stride=d)

    for p, d in enumerate(dils):
        bpc = (s_len // d) // Q_BLOCK
        for n in range(nblk):
            kt[p, n] = k_ref[block_rows(n, d, bpc), :].T.astype(BF16)

    lane = lax.broadcasted_iota(jnp.int32, (Q_BLOCK, LANES), 1)
    head0 = lane < HEAD_DIM
    qi = lax.broadcasted_iota(jnp.int32, (Q_BLOCK, Q_BLOCK), 0)
    ki = lax.broadcasted_iota(jnp.int32, (Q_BLOCK, Q_BLOCK), 1)
    causal = ki <= qi
    band = (qi + Q_BLOCK - ki) <= span
    one = jnp.ones((), BF16)
    zero = jnp.zeros((), BF16)

    for p, d in enumerate(dils):
        bpc = (s_len // d) // Q_BLOCK
        use_prev = bpc > 1

        def block(n, carry, p=p, d=d, bpc=bpc, use_prev=use_prev):
            rows = block_rows(n, d, bpc)
            q = q_ref[rows, :].astype(BF16)
            vc = v_ref[rows, :].astype(BF16)
            ktc = kt[p, n]
            if use_prev:
                n_prev = jnp.maximum(n - 1, 0)
                ktp = kt[p, n_prev]
                vp = v_ref[block_rows(n_prev, d, bpc), :].astype(BF16)
                prev_ok = band & ((n % bpc) != 0)
            res = []
            for h in range(2):
                own = head0 if h == 0 else ~head0
                qh = jnp.where(own, q, zero)
                sc = jnp.where(causal, jnp.dot(qh, ktc, preferred_element_type=F32), NEG)
                if use_prev:
                    sp = jnp.where(prev_ok, jnp.dot(qh, ktp, preferred_element_type=F32), NEG)
                    m = jnp.max(jnp.maximum(sc, sp), axis=1, keepdims=True)
                else:
                    m = jnp.max(sc, axis=1, keepdims=True)
                o = jnp.dot(jnp.exp(sc - m).astype(BF16), jnp.where(own, vc, one),
                            preferred_element_type=F32)
                if use_prev:
                    o = o + jnp.dot(jnp.exp(sp - m).astype(BF16), jnp.where(own, vp, one),
                                    preferred_element_type=F32)
                res.append((m, o))
            (m0, o0), (m1, o1) = res
            l = pltpu.roll(jnp.where(head0, o1, o0), HEAD_DIM, 1)
            on[p, rows, :] = jnp.where(head0, o0, o1) / l
            ls[p, rows, :] = jnp.where(head0, m0, m1) + jnp.log(l)
            return carry

        lax.fori_loop(0, nblk, block, 0, unroll=8)

    def merge(c, carry):
        rows = pl.ds(pl.multiple_of(c * Q_BLOCK, Q_BLOCK), Q_BLOCK)
        lses = [ls[p, rows, :] for p in range(len(dils))]
        m = lses[0]
        for lp in lses[1:]:
            m = jnp.maximum(m, lp)
        num = jnp.zeros((Q_BLOCK, LANES), F32)
        den = jnp.zeros((Q_BLOCK, LANES), F32)
        for p, lp in enumerate(lses):
            e = jnp.exp(lp - m)
            num = num + on[p, rows, :] * e
            den = den + e
        o_ref[rows, :] = (num / den).astype(o_ref.dtype)
        return carry

    lax.fori_loop(0, nblk, merge, 0, unroll=2)


SAMPLE_KEY_CHUNK = 512


def _sattn_body(q_ref, kn_ref, vn_ref, kt_ref, vt_ref, mc_ref, mn_ref, o_ref, s_ref):
    t_len, width = q_ref.shape
    buf_len = kt_ref.shape[2]
    n_heads = width // HEAD_DIM
    rows = n_heads * t_len
    q = q_ref[...]
    q_rep = jnp.concatenate([q] * n_heads, axis=0)
    row_h = lax.broadcasted_iota(jnp.int32, (rows, width), 0) // t_len
    lane_h = lax.broadcasted_iota(jnp.int32, (rows, width), 1) // HEAD_DIM
    own = row_h == lane_h
    q_exp = jnp.where(own, q_rep, 0.0).astype(BF16)

    pad = jnp.zeros((mn_ref.shape[1] - t_len, width), F32)
    kn = jnp.concatenate([kn_ref[...], pad], axis=0).astype(BF16)
    vn = jnp.concatenate([vn_ref[...], pad], axis=0).astype(BF16)

    chunks = [slice(c, c + SAMPLE_KEY_CHUNK) for c in range(0, buf_len, SAMPLE_KEY_CHUNK)]
    m = None
    for sl in chunks:
        s = jnp.dot(q_exp, kt_ref[0, :, sl].astype(BF16), preferred_element_type=F32)
        s = jnp.where(mc_ref[:, sl] > 0, s, NEG)
        s_ref[:, sl] = s
        mx = jnp.max(s, axis=1, keepdims=True)
        m = mx if m is None else jnp.maximum(m, mx)
    sn = lax.dot_general(q_exp, kn, NT_DIMS, preferred_element_type=F32)
    mn = mn_ref[...]
    sn = jnp.where(mn > 0, sn, NEG)
    m = jnp.maximum(m, jnp.max(sn, axis=1, keepdims=True))
    pn = mn * jnp.exp(sn - m)
    l = jnp.sum(pn, axis=1, keepdims=True)
    o = jnp.dot(pn.astype(BF16), vn, preferred_element_type=F32)
    for sl in chunks:
        p = mc_ref[:, sl] * jnp.exp(s_ref[:, sl] - m)
        l = l + jnp.sum(p, axis=1, keepdims=True)
        o = o + lax.dot_general(p.astype(BF16), vt_ref[0, :, sl].astype(BF16), NT_DIMS,
                                preferred_element_type=F32)
    o = jnp.where(own, o / l, 0.0)
    out = o[0:t_len]
    for h in range(1, n_heads):
        out = out + o[h * t_len:(h + 1) * t_len]
    o_ref[...] = out


def _attn_body(q_ref, k_ref, v_ref, qs_ref, kn_ref, vn_ref, kt_ref, vt_ref, mc_ref, mn_ref,
               o_ref, os_ref, kt, on, ls, s_ref, *, dils, span):
    _pattn_body(q_ref, k_ref, v_ref, o_ref, kt, on, ls, dils, span)
    _sattn_body(qs_ref, kn_ref, vn_ref, kt_ref, vt_ref, mc_ref, mn_ref, os_ref, s_ref)


def _attention(q, k, v, batch, s_len, qs, kn, vn, cache_kt, cache_vt, t_len):
    t, width = q.shape
    dils = tuple(d for _, d in DILATED)
    spans = {w // d for w, d in DILATED}
    assert len(spans) == 1 and dils[0] == 1
    span = spans.pop()
    assert span == Q_BLOCK and all(s_len % (d * Q_BLOCK) == 0 for d in dils)
    npat = len(dils)
    nblk = s_len // Q_BLOCK
    pairs = width // LANES
    dec_batch, _, buf_len = cache_kt.shape
    assert dec_batch == batch * pairs and buf_len % SAMPLE_KEY_CHUNK == 0
    n_heads = width // HEAD_DIM
    mult = _sample_mult(t_len, buf_len)
    mc = np.tile(mult[:, :buf_len], (n_heads, 1))
    mn = np.zeros((n_heads * t_len, LANES), np.float32)
    mn[:, :t_len] = np.tile(mult[:, buf_len:], (n_heads, 1))

    spec = pl.BlockSpec((s_len, LANES), lambda b, hp: (b, hp))
    row_spec = pl.BlockSpec((t_len, width), lambda b, hp: (b * pairs + hp, 0))
    cache_spec = pl.BlockSpec((1, width, buf_len), lambda b, hp: (b * pairs + hp, 0, 0))
    const = lambda shape: pl.BlockSpec(shape, lambda b, hp: (0, 0))
    tok_f = pltpu.VMEM((npat, s_len, LANES), F32)
    return pl.pallas_call(
        functools.partial(_attn_body, dils=dils, span=span),
        grid=(batch, pairs),
        in_specs=[spec, spec, spec, row_spec, row_spec, row_spec, cache_spec, cache_spec,
                  const(mc.shape), const(mn.shape)],
        out_specs=[spec, row_spec],
        out_shape=[jax.ShapeDtypeStruct((t, width), BF16),
                   jax.ShapeDtypeStruct((dec_batch * t_len, width), F32)],
        scratch_shapes=[pltpu.VMEM((npat, nblk, LANES, Q_BLOCK), BF16), tok_f, tok_f,
                        pltpu.VMEM((n_heads * t_len, buf_len), F32)],
        compiler_params=pltpu.CompilerParams(
            dimension_semantics=("parallel", "parallel"), vmem_limit_bytes=VMEM_LIMIT),
        name="attention",
    )(q, k, v, qs, kn, vn, cache_kt, cache_vt, jnp.asarray(mc), jnp.asarray(mn))


def _sample_mult(t_len, buf_len):
    mult = np.zeros((t_len, buf_len + t_len), np.float32)
    for w, d in DILATED:
        for t in range(t_len):
            for j in range(w // d + 1):
                idx = buf_len + t - d * j
                if idx >= 0:
                    mult[t, idx] += 1.0
    return mult


def _ln_swish(y, g, b):
    mu = jnp.mean(y, axis=-1, keepdims=True)
    yc = y - mu
    var = jnp.mean(yc * yc, axis=-1, keepdims=True)
    return _silu(yc * lax.rsqrt(var + EPS) * g + b)


CONV_ROWS = 32
SUBLANES = 8


def _conv_out_prompt_body(ucur_ref, uprev_ref, attn_ref, h_ref, w_ref, b_ref, g_ref, beta_ref,
                          wo_ref, o_ref, ext_ref, sh_ref, wb_ref, conv_ref, *, tiles_per_seq):
    tm, ch = ucur_ref.shape
    halo = uprev_ref.shape[0]
    taps = w_ref.shape[0]
    first = (pl.program_id(0) % tiles_per_seq) == 0
    ext_ref[0:halo, :] = jnp.where(first, 0.0, uprev_ref[...])
    ext_ref[halo:halo + tm, :] = ucur_ref[...]
    sh_rows = sh_ref.shape[1]
    for s in range(1, SUBLANES):
        sh_ref[s - 1] = ext_ref[pl.ds(s, sh_rows), :]
    @pl.when(pl.program_id(0) == 0)
    def _():
        for w in range(taps):
            wb_ref[w] = jnp.broadcast_to(w_ref[w:w + 1, :], (SUBLANES, ch))
    off = halo - (taps - 1)
    groups = CONV_ROWS // SUBLANES

    for c in range(tm // CONV_ROWS):
        accs = [jnp.zeros((SUBLANES, ch), F32) for _ in range(groups)]
        for w in range(taps):
            s = (off + w) % SUBLANES
            wb = wb_ref[w]
            for g in range(groups):
                base = c * CONV_ROWS + g * SUBLANES + off + w - s
                x = (ext_ref[pl.ds(base, SUBLANES), :] if s == 0
                     else sh_ref[s - 1, pl.ds(base, SUBLANES), :])
                accs[g] = accs[g] + x * wb
        acc = jnp.concatenate(accs, axis=0)
        act = _ln_swish(acc + b_ref[...], g_ref[...], beta_ref[...])
        conv_ref[c * CONV_ROWS:(c + 1) * CONV_ROWS, :] = act.astype(BF16)
    aw = attn_ref.shape[1]
    o_ref[...] = (h_ref[...]
                  + jnp.dot(attn_ref[...], wo_ref[0:aw, :], preferred_element_type=F32)
                  + jnp.dot(conv_ref[...], wo_ref[aw:aw + ch, :], preferred_element_type=F32))


def _conv_out_prompt(u, attn, h, dw_w, dw_b, ln_g, ln_b, wo, s_len, tm=256, halo=32):
    t, ch = u.shape
    d = h.shape[1]
    aw = attn.shape[1]
    taps = dw_w.shape[0]
    assert taps - 1 <= halo and s_len % tm == 0 and tm % halo == 0 and halo % SUBLANES == 0
    hb = tm // halo
    vec = pl.BlockSpec((1, ch), lambda i: (0, 0))
    return pl.pallas_call(
        functools.partial(_conv_out_prompt_body, tiles_per_seq=s_len // tm),
        grid=(t // tm,),
        in_specs=[
            pl.BlockSpec((tm, ch), lambda i: (i, 0)),
            pl.BlockSpec((halo, ch), lambda i: (jnp.maximum(i * hb - 1, 0), 0)),
            pl.BlockSpec((tm, aw), lambda i: (i, 0)),
            pl.BlockSpec((tm, d), lambda i: (i, 0)),
            pl.BlockSpec((taps, ch), lambda i: (0, 0)),
            vec, vec, vec,
            pl.BlockSpec((aw + ch, d), lambda i: (0, 0)),
        ],
        out_specs=pl.BlockSpec((tm, d), lambda i: (i, 0)),
        out_shape=jax.ShapeDtypeStruct((t, d), F32),
        scratch_shapes=[pltpu.VMEM((halo + tm, ch), F32),
                        pltpu.VMEM((SUBLANES - 1, halo + tm - SUBLANES, ch), F32),
                        pltpu.VMEM((taps, SUBLANES, ch), F32),
                        pltpu.VMEM((tm, ch), BF16)],
        compiler_params=pltpu.CompilerParams(
            dimension_semantics=("arbitrary",), vmem_limit_bytes=VMEM_LIMIT),
        name="conv_out_prompt",
    )(u, u, attn, h, dw_w, dw_b, ln_g, ln_b, wo)


def _conv_out_sample_body(u_ref, st_ref, attn_ref, h_ref, w_ref, b_ref, g_ref, beta_ref, wo_ref,
                          o_ref, ns_ref, ext_ref, conv_ref, at_ref):
    n_state, batch, ch = st_ref.shape
    t_len = u_ref.shape[0]
    taps = w_ref.shape[0]
    ext_ref[0:n_state] = st_ref[...]
    ext_ref[n_state:n_state + t_len] = u_ref[...]
    ns_ref[...] = ext_ref[t_len:t_len + n_state]
    for t in range(t_len):
        acc = jnp.zeros((batch, ch), F32)
        for w in range(taps):
            acc = acc + ext_ref[t + w] * w_ref[w:w + 1, :]
        act = _ln_swish(acc + b_ref[...], g_ref[...], beta_ref[...])
        conv_ref[t * batch:(t + 1) * batch, :] = act.astype(BF16)
        at_ref[t * batch:(t + 1) * batch, :] = attn_ref[t].astype(BF16)
    aw = attn_ref.shape[2]
    proj = (jnp.dot(at_ref[...], wo_ref[0:aw, :], preferred_element_type=F32)
            + jnp.dot(conv_ref[...], wo_ref[aw:aw + ch, :], preferred_element_type=F32))
    for t in range(t_len):
        o_ref[t] = h_ref[t] + proj[t * batch:(t + 1) * batch, :]


def _conv_out_sample(u_t, state_t, attn_t, h_t, dw_w, dw_b, ln_g, ln_b, wo):
    t_len, batch, ch = u_t.shape
    d = h_t.shape[2]
    n_state = state_t.shape[0]
    assert dw_w.shape[0] == n_state + 1
    return pl.pallas_call(
        _conv_out_sample_body,
        out_shape=[jax.ShapeDtypeStruct((t_len, batch, d), F32),
                   jax.ShapeDtypeStruct((n_state, batch, ch), F32)],
        scratch_shapes=[pltpu.VMEM((n_state + t_len, batch, ch), F32),
                        pltpu.VMEM((t_len * batch, ch), BF16),
                        pltpu.VMEM((t_len * batch, attn_t.shape[2]), BF16)],
        compiler_params=pltpu.CompilerParams(vmem_limit_bytes=VMEM_LIMIT),
        name="conv_out_sample",
    )(u_t, state_t, attn_t, h_t, dw_w, dw_b, ln_g, ln_b, wo)


def _rope_tables(positions):
    half = HEAD_DIM // 2
    inv = ROPE_THETA ** (-np.arange(half, dtype=np.float64) / half)
    ang = np.asarray(positions, np.float64)[:, None] * inv[None, :]
    cos, sin = np.cos(ang), np.sin(ang)
    reps = LANES // HEAD_DIM
    cos_t = np.tile(np.concatenate([cos, cos], axis=1), (1, reps))
    sin_t = np.tile(np.concatenate([-sin, sin], axis=1), (1, reps))
    return jnp.asarray(cos_t, F32), jnp.asarray(sin_t, F32)


def kernel(x_prompt, x_sample, cache_k, cache_v, state_conv, ln_ffn1, ffn1_w_gate, ffn1_w_up,
           ffn1_w_down, ln_mix, w_in, q_norm, k_norm, conv_dw_w, conv_dw_b, conv_ln_g, conv_ln_b,
           w_out, ln_ffn2, ffn2_w_gate, ffn2_w_up, ffn2_w_down):
    batch, s_len, d_model = x_prompt.shape
    dec_batch, t_len, _ = x_sample.shape
    depth, _, buf_len, n_heads, head_dim = cache_k.shape
    assert depth == 1 and head_dim == HEAD_DIM
    width = n_heads * head_dim
    ch = conv_dw_w.shape[2]

    xp = x_prompt.reshape(batch * s_len, d_model)
    xs = x_sample.reshape(dec_batch * t_len, d_model)
    ts = dec_batch * t_len

    reps = width // HEAD_DIM
    gq = jnp.tile(q_norm, (1, reps))
    gk = jnp.tile(k_norm, (1, reps))
    heads_per_tile = MXU_DIM // HEAD_DIM
    pm = jnp.asarray(np.kron(np.eye(heads_per_tile), np.ones((HEAD_DIM, HEAD_DIM))), BF16)

    tiles = INPROJ_TILES
    pos_p = (np.arange(batch * s_len) % s_len).reshape(tiles, -1)
    pos_s = np.tile(PAST_LEN + np.arange(t_len), dec_batch).reshape(tiles, -1)
    cos_t, sin_t = _rope_tables(np.concatenate([pos_p, pos_s], axis=1).reshape(-1))

    hp, hs, (wi, wo) = _ffn(xp, xs, ln_ffn1, ffn1_w_gate[0], ffn1_w_up[0], ffn1_w_down[0],
                            casts=(w_in[0], w_out[0]))

    (qp, kp, vp, up, kt_p, vt_p), (qs, ks, vs, us) = _inproj(
        hp, hs, ln_mix, wi, gq, gk, cos_t, sin_t, pm, seq_len=s_len, tiles=tiles)

    kt = jnp.transpose(cache_k[0], (0, 2, 3, 1)).reshape(dec_batch, width, buf_len)
    vt = jnp.transpose(cache_v[0], (0, 2, 3, 1)).reshape(dec_batch, width, buf_len)
    attn_p, attn_s = _attention(qp, kp, vp, batch, s_len, qs, ks, vs, kt, vt, t_len)

    hp = _conv_out_prompt(up, attn_p, hp, conv_dw_w[0], conv_dw_b, conv_ln_g, conv_ln_b, wo, s_len)
    state_t = jnp.transpose(state_conv[0], (1, 0, 2))
    time_major = lambda a: jnp.transpose(a.reshape(dec_batch, t_len, a.shape[-1]), (1, 0, 2))
    hs_t, new_state_t = _conv_out_sample(time_major(us), state_t, time_major(attn_s),
                                         time_major(hs), conv_dw_w[0], conv_dw_b,
                                         conv_ln_g, conv_ln_b, wo)
    hs = jnp.transpose(hs_t, (1, 0, 2)).reshape(ts, d_model)

    yp, ys, _ = _ffn(hp, hs, ln_ffn2, ffn2_w_gate[0], ffn2_w_up[0], ffn2_w_down[0])

    n_state = state_conv.shape[2]
    seq_major = lambda a: jnp.transpose(a.reshape(batch, n_heads, head_dim, s_len), (0, 3, 1, 2))[None]
    kv_s = (1, dec_batch, t_len, n_heads, head_dim)
    return (yp.reshape(batch, s_len, d_model),
            ys.reshape(dec_batch, t_len, d_model),
            seq_major(kt_p), seq_major(vt_p),
            up.reshape(batch, s_len, ch)[:, s_len - n_state:][None],
            ks.reshape(kv_s), vs.reshape(kv_s),
            jnp.transpose(new_state_t, (1, 0, 2))[None])
```

```python
import functools

import numpy as np
import jax
import jax.numpy as jnp
from jax import lax
from jax.experimental import pallas as pl
from jax.experimental.pallas import tpu as pltpu

F32 = jnp.float32
BF16 = jnp.bfloat16

HEAD_DIM = 64
PAST_LEN = 16384
DILATED = ((128, 1), (512, 4), (2048, 16))
Q_BLOCK = 128
ROPE_THETA = 10000.0
EPS = 1e-6
FFN_RES = 0.5
NEG = -1e30

LANES = 128
MXU_DIM = 256
BF16_ROWS = 16
VMEM_LIMIT = 56 * 1024 * 1024
FFN_VMEM_LIMIT = 60 * 1024 * 1024
INPROJ_TILES = 16

NT_DIMS = (((1,), (1,)), ((), ()))


def _silu(x):
    return x * jax.nn.sigmoid(x)


def _rms_rows(x, gain):
    ms = jnp.mean(x * x, axis=-1, keepdims=True)
    return x * lax.rsqrt(ms + EPS) * gain


FFN_ROW_CHUNK = 512


def _ffn_body(xp_ref, xs_ref, g_ref, wg_ref, wu_ref, wd_ref, *rest, n_casts):
    cast_in, rest = rest[:n_casts], rest[n_casts:]
    op_ref, os_ref = rest[:2]
    cast_out, (h_ref, wgb_ref, wub_ref, wdb_ref) = rest[2:2 + n_casts], rest[2 + n_casts:]
    tp = xp_ref.shape[0]
    ts = xs_ref.shape[0]

    for src, dst in zip(cast_in, cast_out):
        dst[...] = src[...].astype(BF16)

    @pl.when(pl.program_id(1) == 0)
    def _():
        for r0 in range(0, tp, FFN_ROW_CHUNK):
            x = xp_ref[r0:r0 + FFN_ROW_CHUNK, :]
            h_ref[r0:r0 + FFN_ROW_CHUNK, :] = _rms_rows(x, g_ref[...]).astype(BF16)
            op_ref[r0:r0 + FFN_ROW_CHUNK, :] = x
        x = xs_ref[...]
        h_ref[tp:tp + ts, :] = _rms_rows(x, g_ref[...]).astype(BF16)
        os_ref[...] = x

    wgb_ref[...] = wg_ref[...].astype(BF16)
    wub_ref[...] = wu_ref[...].astype(BF16)
    wdb_ref[...] = wd_ref[...].astype(BF16)

    def half_step(h):
        g = jnp.dot(h, wgb_ref[...], preferred_element_type=F32)
        u = jnp.dot(h, wub_ref[...], preferred_element_type=F32)
        a = (_silu(g) * u * FFN_RES).astype(BF16)
        return jnp.dot(a, wdb_ref[...], preferred_element_type=F32)

    chunk = (tp + ts) // 2
    op_ref[0:chunk, :] += half_step(h_ref[0:chunk, :])
    res = half_step(h_ref[chunk:tp + ts, :])
    op_ref[chunk:tp, :] += res[0:tp - chunk, :]
    os_ref[...] += res[tp - chunk:chunk, :]


def _ffn(xp, xs, gain, wg, wu, wd, casts=(), tiles=8, tf=256):
    tp_all, d = xp.shape
    ts_all = xs.shape[0]
    dff = wg.shape[1]
    tp, ts = tp_all // tiles, ts_all // tiles
    assert tp % FFN_ROW_CHUNK == 0 and ts % 16 == 0 and dff % tf == 0 and (tp + ts) % 32 == 0
    row = lambda rows: pl.BlockSpec((rows, d), lambda i, f: (i, 0))
    row_in = lambda rows: pl.BlockSpec(
        (rows, d), lambda i, f: (jnp.where(f == 0, i, jnp.minimum(i + 1, tiles - 1)), 0))
    steps = dff // tf
    cast_specs, cast_shapes = [], []
    for m in casts:
        slabs = m.shape[0] // BF16_ROWS
        assert m.shape[0] % BF16_ROWS == 0 and slabs <= tiles * steps
        cast_specs.append(pl.BlockSpec(
            (BF16_ROWS, m.shape[1]),
            lambda i, f, slabs=slabs: (jnp.minimum(i * steps + f, slabs - 1), 0)))
        cast_shapes.append(jax.ShapeDtypeStruct(m.shape, BF16))
    outs = pl.pallas_call(
        functools.partial(_ffn_body, n_casts=len(casts)),
        grid=(tiles, steps),
        in_specs=[
            row_in(tp), row_in(ts),
            pl.BlockSpec((1, d), lambda i, f: (0, 0)),
            pl.BlockSpec((d, tf), lambda i, f: (0, f)),
            pl.BlockSpec((d, tf), lambda i, f: (0, f)),
            pl.BlockSpec((tf, d), lambda i, f: (f, 0)),
        ] + cast_specs,
        out_specs=[row(tp), row(ts)] + cast_specs,
        out_shape=[jax.ShapeDtypeStruct((tp_all, d), F32), jax.ShapeDtypeStruct((ts_all, d), F32)]
        + cast_shapes,
        scratch_shapes=[pltpu.VMEM((tp + ts, d), BF16), pltpu.VMEM((d, tf), BF16),
                        pltpu.VMEM((d, tf), BF16), pltpu.VMEM((tf, d), BF16)],
        compiler_params=pltpu.CompilerParams(
            dimension_semantics=("arbitrary", "arbitrary"), vmem_limit_bytes=FFN_VMEM_LIMIT),
        name="ffn",
    )(xp, xs, gain, wg, wu, wd, *casts)
    return outs[0], outs[1], tuple(outs[2:])


def _inproj_load(xp_ref, xs_ref, g_ref, hn_ref):
    tp = xp_ref.shape[0]
    hn_ref[0:tp, :] = _rms_rows(xp_ref[...], g_ref[...]).astype(BF16)
    hn_ref[tp:, :] = _rms_rows(xs_ref[...], g_ref[...]).astype(BF16)


def _inproj_matmul(hn_ref, w_ref, z_ref, slot, group, c=None):
    width = z_ref.shape[2]
    if c is None:
        lo, n = group * width, width
        z_ref[slot] = jnp.dot(hn_ref[...], w_ref[:, lo:lo + n], preferred_element_type=F32)
    else:
        lo = group * width + c * MXU_DIM
        z_ref[slot, :, c * MXU_DIM:(c + 1) * MXU_DIM] = jnp.dot(
            hn_ref[...], w_ref[:, lo:lo + MXU_DIM], preferred_element_type=F32)


def _inproj_qk_body(xp_ref, xs_ref, g_ref, w_ref, gq_ref, gk_ref, cos_ref, sin_ref, pm_ref,
                    qp_ref, qs_ref, kp_ref, ks_ref, kt_ref, hn_ref, z_ref):
    j = pl.program_id(1)
    tp = xp_ref.shape[0]
    tm, width = z_ref.shape[1], z_ref.shape[2]
    lane = lax.broadcasted_iota(jnp.int32, (tm, LANES), 1)
    first_half = (lane % HEAD_DIM) < (HEAD_DIM // 2)

    def norm_rope(slot, gain_ref, p_ref, s_ref, t_ref, scale, next_group):
        pm = pm_ref[...]
        for c in range(width // MXU_DIM):
            sl = slice(c * MXU_DIM, (c + 1) * MXU_DIM)
            if next_group is not None:
                _inproj_matmul(hn_ref, w_ref, z_ref, 1 - slot, next_group, c)
            z = z_ref[slot, :, sl]
            zz = z * z
            hi = zz.astype(BF16)
            lo = (zz - hi.astype(F32)).astype(BF16)
            ss = (jnp.dot(hi, pm, preferred_element_type=F32)
                  + jnp.dot(lo, pm, preferred_element_type=F32))
            zn = z * lax.rsqrt(ss * (1.0 / HEAD_DIM) + EPS) * gain_ref[:, sl]
            for e in range(MXU_DIM // LANES):
                x = zn[:, e * LANES:(e + 1) * LANES]
                partner = jnp.where(first_half,
                                    pltpu.roll(x, LANES - HEAD_DIM // 2, 1),
                                    pltpu.roll(x, HEAD_DIM // 2, 1))
                y = x * cos_ref[...] + partner * sin_ref[...]
                if scale != 1.0:
                    y = y * scale
                lanes = slice(c * MXU_DIM + e * LANES, c * MXU_DIM + (e + 1) * LANES)
                p_ref[:, lanes] = y[0:tp]
                s_ref[:, lanes] = y[tp:tm]
                if t_ref is not None:
                    t_ref[0, lanes, :] = y[0:tp].T

    @pl.when(j == 0)
    def _():
        _inproj_load(xp_ref, xs_ref, g_ref, hn_ref)
        _inproj_matmul(hn_ref, w_ref, z_ref, 0, 1)

    @pl.when(j == 1)
    def _():
        norm_rope(0, gk_ref, kp_ref, ks_ref, kt_ref, 1.0, 0)

    @pl.when(j == 2)
    def _():
        norm_rope(1, gq_ref, qp_ref, qs_ref, None, HEAD_DIM ** -0.5, None)


def _inproj_vu_body(xp_ref, xs_ref, g_ref, wv_ref, wa_ref, wb_ref, vp_ref, vs_ref, vt_ref,
                    up_ref, us_ref, hn_ref, z_ref):
    j = pl.program_id(1)
    tp = xp_ref.shape[0]
    tm, width = z_ref.shape[1], z_ref.shape[2]

    @pl.when(j == 0)
    def _():
        _inproj_load(xp_ref, xs_ref, g_ref, hn_ref)
        _inproj_matmul(hn_ref, wv_ref, z_ref, 0, 0)

    @pl.when(j == 1)
    def _():
        for c in range(width // MXU_DIM):
            sl = slice(c * MXU_DIM, (c + 1) * MXU_DIM)
            _inproj_matmul(hn_ref, wa_ref, z_ref, 1, 0, c)
            v = z_ref[0, :, sl]
            vp_ref[:, sl] = v[0:tp]
            vs_ref[:, sl] = v[tp:tm]
            vt_ref[0, sl, :] = v[0:tp].T

    @pl.when(j == 2)
    def _():
        _inproj_matmul(hn_ref, wb_ref, z_ref, 0, 0)

    @pl.when(j == 3)
    def _():
        u = z_ref[1] * jax.nn.sigmoid(z_ref[0])
        up_ref[...] = u[0:tp]
        us_ref[...] = u[tp:tm]


def _inproj(xp, xs, gain, w, gq, gk, cos_t, sin_t, pm, seq_len, tiles=16):
    tp_all, d = xp.shape
    ts_all = xs.shape[0]
    tp, ts = tp_all // tiles, ts_all // tiles
    tm = tp + ts
    width = w.shape[1] // 5
    tiles_per_seq = seq_len // tp
    assert ts % SUBLANES == 0 and tm % 16 == 0 and seq_len % tp == 0 and cos_t.shape[0] == tiles * tm

    x_in = lambda rows: pl.BlockSpec(
        (rows, d), lambda i, j: (jnp.where(j == 0, i, jnp.minimum(i + 1, tiles - 1)), 0))
    const = lambda shape: pl.BlockSpec(shape, lambda i, j: (0,) * len(shape))
    w_cols = lambda first, groups: pl.BlockSpec(
        (d, groups * width), lambda i, j: (0, first // groups))

    def tile_at(i, j, ready):
        return jnp.where(j >= ready, i, jnp.maximum(i - 1, 0))

    def rows_out(rows, ready):
        return pl.BlockSpec((rows, width), lambda i, j: (tile_at(i, j, ready), 0))

    def cols_out(ready):
        def index_map(i, j):
            tile = tile_at(i, j, ready)
            return (tile // tiles_per_seq, 0, tile % tiles_per_seq)
        return pl.BlockSpec((1, width, tp), index_map)

    f32 = lambda *shape: jax.ShapeDtypeStruct(shape, F32)
    scratch = [pltpu.VMEM((tm, d), BF16), pltpu.VMEM((2, tm, width), F32)]
    params = pltpu.CompilerParams(
        dimension_semantics=("arbitrary", "arbitrary"), vmem_limit_bytes=VMEM_LIMIT)
    tab_spec = pl.BlockSpec((tm, LANES), lambda i, j: (i, 0))

    qp, qs, kp, ks, kt = pl.pallas_call(
        _inproj_qk_body,
        grid=(tiles, 3),
        in_specs=[x_in(tp), x_in(ts), const((1, d)), w_cols(0, 2),
                  const((1, width)), const((1, width)), tab_spec, tab_spec, const(pm.shape)],
        out_specs=[rows_out(tp, 2), rows_out(ts, 2), rows_out(tp, 1), rows_out(ts, 1), cols_out(1)],
        out_shape=[f32(tp_all, width), f32(ts_all, width), f32(tp_all, width), f32(ts_all, width),
                   f32(tp_all // seq_len, width, seq_len)],
        scratch_shapes=scratch, compiler_params=params, name="inproj_qk",
    )(xp, xs, gain, w, gq, gk, cos_t, sin_t, pm)

    vp, vs, vt, up, us = pl.pallas_call(
        _inproj_vu_body,
        grid=(tiles, 4),
        in_specs=[x_in(tp), x_in(ts), const((1, d)), w_cols(2, 1), w_cols(3, 1), w_cols(4, 1)],
        out_specs=[rows_out(tp, 1), rows_out(ts, 1), cols_out(1), rows_out(tp, 3), rows_out(ts, 3)],
        out_shape=[f32(tp_all, width), f32(ts_all, width), f32(tp_all // seq_len, width, seq_len),
                   f32(tp_all, width), f32(ts_all, width)],
        scratch_shapes=scratch, compiler_params=params, name="inproj_vu",
    )(xp, xs, gain, w, w, w)
    return (qp, kp, vp, up, kt, vt), (qs, ks, vs, us)


def _pattn_body(q_ref, k_ref, kseq_ref, v_ref, o_ref, kt, on, ls, dils, span):
    s_len = q_ref.shape[0]
    nblk = s_len // Q_BLOCK

    def block_rows(n, d, bpc):
        if d == 1:
            return pl.ds(pl.multiple_of(n * Q_BLOCK, Q_BLOCK), Q_BLOCK)
        return pl.ds(n // bpc + (n % bpc) * (Q_BLOCK * d), Q_BLOCK, stride=d)

    for p, d in enumerate(dils):
        bpc = (s_len // d) // Q_BLOCK
        for n in range(nblk):
            if d == 1:
                kt[p, n] = kseq_ref[0, :, n * Q_BLOCK:(n + 1) * Q_BLOCK].astype(BF16)
            else:
                kt[p, n] = k_ref[block_rows(n, d, bpc), :].T.astype(BF16)

    lane = lax.broadcasted_iota(jnp.int32, (Q_BLOCK, LANES), 1)
    head0 = lane < HEAD_DIM
    qi = lax.broadcasted_iota(jnp.int32, (Q_BLOCK, Q_BLOCK), 0)
    ki = lax.broadcasted_iota(jnp.int32, (Q_BLOCK, Q_BLOCK), 1)
    causal_bias = jnp.where(ki <= qi, 0.0, NEG)
    band_bias = jnp.where((qi + Q_BLOCK - ki) <= span, 0.0, NEG)
    in_head = (jnp.where(head0, 1.0, 0.0).astype(BF16), jnp.where(head0, 0.0, 1.0).astype(BF16))

    for p, d in enumerate(dils):
        bpc = (s_len // d) // Q_BLOCK
        use_prev = bpc > 1

        def block(n, carry, p=p, d=d, bpc=bpc, use_prev=use_prev):
            rows = block_rows(n, d, bpc)
            q = q_ref[rows, :].astype(BF16)
            vc = v_ref[rows, :].astype(BF16)
            ktc = kt[p, n]
            if use_prev:
                n_prev = jnp.maximum(n - 1, 0)
                ktp = kt[p, n_prev]
                vp = v_ref[block_rows(n_prev, d, bpc), :].astype(BF16)
                prev_bias = band_bias + jnp.where((n % bpc) != 0, 0.0, NEG)
            res = []
            for h in range(2):
                own, other = in_head[h], in_head[1 - h]
                qh = q * own
                sc = jnp.dot(qh, ktc, preferred_element_type=F32) + causal_bias
                if use_prev:
                    sp = jnp.dot(qh, ktp, preferred_element_type=F32) + prev_bias
                    m = jnp.max(jnp.maximum(sc, sp), axis=1, keepdims=True)
                else:
                    m = jnp.max(sc, axis=1, keepdims=True)
                o = jnp.dot(jnp.exp(sc - m).astype(BF16), vc * own + other,
                            preferred_element_type=F32)
                if use_prev:
                    o = o + jnp.dot(jnp.exp(sp - m).astype(BF16), vp * own + other,
                                    preferred_element_type=F32)
                res.append((m, o))
            (m0, o0), (m1, o1) = res
            l = pltpu.roll(jnp.where(head0, o1, o0), HEAD_DIM, 1)
            on[p, rows, :] = jnp.where(head0, o0, o1) / l
            ls[p, rows, :] = jnp.where(head0, m0, m1) + jnp.log(l)
            return carry

        lax.fori_loop(0, nblk, block, 0, unroll=8)

    def merge(c, carry):
        rows = pl.ds(pl.multiple_of(c * Q_BLOCK, Q_BLOCK), Q_BLOCK)
        lses = [ls[p, rows, :] for p in range(len(dils))]
        m = lses[0]
        for lp in lses[1:]:
            m = jnp.maximum(m, lp)
        num = jnp.zeros((Q_BLOCK, LANES), F32)
        den = jnp.zeros((Q_BLOCK, LANES), F32)
        for p, lp in enumerate(lses):
            e = jnp.exp(lp - m)
            num = num + on[p, rows, :] * e
            den = den + e
        o_ref[rows, :] = (num / den).astype(o_ref.dtype)
        return carry

    lax.fori_loop(0, nblk, merge, 0, unroll=2)


SAMPLE_KEY_CHUNK = 512


def _sattn_body(q_ref, kn_ref, vn_ref, kt_ref, vt_ref, mc_ref, mn_ref, o_ref, s_ref):
    t_len, width = q_ref.shape
    buf_len = kt_ref.shape[2]
    n_heads = width // HEAD_DIM
    rows = n_heads * t_len
    q = q_ref[...]
    q_rep = jnp.concatenate([q] * n_heads, axis=0)
    row_h = lax.broadcasted_iota(jnp.int32, (rows, width), 0) // t_len
    lane_h = lax.broadcasted_iota(jnp.int32, (rows, width), 1) // HEAD_DIM
    own = row_h == lane_h
    q_exp = jnp.where(own, q_rep, 0.0).astype(BF16)

    pad = jnp.zeros((mn_ref.shape[1] - t_len, width), F32)
    kn = jnp.concatenate([kn_ref[...], pad], axis=0).astype(BF16)
    vn = jnp.concatenate([vn_ref[...], pad], axis=0).astype(BF16)

    chunks = [slice(c, c + SAMPLE_KEY_CHUNK) for c in range(0, buf_len, SAMPLE_KEY_CHUNK)]
    m = None
    for sl in chunks:
        s = jnp.dot(q_exp, kt_ref[0, :, sl].astype(BF16), preferred_element_type=F32)
        s = jnp.where(mc_ref[:, sl] > 0, s, NEG)
        s_ref[:, sl] = s
        mx = jnp.max(s, axis=1, keepdims=True)
        m = mx if m is None else jnp.maximum(m, mx)
    sn = lax.dot_general(q_exp, kn, NT_DIMS, preferred_element_type=F32)
    mn = mn_ref[...]
    sn = jnp.where(mn > 0, sn, NEG)
    m = jnp.maximum(m, jnp.max(sn, axis=1, keepdims=True))
    pn = mn * jnp.exp(sn - m)
    l = jnp.sum(pn, axis=1, keepdims=True)
    o = jnp.dot(pn.astype(BF16), vn, preferred_element_type=F32)
    for sl in chunks:
        p = mc_ref[:, sl] * jnp.exp(s_ref[:, sl] - m)
        l = l + jnp.sum(p, axis=1, keepdims=True)
        o = o + lax.dot_general(p.astype(BF16), vt_ref[0, :, sl].astype(BF16), NT_DIMS,
                                preferred_element_type=F32)
    o = jnp.where(own, o / l, 0.0)
    out = o[0:t_len]
    for h in range(1, n_heads):
        out = out + o[h * t_len:(h + 1) * t_len]
    o_ref[...] = out


def _attn_body(q_ref, k_ref, kseq_ref, v_ref, qs_ref, kn_ref, vn_ref, kt_ref, vt_ref, mc_ref,
               mn_ref, o_ref, os_ref, kt, on, ls, s_ref, *, dils, span):
    _pattn_body(q_ref, k_ref, kseq_ref, v_ref, o_ref, kt, on, ls, dils, span)
    _sattn_body(qs_ref, kn_ref, vn_ref, kt_ref, vt_ref, mc_ref, mn_ref, os_ref, s_ref)


def _attention(q, k, k_seq, v, batch, s_len, qs, kn, vn, cache_kt, cache_vt, t_len):
    t, width = q.shape
    dils = tuple(d for _, d in DILATED)
    spans = {w // d for w, d in DILATED}
    assert len(spans) == 1 and dils[0] == 1
    span = spans.pop()
    assert span == Q_BLOCK and all(s_len % (d * Q_BLOCK) == 0 for d in dils)
    npat = len(dils)
    nblk = s_len // Q_BLOCK
    pairs = width // LANES
    dec_batch, _, buf_len = cache_kt.shape
    assert dec_batch == batch * pairs and buf_len % SAMPLE_KEY_CHUNK == 0
    n_heads = width // HEAD_DIM
    mult = _sample_mult(t_len, buf_len)
    mc = np.tile(mult[:, :buf_len], (n_heads, 1))
    mn = np.zeros((n_heads * t_len, LANES), np.float32)
    mn[:, :t_len] = np.tile(mult[:, buf_len:], (n_heads, 1))

    spec = pl.BlockSpec((s_len, LANES), lambda b, hp: (b, hp))
    kseq_spec = pl.BlockSpec((1, LANES, s_len), lambda b, hp: (b, hp, 0))
    row_spec = pl.BlockSpec((t_len, width), lambda b, hp: (b * pairs + hp, 0))
    cache_spec = pl.BlockSpec((1, width, buf_len), lambda b, hp: (b * pairs + hp, 0, 0))
    const = lambda shape: pl.BlockSpec(shape, lambda b, hp: (0, 0))
    tok_f = pltpu.VMEM((npat, s_len, LANES), F32)
    return pl.pallas_call(
        functools.partial(_attn_body, dils=dils, span=span),
        grid=(batch, pairs),
        in_specs=[spec, spec, kseq_spec, spec, row_spec, row_spec, row_spec, cache_spec, cache_spec,
                  const(mc.shape), const(mn.shape)],
        out_specs=[spec, row_spec],
        out_shape=[jax.ShapeDtypeStruct((t, width), BF16),
                   jax.ShapeDtypeStruct((dec_batch * t_len, width), F32)],
        scratch_shapes=[pltpu.VMEM((npat, nblk, LANES, Q_BLOCK), BF16), tok_f, tok_f,
                        pltpu.VMEM((n_heads * t_len, buf_len), F32)],
        compiler_params=pltpu.CompilerParams(
            dimension_semantics=("parallel", "parallel"), vmem_limit_bytes=VMEM_LIMIT),
        name="attention",
    )(q, k, k_seq, v, qs, kn, vn, cache_kt, cache_vt, jnp.asarray(mc), jnp.asarray(mn))


def _sample_mult(t_len, buf_len):
    mult = np.zeros((t_len, buf_len + t_len), np.float32)
    for w, d in DILATED:
        for t in range(t_len):
            for j in range(w // d + 1):
                idx = buf_len + t - d * j
                if idx >= 0:
                    mult[t, idx] += 1.0
    return mult


def _ln_swish(y, g, b):
    mu = jnp.mean(y, axis=-1, keepdims=True)
    yc = y - mu
    var = jnp.mean(yc * yc, axis=-1, keepdims=True)
    return _silu(yc * lax.rsqrt(var + EPS) * g + b)


CONV_ROWS = 32
SUBLANES = 8


def _conv_out_prompt_body(ucur_ref, uprev_ref, attn_ref, h_ref, w_ref, b_ref, g_ref, beta_ref,
                          wo_ref, o_ref, ext_ref, sh_ref, wb_ref, conv_ref, *, tiles_per_seq):
    tm, ch = ucur_ref.shape
    halo = uprev_ref.shape[0]
    taps = w_ref.shape[0]
    first = (pl.program_id(0) % tiles_per_seq) == 0
    ext_ref[0:halo, :] = jnp.where(first, 0.0, uprev_ref[...])
    ext_ref[halo:halo + tm, :] = ucur_ref[...]
    sh_rows = sh_ref.shape[1]
    for s in range(1, SUBLANES):
        sh_ref[s - 1] = ext_ref[pl.ds(s, sh_rows), :]
    @pl.when(pl.program_id(0) == 0)
    def _():
        for w in range(taps):
            wb_ref[w] = jnp.broadcast_to(w_ref[w:w + 1, :], (SUBLANES, ch))
    off = halo - (taps - 1)
    groups = CONV_ROWS // SUBLANES

    for c in range(tm // CONV_ROWS):
        accs = [jnp.zeros((SUBLANES, ch), F32) for _ in range(groups)]
        for w in range(taps):
            s = (off + w) % SUBLANES
            wb = wb_ref[w]
            for g in range(groups):
                base = c * CONV_ROWS + g * SUBLANES + off + w - s
                x = (ext_ref[pl.ds(base, SUBLANES), :] if s == 0
                     else sh_ref[s - 1, pl.ds(base, SUBLANES), :])
                accs[g] = accs[g] + x * wb
        acc = jnp.concatenate(accs, axis=0)
        act = _ln_swish(acc + b_ref[...], g_ref[...], beta_ref[...])
        conv_ref[c * CONV_ROWS:(c + 1) * CONV_ROWS, :] = act.astype(BF16)
    aw = attn_ref.shape[1]
    o_ref[...] = (h_ref[...]
                  + jnp.dot(attn_ref[...], wo_ref[0:aw, :], preferred_element_type=F32)
                  + jnp.dot(conv_ref[...], wo_ref[aw:aw + ch, :], preferred_element_type=F32))


def _conv_out_prompt(u, attn, h, dw_w, dw_b, ln_g, ln_b, wo, s_len, tm=256, halo=32):
    t, ch = u.shape
    d = h.shape[1]
    aw = attn.shape[1]
    taps = dw_w.shape[0]
    assert taps - 1 <= halo and s_len % tm == 0 and tm % halo == 0 and halo % SUBLANES == 0
    hb = tm // halo
    vec = pl.BlockSpec((1, ch), lambda i: (0, 0))
    return pl.pallas_call(
        functools.partial(_conv_out_prompt_body, tiles_per_seq=s_len // tm),
        grid=(t // tm,),
        in_specs=[
            pl.BlockSpec((tm, ch), lambda i: (i, 0)),
            pl.BlockSpec((halo, ch), lambda i: (jnp.maximum(i * hb - 1, 0), 0)),
            pl.BlockSpec((tm, aw), lambda i: (i, 0)),
            pl.BlockSpec((tm, d), lambda i: (i, 0)),
            pl.BlockSpec((taps, ch), lambda i: (0, 0)),
            vec, vec, vec,
            pl.BlockSpec((aw + ch, d), lambda i: (0, 0)),
        ],
        out_specs=pl.BlockSpec((tm, d), lambda i: (i, 0)),
        out_shape=jax.ShapeDtypeStruct((t, d), F32),
        scratch_shapes=[pltpu.VMEM((halo + tm, ch), F32),
                        pltpu.VMEM((SUBLANES - 1, halo + tm - SUBLANES, ch), F32),
                        pltpu.VMEM((taps, SUBLANES, ch), F32),
                        pltpu.VMEM((tm, ch), BF16)],
        compiler_params=pltpu.CompilerParams(
            dimension_semantics=("arbitrary",), vmem_limit_bytes=VMEM_LIMIT),
        name="conv_out_prompt",
    )(u, u, attn, h, dw_w, dw_b, ln_g, ln_b, wo)


def _conv_out_sample_body(u_ref, st_ref, attn_ref, h_ref, w_ref, b_ref, g_ref, beta_ref, wo_ref,
                          o_ref, ns_ref, ext_ref, conv_ref, at_ref):
    n_state, batch, ch = st_ref.shape
    t_len = u_ref.shape[0]
    taps = w_ref.shape[0]
    ext_ref[0:n_state] = st_ref[...]
    ext_ref[n_state:n_state + t_len] = u_ref[...]
    ns_ref[...] = ext_ref[t_len:t_len + n_state]
    for t in range(t_len):
        acc = jnp.zeros((batch, ch), F32)
        for w in range(taps):
            acc = acc + ext_ref[t + w] * w_ref[w:w + 1, :]
        act = _ln_swish(acc + b_ref[...], g_ref[...], beta_ref[...])
        conv_ref[t * batch:(t + 1) * batch, :] = act.astype(BF16)
        at_ref[t * batch:(t + 1) * batch, :] = attn_ref[t].astype(BF16)
    aw = attn_ref.shape[2]
    proj = (jnp.dot(at_ref[...], wo_ref[0:aw, :], preferred_element_type=F32)
            + jnp.dot(conv_ref[...], wo_ref[aw:aw + ch, :], preferred_element_type=F32))
    for t in range(t_len):
        o_ref[t] = h_ref[t] + proj[t * batch:(t + 1) * batch, :]


def _conv_out_sample(u_t, state_t, attn_t, h_t, dw_w, dw_b, ln_g, ln_b, wo):
    t_len, batch, ch = u_t.shape
    d = h_t.shape[2]
    n_state = state_t.shape[0]
    assert dw_w.shape[0] == n_state + 1
    return pl.pallas_call(
        _conv_out_sample_body,
        out_shape=[jax.ShapeDtypeStruct((t_len, batch, d), F32),
                   jax.ShapeDtypeStruct((n_state, batch, ch), F32)],
        scratch_shapes=[pltpu.VMEM((n_state + t_len, batch, ch), F32),
                        pltpu.VMEM((t_len * batch, ch), BF16),
                        pltpu.VMEM((t_len * batch, attn_t.shape[2]), BF16)],
        compiler_params=pltpu.CompilerParams(vmem_limit_bytes=VMEM_LIMIT),
        name="conv_out_sample",
    )(u_t, state_t, attn_t, h_t, dw_w, dw_b, ln_g, ln_b, wo)


def _rope_tables(positions):
    half = HEAD_DIM // 2
    inv = ROPE_THETA ** (-np.arange(half, dtype=np.float64) / half)
    ang = np.asarray(positions, np.float64)[:, None] * inv[None, :]
    cos, sin = np.cos(ang), np.sin(ang)
    reps = LANES // HEAD_DIM
    cos_t = np.tile(np.concatenate([cos, cos], axis=1), (1, reps))
    sin_t = np.tile(np.concatenate([-sin, sin], axis=1), (1, reps))
    return jnp.asarray(cos_t, F32), jnp.asarray(sin_t, F32)


def kernel(x_prompt, x_sample, cache_k, cache_v, state_conv, ln_ffn1, ffn1_w_gate, ffn1_w_up,
           ffn1_w_down, ln_mix, w_in, q_norm, k_norm, conv_dw_w, conv_dw_b, conv_ln_g, conv_ln_b,
           w_out, ln_ffn2, ffn2_w_gate, ffn2_w_up, ffn2_w_down):
    batch, s_len, d_model = x_prompt.shape
    dec_batch, t_len, _ = x_sample.shape
    depth, _, buf_len, n_heads, head_dim = cache_k.shape
    assert depth == 1 and head_dim == HEAD_DIM
    width = n_heads * head_dim
    ch = conv_dw_w.shape[2]

    xp = x_prompt.reshape(batch * s_len, d_model)
    xs = x_sample.reshape(dec_batch * t_len, d_model)
    ts = dec_batch * t_len

    reps = width // HEAD_DIM
    gq = jnp.tile(q_norm, (1, reps))
    gk = jnp.tile(k_norm, (1, reps))
    heads_per_tile = MXU_DIM // HEAD_DIM
    pm = jnp.asarray(np.kron(np.eye(heads_per_tile), np.ones((HEAD_DIM, HEAD_DIM))), BF16)

    tiles = INPROJ_TILES
    pos_p = (np.arange(batch * s_len) % s_len).reshape(tiles, -1)
    pos_s = np.tile(PAST_LEN + np.arange(t_len), dec_batch).reshape(tiles, -1)
    cos_t, sin_t = _rope_tables(np.concatenate([pos_p, pos_s], axis=1).reshape(-1))

    hp, hs, (wi, wo) = _ffn(xp, xs, ln_ffn1, ffn1_w_gate[0], ffn1_w_up[0], ffn1_w_down[0],
                            casts=(w_in[0], w_out[0]))

    (qp, kp, vp, up, kt_p, vt_p), (qs, ks, vs, us) = _inproj(
        hp, hs, ln_mix, wi, gq, gk, cos_t, sin_t, pm, seq_len=s_len, tiles=tiles)

    kt = jnp.transpose(cache_k[0], (0, 2, 3, 1)).reshape(dec_batch, width, buf_len)
    vt = jnp.transpose(cache_v[0], (0, 2, 3, 1)).reshape(dec_batch, width, buf_len)
    attn_p, attn_s = _attention(qp, kp, kt_p, vp, batch, s_len, qs, ks, vs, kt, vt, t_len)

    hp = _conv_out_prompt(up, attn_p, hp, conv_dw_w[0], conv_dw_b, conv_ln_g, conv_ln_b, wo, s_len)
    state_t = jnp.transpose(state_conv[0], (1, 0, 2))
    time_major = lambda a: jnp.transpose(a.reshape(dec_batch, t_len, a.shape[-1]), (1, 0, 2))
    hs_t, new_state_t = _conv_out_sample(time_major(us), state_t, time_major(attn_s),
                                         time_major(hs), conv_dw_w[0], conv_dw_b,
                                         conv_ln_g, conv_ln_b, wo)
    hs = jnp.transpose(hs_t, (1, 0, 2)).reshape(ts, d_model)

    yp, ys, _ = _ffn(hp, hs, ln_ffn2, ffn2_w_gate[0], ffn2_w_up[0], ffn2_w_down[0])

    n_state = state_conv.shape[2]
    seq_major = lambda a: jnp.transpose(a.reshape(batch, n_heads, head_dim, s_len), (0, 3, 1, 2))[None]
    kv_s = (1, dec_batch, t_len, n_heads, head_dim)
    return (yp.reshape(batch, s_len, d_model),
            ys.reshape(dec_batch, t_len, d_model),
            seq_major(kt_p), seq_major(vt_p),
            up.reshape(batch, s_len, ch)[:, s_len - n_state:][None],
            ks.reshape(kv_s), vs.reshape(kv_s),
            jnp.transpose(new_state_t, (1, 0, 2))[None])
```

```python
import functools

import numpy as np
import jax
import jax.numpy as jnp
from jax import lax
from jax.experimental import pallas as pl
from jax.experimental.pallas import tpu as pltpu

F32 = jnp.float32
BF16 = jnp.bfloat16

HEAD_DIM = 64
PAST_LEN = 16384
DILATED = ((128, 1), (512, 4), (2048, 16))
Q_BLOCK = 128
ROPE_THETA = 10000.0
EPS = 1e-6
FFN_RES = 0.5
NEG = -1e30

LANES = 128
MXU_DIM = 256
BF16_ROWS = 16
VMEM_LIMIT = 56 * 1024 * 1024
FFN_VMEM_LIMIT = 60 * 1024 * 1024
INPROJ_TILES = 16

NT_DIMS = (((1,), (1,)), ((), ()))


def _silu(x):
    return x * jax.nn.sigmoid(x)


def _rms_rows(x, gain):
    ms = jnp.mean(x * x, axis=-1, keepdims=True)
    return x * lax.rsqrt(ms + EPS) * gain


FFN_ROW_CHUNK = 512


def _ffn_body(xp_ref, xs_ref, g_ref, wg_ref, wu_ref, wd_ref, *rest, n_casts):
    cast_in, rest = rest[:n_casts], rest[n_casts:]
    op_ref, os_ref = rest[:2]
    cast_out, (h_ref, wgb_ref, wub_ref, wdb_ref) = rest[2:2 + n_casts], rest[2 + n_casts:]
    tp = xp_ref.shape[0]
    ts = xs_ref.shape[0]

    for src, dst in zip(cast_in, cast_out):
        dst[...] = src[...].astype(BF16)

    @pl.when(pl.program_id(1) == 0)
    def _():
        for r0 in range(0, tp, FFN_ROW_CHUNK):
            x = xp_ref[r0:r0 + FFN_ROW_CHUNK, :]
            h_ref[r0:r0 + FFN_ROW_CHUNK, :] = _rms_rows(x, g_ref[...]).astype(BF16)
            op_ref[r0:r0 + FFN_ROW_CHUNK, :] = x
        x = xs_ref[...]
        h_ref[tp:tp + ts, :] = _rms_rows(x, g_ref[...]).astype(BF16)
        os_ref[...] = x

    wgb_ref[...] = wg_ref[...].astype(BF16)
    wub_ref[...] = wu_ref[...].astype(BF16)
    wdb_ref[...] = wd_ref[...].astype(BF16)

    def half_step(h):
        g = jnp.dot(h, wgb_ref[...], preferred_element_type=F32)
        u = jnp.dot(h, wub_ref[...], preferred_element_type=F32)
        a = (_silu(g) * u * FFN_RES).astype(BF16)
        return jnp.dot(a, wdb_ref[...], preferred_element_type=F32)

    chunk = (tp + ts) // 2
    op_ref[0:chunk, :] += half_step(h_ref[0:chunk, :])
    res = half_step(h_ref[chunk:tp + ts, :])
    op_ref[chunk:tp, :] += res[0:tp - chunk, :]
    os_ref[...] += res[tp - chunk:chunk, :]


def _ffn(xp, xs, gain, wg, wu, wd, casts=(), tiles=8, tf=256):
    tp_all, d = xp.shape
    ts_all = xs.shape[0]
    dff = wg.shape[1]
    tp, ts = tp_all // tiles, ts_all // tiles
    assert tp % FFN_ROW_CHUNK == 0 and ts % 16 == 0 and dff % tf == 0 and (tp + ts) % 32 == 0
    row = lambda rows: pl.BlockSpec((rows, d), lambda i, f: (i, 0))
    row_in = lambda rows: pl.BlockSpec(
        (rows, d), lambda i, f: (jnp.where(f == 0, i, jnp.minimum(i + 1, tiles - 1)), 0))
    steps = dff // tf
    cast_specs, cast_shapes = [], []
    for m in casts:
        slabs = m.shape[0] // BF16_ROWS
        assert m.shape[0] % BF16_ROWS == 0 and slabs <= tiles * steps
        cast_specs.append(pl.BlockSpec(
            (BF16_ROWS, m.shape[1]),
            lambda i, f, slabs=slabs: (jnp.minimum(i * steps + f, slabs - 1), 0)))
        cast_shapes.append(jax.ShapeDtypeStruct(m.shape, BF16))
    outs = pl.pallas_call(
        functools.partial(_ffn_body, n_casts=len(casts)),
        grid=(tiles, steps),
        in_specs=[
            row_in(tp), row_in(ts),
            pl.BlockSpec((1, d), lambda i, f: (0, 0)),
            pl.BlockSpec((d, tf), lambda i, f: (0, f)),
            pl.BlockSpec((d, tf), lambda i, f: (0, f)),
            pl.BlockSpec((tf, d), lambda i, f: (f, 0)),
        ] + cast_specs,
        out_specs=[row(tp), row(ts)] + cast_specs,
        out_shape=[jax.ShapeDtypeStruct((tp_all, d), F32), jax.ShapeDtypeStruct((ts_all, d), F32)]
        + cast_shapes,
        scratch_shapes=[pltpu.VMEM((tp + ts, d), BF16), pltpu.VMEM((d, tf), BF16),
                        pltpu.VMEM((d, tf), BF16), pltpu.VMEM((tf, d), BF16)],
        compiler_params=pltpu.CompilerParams(
            dimension_semantics=("arbitrary", "arbitrary"), vmem_limit_bytes=FFN_VMEM_LIMIT),
        name="ffn",
    )(xp, xs, gain, wg, wu, wd, *casts)
    return outs[0], outs[1], tuple(outs[2:])


def _inproj_load(xp_ref, xs_ref, g_ref, hn_ref):
    tp = xp_ref.shape[0]
    hn_ref[0:tp, :] = _rms_rows(xp_ref[...], g_ref[...]).astype(BF16)
    hn_ref[tp:, :] = _rms_rows(xs_ref[...], g_ref[...]).astype(BF16)


def _inproj_matmul(hn_ref, w_ref, z_ref, slot, group, c=None):
    width = z_ref.shape[2]
    if c is None:
        lo, n = group * width, width
        z_ref[slot] = jnp.dot(hn_ref[...], w_ref[:, lo:lo + n], preferred_element_type=F32)
    else:
        lo = group * width + c * MXU_DIM
        z_ref[slot, :, c * MXU_DIM:(c + 1) * MXU_DIM] = jnp.dot(
            hn_ref[...], w_ref[:, lo:lo + MXU_DIM], preferred_element_type=F32)


def _inproj_qk_body(xp_ref, xs_ref, g_ref, w_ref, gq_ref, gk_ref, cos_ref, sin_ref, pm_ref,
                    qp_ref, qs_ref, kp_ref, ks_ref, kt_ref, hn_ref, z_ref):
    j = pl.program_id(1)
    tp = xp_ref.shape[0]
    tm, width = z_ref.shape[1], z_ref.shape[2]
    lane = lax.broadcasted_iota(jnp.int32, (tm, LANES), 1)
    first_half = (lane % HEAD_DIM) < (HEAD_DIM // 2)

    def norm_rope(slot, gain_ref, p_ref, s_ref, t_ref, scale, next_group):
        pm = pm_ref[...]
        for c in range(width // MXU_DIM):
            sl = slice(c * MXU_DIM, (c + 1) * MXU_DIM)
            if next_group is not None:
                _inproj_matmul(hn_ref, w_ref, z_ref, 1 - slot, next_group, c)
            z = z_ref[slot, :, sl]
            zz = z * z
            hi = zz.astype(BF16)
            lo = (zz - hi.astype(F32)).astype(BF16)
            ss = (jnp.dot(hi, pm, preferred_element_type=F32)
                  + jnp.dot(lo, pm, preferred_element_type=F32))
            zn = z * lax.rsqrt(ss * (1.0 / HEAD_DIM) + EPS) * gain_ref[:, sl]
            for e in range(MXU_DIM // LANES):
                x = zn[:, e * LANES:(e + 1) * LANES]
                partner = jnp.where(first_half,
                                    pltpu.roll(x, LANES - HEAD_DIM // 2, 1),
                                    pltpu.roll(x, HEAD_DIM // 2, 1))
                y = x * cos_ref[...] + partner * sin_ref[...]
                if scale != 1.0:
                    y = y * scale
                lanes = slice(c * MXU_DIM + e * LANES, c * MXU_DIM + (e + 1) * LANES)
                p_ref[:, lanes] = y[0:tp]
                s_ref[:, lanes] = y[tp:tm]
                if t_ref is not None:
                    t_ref[0, lanes, :] = y[0:tp].T

    @pl.when(j == 0)
    def _():
        _inproj_load(xp_ref, xs_ref, g_ref, hn_ref)
        _inproj_matmul(hn_ref, w_ref, z_ref, 0, 1)

    @pl.when(j == 1)
    def _():
        norm_rope(0, gk_ref, kp_ref, ks_ref, kt_ref, 1.0, 0)

    @pl.when(j == 2)
    def _():
        norm_rope(1, gq_ref, qp_ref, qs_ref, None, HEAD_DIM ** -0.5, None)


def _inproj_vu_body(xp_ref, xs_ref, g_ref, wv_ref, wa_ref, wb_ref, vp_ref, vs_ref, vt_ref,
                    up_ref, us_ref, hn_ref, z_ref):
    j = pl.program_id(1)
    tp = xp_ref.shape[0]
    tm, width = z_ref.shape[1], z_ref.shape[2]

    @pl.when(j == 0)
    def _():
        _inproj_load(xp_ref, xs_ref, g_ref, hn_ref)
        _inproj_matmul(hn_ref, wv_ref, z_ref, 0, 0)

    @pl.when(j == 1)
    def _():
        for c in range(width // MXU_DIM):
            sl = slice(c * MXU_DIM, (c + 1) * MXU_DIM)
            _inproj_matmul(hn_ref, wa_ref, z_ref, 1, 0, c)
            v = z_ref[0, :, sl]
            vp_ref[:, sl] = v[0:tp]
            vs_ref[:, sl] = v[tp:tm]
            vt_ref[0, sl, :] = v[0:tp].T

    @pl.when(j == 2)
    def _():
        for c in range(width // MXU_DIM):
            sl = slice(c * MXU_DIM, (c + 1) * MXU_DIM)
            _inproj_matmul(hn_ref, wb_ref, z_ref, 0, 0, c)
            u = z_ref[1, :, sl] * jax.nn.sigmoid(z_ref[0, :, sl])
            up_ref[:, sl] = u[0:tp]
            us_ref[:, sl] = u[tp:tm]


def _inproj(xp, xs, gain, w, gq, gk, cos_t, sin_t, pm, seq_len, tiles=16):
    tp_all, d = xp.shape
    ts_all = xs.shape[0]
    tp, ts = tp_all // tiles, ts_all // tiles
    tm = tp + ts
    width = w.shape[1] // 5
    tiles_per_seq = seq_len // tp
    assert ts % SUBLANES == 0 and tm % 16 == 0 and seq_len % tp == 0 and cos_t.shape[0] == tiles * tm

    x_in = lambda rows: pl.BlockSpec(
        (rows, d), lambda i, j: (jnp.where(j == 0, i, jnp.minimum(i + 1, tiles - 1)), 0))
    const = lambda shape: pl.BlockSpec(shape, lambda i, j: (0,) * len(shape))
    w_cols = lambda first, groups: pl.BlockSpec(
        (d, groups * width), lambda i, j: (0, first // groups))

    def tile_at(i, j, ready):
        return jnp.where(j >= ready, i, jnp.maximum(i - 1, 0))

    def rows_out(rows, ready):
        return pl.BlockSpec((rows, width), lambda i, j: (tile_at(i, j, ready), 0))

    def cols_out(ready):
        def index_map(i, j):
            tile = tile_at(i, j, ready)
            return (tile // tiles_per_seq, 0, tile % tiles_per_seq)
        return pl.BlockSpec((1, width, tp), index_map)

    f32 = lambda *shape: jax.ShapeDtypeStruct(shape, F32)
    scratch = [pltpu.VMEM((tm, d), BF16), pltpu.VMEM((2, tm, width), F32)]
    params = pltpu.CompilerParams(
        dimension_semantics=("arbitrary", "arbitrary"), vmem_limit_bytes=VMEM_LIMIT)
    tab_spec = pl.BlockSpec((tm, LANES), lambda i, j: (i, 0))

    qp, qs, kp, ks, kt = pl.pallas_call(
        _inproj_qk_body,
        grid=(tiles, 3),
        in_specs=[x_in(tp), x_in(ts), const((1, d)), w_cols(0, 2),
                  const((1, width)), const((1, width)), tab_spec, tab_spec, const(pm.shape)],
        out_specs=[rows_out(tp, 2), rows_out(ts, 2), rows_out(tp, 1), rows_out(ts, 1), cols_out(1)],
        out_shape=[f32(tp_all, width), f32(ts_all, width), f32(tp_all, width), f32(ts_all, width),
                   f32(tp_all // seq_len, width, seq_len)],
        scratch_shapes=scratch, compiler_params=params, name="inproj_qk",
    )(xp, xs, gain, w, gq, gk, cos_t, sin_t, pm)

    vp, vs, vt, up, us = pl.pallas_call(
        _inproj_vu_body,
        grid=(tiles, 3),
        in_specs=[x_in(tp), x_in(ts), const((1, d)), w_cols(2, 1), w_cols(3, 1), w_cols(4, 1)],
        out_specs=[rows_out(tp, 1), rows_out(ts, 1), cols_out(1), rows_out(tp, 2), rows_out(ts, 2)],
        out_shape=[f32(tp_all, width), f32(ts_all, width), f32(tp_all // seq_len, width, seq_len),
                   f32(tp_all, width), f32(ts_all, width)],
        scratch_shapes=scratch, compiler_params=params, name="inproj_vu",
    )(xp, xs, gain, w, w, w)
    return (qp, kp, vp, up, kt, vt), (qs, ks, vs, us)


def _pattn_body(q_ref, k_ref, v_ref, o_ref, kt, on, ls, dils, span):
    s_len = q_ref.shape[0]
    nblk = s_len // Q_BLOCK

    def block_rows(n, d, bpc):
        if d == 1:
            return pl.ds(pl.multiple_of(n * Q_BLOCK, Q_BLOCK), Q_BLOCK)
        return pl.ds(n // bpc + (n % bpc) * (Q_BLOCK * d), Q_BLOCK, stride=d)

    for p, d in enumerate(dils):
        bpc = (s_len // d) // Q_BLOCK
        for n in range(nblk):
            kt[p, n] = k_ref[block_rows(n, d, bpc), :].T.astype(BF16)

    lane = lax.broadcasted_iota(jnp.int32, (Q_BLOCK, LANES), 1)
    head0 = lane < HEAD_DIM
    qi = lax.broadcasted_iota(jnp.int32, (Q_BLOCK, Q_BLOCK), 0)
    ki = lax.broadcasted_iota(jnp.int32, (Q_BLOCK, Q_BLOCK), 1)
    causal = ki <= qi
    band = (qi + Q_BLOCK - ki) <= span
    one = jnp.ones((), BF16)
    zero = jnp.zeros((), BF16)

    for p, d in enumerate(dils):
        bpc = (s_len // d) // Q_BLOCK
        use_prev = bpc > 1

        def block(n, carry, p=p, d=d, bpc=bpc, use_prev=use_prev):
            rows = block_rows(n, d, bpc)
            q = q_ref[rows, :].astype(BF16)
            vc = v_ref[rows, :].astype(BF16)
            ktc = kt[p, n]
            if use_prev:
                n_prev = jnp.maximum(n - 1, 0)
                ktp = kt[p, n_prev]
                vp = v_ref[block_rows(n_prev, d, bpc), :].astype(BF16)
                prev_ok = band & ((n % bpc) != 0)
            res = []
            for h in range(2):
                own = head0 if h == 0 else ~head0
                qh = jnp.where(own, q, zero)
                sc = jnp.where(causal, jnp.dot(qh, ktc, preferred_element_type=F32), NEG)
                if use_prev:
                    sp = jnp.where(prev_ok, jnp.dot(qh, ktp, preferred_element_type=F32), NEG)
                    m = jnp.max(jnp.maximum(sc, sp), axis=1, keepdims=True)
                else:
                    m = jnp.max(sc, axis=1, keepdims=True)
                o = jnp.dot(jnp.exp(sc - m).astype(BF16), jnp.where(own, vc, one),
                            preferred_element_type=F32)
                if use_prev:
                    o = o + jnp.dot(jnp.exp(sp - m).astype(BF16), jnp.where(own, vp, one),
                                    preferred_element_type=F32)
                res.append((m, o))
            (m0, o0), (m1, o1) = res
            l = pltpu.roll(jnp.where(head0, o1, o0), HEAD_DIM, 1)
            on[p, rows, :] = jnp.where(head0, o0, o1) / l
            ls[p, rows, :] = jnp.where(head0, m0, m1) + jnp.log(l)
            return carry

        lax.fori_loop(0, nblk, block, 0, unroll=8)

    def merge(c, carry):
        rows = pl.ds(pl.multiple_of(c * Q_BLOCK, Q_BLOCK), Q_BLOCK)
        lses = [ls[p, rows, :] for p in range(len(dils))]
        m = lses[0]
        for lp in lses[1:]:
            m = jnp.maximum(m, lp)
        num = jnp.zeros((Q_BLOCK, LANES), F32)
        den = jnp.zeros((Q_BLOCK, LANES), F32)
        for p, lp in enumerate(lses):
            e = jnp.exp(lp - m)
            num = num + on[p, rows, :] * e
            den = den + e
        o_ref[rows, :] = (num / den).astype(o_ref.dtype)
        return carry

    lax.fori_loop(0, nblk, merge, 0, unroll=2)


SAMPLE_KEY_CHUNK = 512


def _sattn_body(q_ref, kn_ref, vn_ref, kt_ref, vt_ref, mc_ref, mn_ref, o_ref, s_ref):
    t_len, width = q_ref.shape
    buf_len = kt_ref.shape[2]
    n_heads = width // HEAD_DIM
    rows = n_heads * t_len
    q = q_ref[...]
    q_rep = jnp.concatenate([q] * n_heads, axis=0)
    row_h = lax.broadcasted_iota(jnp.int32, (rows, width), 0) // t_len
    lane_h = lax.broadcasted_iota(jnp.int32, (rows, width), 1) // HEAD_DIM
    own = row_h == lane_h
    q_exp = jnp.where(own, q_rep, 0.0).astype(BF16)

    pad = jnp.zeros((mn_ref.shape[1] - t_len, width), F32)
    kn = jnp.concatenate([kn_ref[...], pad], axis=0).astype(BF16)
    vn = jnp.concatenate([vn_ref[...], pad], axis=0).astype(BF16)

    chunks = [slice(c, c + SAMPLE_KEY_CHUNK) for c in range(0, buf_len, SAMPLE_KEY_CHUNK)]
    m = None
    for sl in chunks:
        s = jnp.dot(q_exp, kt_ref[0, :, sl].astype(BF16), preferred_element_type=F32)
        s = jnp.where(mc_ref[:, sl] > 0, s, NEG)
        s_ref[:, sl] = s
        mx = jnp.max(s, axis=1, keepdims=True)
        m = mx if m is None else jnp.maximum(m, mx)
    sn = lax.dot_general(q_exp, kn, NT_DIMS, preferred_element_type=F32)
    mn = mn_ref[...]
    sn = jnp.where(mn > 0, sn, NEG)
    m = jnp.maximum(m, jnp.max(sn, axis=1, keepdims=True))
    pn = mn * jnp.exp(sn - m)
    l = jnp.sum(pn, axis=1, keepdims=True)
    o = jnp.dot(pn.astype(BF16), vn, preferred_element_type=F32)
    for sl in chunks:
        p = mc_ref[:, sl] * jnp.exp(s_ref[:, sl] - m)
        l = l + jnp.sum(p, axis=1, keepdims=True)
        o = o + lax.dot_general(p.astype(BF16), vt_ref[0, :, sl].astype(BF16), NT_DIMS,
                                preferred_element_type=F32)
    o = jnp.where(own, o / l, 0.0)
    out = o[0:t_len]
    for h in range(1, n_heads):
        out = out + o[h * t_len:(h + 1) * t_len]
    o_ref[...] = out


def _attn_body(q_ref, k_ref, v_ref, qs_ref, kn_ref, vn_ref, kt_ref, vt_ref, mc_ref, mn_ref,
               o_ref, os_ref, kt, on, ls, s_ref, *, dils, span):
    _pattn_body(q_ref, k_ref, v_ref, o_ref, kt, on, ls, dils, span)
    _sattn_body(qs_ref, kn_ref, vn_ref, kt_ref, vt_ref, mc_ref, mn_ref, os_ref, s_ref)


def _attention(q, k, v, batch, s_len, qs, kn, vn, cache_kt, cache_vt, t_len):
    t, width = q.shape
    dils = tuple(d for _, d in DILATED)
    spans = {w // d for w, d in DILATED}
    assert len(spans) == 1 and dils[0] == 1
    span = spans.pop()
    assert span == Q_BLOCK and all(s_len % (d * Q_BLOCK) == 0 for d in dils)
    npat = len(dils)
    nblk = s_len // Q_BLOCK
    pairs = width // LANES
    dec_batch, _, buf_len = cache_kt.shape
    assert dec_batch == batch * pairs and buf_len % SAMPLE_KEY_CHUNK == 0
    n_heads = width // HEAD_DIM
    mult = _sample_mult(t_len, buf_len)
    mc = np.tile(mult[:, :buf_len], (n_heads, 1))
    mn = np.zeros((n_heads * t_len, LANES), np.float32)
    mn[:, :t_len] = np.tile(mult[:, buf_len:], (n_heads, 1))

    spec = pl.BlockSpec((s_len, LANES), lambda b, hp: (b, hp))
    row_spec = pl.BlockSpec((t_len, width), lambda b, hp: (b * pairs + hp, 0))
    cache_spec = pl.BlockSpec((1, width, buf_len), lambda b, hp: (b * pairs + hp, 0, 0))
    const = lambda shape: pl.BlockSpec(shape, lambda b, hp: (0, 0))
    tok_f = pltpu.VMEM((npat, s_len, LANES), F32)
    return pl.pallas_call(
        functools.partial(_attn_body, dils=dils, span=span),
        grid=(batch, pairs),
        in_specs=[spec, spec, spec, row_spec, row_spec, row_spec, cache_spec, cache_spec,
                  const(mc.shape), const(mn.shape)],
        out_specs=[spec, row_spec],
        out_shape=[jax.ShapeDtypeStruct((t, width), BF16),
                   jax.ShapeDtypeStruct((dec_batch * t_len, width), F32)],
        scratch_shapes=[pltpu.VMEM((npat, nblk, LANES, Q_BLOCK), BF16), tok_f, tok_f,
                        pltpu.VMEM((n_heads * t_len, buf_len), F32)],
        compiler_params=pltpu.CompilerParams(
            dimension_semantics=("parallel", "parallel"), vmem_limit_bytes=VMEM_LIMIT),
        name="attention",
    )(q, k, v, qs, kn, vn, cache_kt, cache_vt, jnp.asarray(mc), jnp.asarray(mn))


def _sample_mult(t_len, buf_len):
    mult = np.zeros((t_len, buf_len + t_len), np.float32)
    for w, d in DILATED:
        for t in range(t_len):
            for j in range(w // d + 1):
                idx = buf_len + t - d * j
                if idx >= 0:
                    mult[t, idx] += 1.0
    return mult


def _ln_swish(y, g, b):
    mu = jnp.mean(y, axis=-1, keepdims=True)
    yc = y - mu
    var = jnp.mean(yc * yc, axis=-1, keepdims=True)
    return _silu(yc * lax.rsqrt(var + EPS) * g + b)


CONV_ROWS = 32
SUBLANES = 8


def _conv_out_prompt_body(ucur_ref, uprev_ref, attn_ref, h_ref, w_ref, b_ref, g_ref, beta_ref,
                          wo_ref, o_ref, ext_ref, sh_ref, wb_ref, conv_ref, *, tiles_per_seq):
    tm, ch = ucur_ref.shape
    halo = uprev_ref.shape[0]
    taps = w_ref.shape[0]
    first = (pl.program_id(0) % tiles_per_seq) == 0
    ext_ref[0:halo, :] = jnp.where(first, 0.0, uprev_ref[...])
    ext_ref[halo:halo + tm, :] = ucur_ref[...]
    sh_rows = sh_ref.shape[1]
    for s in range(1, SUBLANES):
        sh_ref[s - 1] = ext_ref[pl.ds(s, sh_rows), :]
    @pl.when(pl.program_id(0) == 0)
    def _():
        for w in range(taps):
            wb_ref[w] = jnp.broadcast_to(w_ref[w:w + 1, :], (SUBLANES, ch))
    off = halo - (taps - 1)
    groups = CONV_ROWS // SUBLANES

    for c in range(tm // CONV_ROWS):
        accs = [jnp.zeros((SUBLANES, ch), F32) for _ in range(groups)]
        for w in range(taps):
            s = (off + w) % SUBLANES
            wb = wb_ref[w]
            for g in range(groups):
                base = c * CONV_ROWS + g * SUBLANES + off + w - s
                x = (ext_ref[pl.ds(base, SUBLANES), :] if s == 0
                     else sh_ref[s - 1, pl.ds(base, SUBLANES), :])
                accs[g] = accs[g] + x * wb
        acc = jnp.concatenate(accs, axis=0)
        act = _ln_swish(acc + b_ref[...], g_ref[...], beta_ref[...])
        conv_ref[c * CONV_ROWS:(c + 1) * CONV_ROWS, :] = act.astype(BF16)
    aw = attn_ref.shape[1]
    o_ref[...] = (h_ref[...]
                  + jnp.dot(attn_ref[...], wo_ref[0:aw, :], preferred_element_type=F32)
                  + jnp.dot(conv_ref[...], wo_ref[aw:aw + ch, :], preferred_element_type=F32))


def _conv_out_prompt(u, attn, h, dw_w, dw_b, ln_g, ln_b, wo, s_len, tm=256, halo=32):
    t, ch = u.shape
    d = h.shape[1]
    aw = attn.shape[1]
    taps = dw_w.shape[0]
    assert taps - 1 <= halo and s_len % tm == 0 and tm % halo == 0 and halo % SUBLANES == 0
    hb = tm // halo
    vec = pl.BlockSpec((1, ch), lambda i: (0, 0))
    return pl.pallas_call(
        functools.partial(_conv_out_prompt_body, tiles_per_seq=s_len // tm),
        grid=(t // tm,),
        in_specs=[
            pl.BlockSpec((tm, ch), lambda i: (i, 0)),
            pl.BlockSpec((halo, ch), lambda i: (jnp.maximum(i * hb - 1, 0), 0)),
            pl.BlockSpec((tm, aw), lambda i: (i, 0)),
            pl.BlockSpec((tm, d), lambda i: (i, 0)),
            pl.BlockSpec((taps, ch), lambda i: (0, 0)),
            vec, vec, vec,
            pl.BlockSpec((aw + ch, d), lambda i: (0, 0)),
        ],
        out_specs=pl.BlockSpec((tm, d), lambda i: (i, 0)),
        out_shape=jax.ShapeDtypeStruct((t, d), F32),
        scratch_shapes=[pltpu.VMEM((halo + tm, ch), F32),
                        pltpu.VMEM((SUBLANES - 1, halo + tm - SUBLANES, ch), F32),
                        pltpu.VMEM((taps, SUBLANES, ch), F32),
                        pltpu.VMEM((tm, ch), BF16)],
        compiler_params=pltpu.CompilerParams(
            dimension_semantics=("arbitrary",), vmem_limit_bytes=VMEM_LIMIT),
        name="conv_out_prompt",
    )(u, u, attn, h, dw_w, dw_b, ln_g, ln_b, wo)


def _conv_out_sample_body(u_ref, st_ref, attn_ref, h_ref, w_ref, b_ref, g_ref, beta_ref, wo_ref,
                          o_ref, ns_ref, ext_ref, conv_ref, at_ref):
    n_state, batch, ch = st_ref.shape
    t_len = u_ref.shape[0]
    taps = w_ref.shape[0]
    ext_ref[0:n_state] = st_ref[...]
    ext_ref[n_state:n_state + t_len] = u_ref[...]
    ns_ref[...] = ext_ref[t_len:t_len + n_state]
    for t in range(t_len):
        acc = jnp.zeros((batch, ch), F32)
        for w in range(taps):
            acc = acc + ext_ref[t + w] * w_ref[w:w + 1, :]
        act = _ln_swish(acc + b_ref[...], g_ref[...], beta_ref[...])
        conv_ref[t * batch:(t + 1) * batch, :] = act.astype(BF16)
        at_ref[t * batch:(t + 1) * batch, :] = attn_ref[t].astype(BF16)
    aw = attn_ref.shape[2]
    proj = (jnp.dot(at_ref[...], wo_ref[0:aw, :], preferred_element_type=F32)
            + jnp.dot(conv_ref[...], wo_ref[aw:aw + ch, :], preferred_element_type=F32))
    for t in range(t_len):
        o_ref[t] = h_ref[t] + proj[t * batch:(t + 1) * batch, :]


def _conv_out_sample(u_t, state_t, attn_t, h_t, dw_w, dw_b, ln_g, ln_b, wo):
    t_len, batch, ch = u_t.shape
    d = h_t.shape[2]
    n_state = state_t.shape[0]
    assert dw_w.shape[0] == n_state + 1
    return pl.pallas_call(
        _conv_out_sample_body,
        out_shape=[jax.ShapeDtypeStruct((t_len, batch, d), F32),
                   jax.ShapeDtypeStruct((n_state, batch, ch), F32)],
        scratch_shapes=[pltpu.VMEM((n_state + t_len, batch, ch), F32),
                        pltpu.VMEM((t_len * batch, ch), BF16),
                        pltpu.VMEM((t_len * batch, attn_t.shape[2]), BF16)],
        compiler_params=pltpu.CompilerParams(vmem_limit_bytes=VMEM_LIMIT),
        name="conv_out_sample",
    )(u_t, state_t, attn_t, h_t, dw_w, dw_b, ln_g, ln_b, wo)


def _rope_tables(positions):
    half = HEAD_DIM // 2
    inv = ROPE_THETA ** (-np.arange(half, dtype=np.float64) / half)
    ang = np.asarray(positions, np.float64)[:, None] * inv[None, :]
    cos, sin = np.cos(ang), np.sin(ang)
    reps = LANES // HEAD_DIM
    cos_t = np.tile(np.concatenate([cos, cos], axis=1), (1, reps))
    sin_t = np.tile(np.concatenate([-sin, sin], axis=1), (1, reps))
    return jnp.asarray(cos_t, F32), jnp.asarray(sin_t, F32)


def kernel(x_prompt, x_sample, cache_k, cache_v, state_conv, ln_ffn1, ffn1_w_gate, ffn1_w_up,
           ffn1_w_down, ln_mix, w_in, q_norm, k_norm, conv_dw_w, conv_dw_b, conv_ln_g, conv_ln_b,
           w_out, ln_ffn2, ffn2_w_gate, ffn2_w_up, ffn2_w_down):
    batch, s_len, d_model = x_prompt.shape
    dec_batch, t_len, _ = x_sample.shape
    depth, _, buf_len, n_heads, head_dim = cache_k.shape
    assert depth == 1 and head_dim == HEAD_DIM
    width = n_heads * head_dim
    ch = conv_dw_w.shape[2]

    xp = x_prompt.reshape(batch * s_len, d_model)
    xs = x_sample.reshape(dec_batch * t_len, d_model)
    ts = dec_batch * t_len

    reps = width // HEAD_DIM
    gq = jnp.tile(q_norm, (1, reps))
    gk = jnp.tile(k_norm, (1, reps))
    heads_per_tile = MXU_DIM // HEAD_DIM
    pm = jnp.asarray(np.kron(np.eye(heads_per_tile), np.ones((HEAD_DIM, HEAD_DIM))), BF16)

    tiles = INPROJ_TILES
    pos_p = (np.arange(batch * s_len) % s_len).reshape(tiles, -1)
    pos_s = np.tile(PAST_LEN + np.arange(t_len), dec_batch).reshape(tiles, -1)
    cos_t, sin_t = _rope_tables(np.concatenate([pos_p, pos_s], axis=1).reshape(-1))

    hp, hs, (wi, wo) = _ffn(xp, xs, ln_ffn1, ffn1_w_gate[0], ffn1_w_up[0], ffn1_w_down[0],
                            casts=(w_in[0], w_out[0]))

    (qp, kp, vp, up, kt_p, vt_p), (qs, ks, vs, us) = _inproj(
        hp, hs, ln_mix, wi, gq, gk, cos_t, sin_t, pm, seq_len=s_len, tiles=tiles)

    kt = jnp.transpose(cache_k[0], (0, 2, 3, 1)).reshape(dec_batch, width, buf_len)
    vt = jnp.transpose(cache_v[0], (0, 2, 3, 1)).reshape(dec_batch, width, buf_len)
    attn_p, attn_s = _attention(qp, kp, vp, batch, s_len, qs, ks, vs, kt, vt, t_len)

    hp = _conv_out_prompt(up, attn_p, hp, conv_dw_w[0], conv_dw_b, conv_ln_g, conv_ln_b, wo, s_len)
    state_t = jnp.transpose(state_conv[0], (1, 0, 2))
    time_major = lambda a: jnp.transpose(a.reshape(dec_batch, t_len, a.shape[-1]), (1, 0, 2))
    hs_t, new_state_t = _conv_out_sample(time_major(us), state_t, time_major(attn_s),
                                         time_major(hs), conv_dw_w[0], conv_dw_b,
                                         conv_ln_g, conv_ln_b, wo)
    hs = jnp.transpose(hs_t, (1, 0, 2)).reshape(ts, d_model)

    yp, ys, _ = _ffn(hp, hs, ln_ffn2, ffn2_w_gate[0], ffn2_w_up[0], ffn2_w_down[0])

    n_state = state_conv.shape[2]
    seq_major = lambda a: jnp.transpose(a.reshape(batch, n_heads, head_dim, s_len), (0, 3, 1, 2))[None]
    kv_s = (1, dec_batch, t_len, n_heads, head_dim)
    return (yp.reshape(batch, s_len, d_model),
            ys.reshape(dec_batch, t_len, d_model),
            seq_major(kt_p), seq_major(vt_p),
            up.reshape(batch, s_len, ch)[:, s_len - n_state:][None],
            ks.reshape(kv_s), vs.reshape(kv_s),
            jnp.transpose(new_state_t, (1, 0, 2))[None])
```

```python
import functools

import numpy as np
import jax
import jax.numpy as jnp
from jax import lax
from jax.experimental import pallas as pl
from jax.experimental.pallas import tpu as pltpu

F32 = jnp.float32
BF16 = jnp.bfloat16

HEAD_DIM = 64
PAST_LEN = 16384
DILATED = ((128, 1), (512, 4), (2048, 16))
Q_BLOCK = 128
ROPE_THETA = 10000.0
EPS = 1e-6
FFN_RES = 0.5
NEG = -1e30

LANES = 128
MXU_DIM = 256
BF16_ROWS = 16
VMEM_LIMIT = 56 * 1024 * 1024
FFN_VMEM_LIMIT = 60 * 1024 * 1024
INPROJ_TILES = 16

NT_DIMS = (((1,), (1,)), ((), ()))


def _silu(x):
    return x * jax.nn.sigmoid(x)


def _rms_rows(x, gain):
    ms = jnp.mean(x * x, axis=-1, keepdims=True)
    return x * lax.rsqrt(ms + EPS) * gain


FFN_ROW_CHUNK = 512


def _ffn_body(xp_ref, xs_ref, g_ref, wg_ref, wu_ref, wd_ref, *rest, n_casts):
    cast_in, rest = rest[:n_casts], rest[n_casts:]
    op_ref, os_ref = rest[:2]
    cast_out, (h_ref, wgb_ref, wub_ref, wdb_ref) = rest[2:2 + n_casts], rest[2 + n_casts:]
    tp = xp_ref.shape[0]
    ts = xs_ref.shape[0]

    for src, dst in zip(cast_in, cast_out):
        dst[...] = src[...].astype(BF16)

    @pl.when(pl.program_id(1) == 0)
    def _():
        for r0 in range(0, tp, FFN_ROW_CHUNK):
            x = xp_ref[r0:r0 + FFN_ROW_CHUNK, :]
            h_ref[r0:r0 + FFN_ROW_CHUNK, :] = _rms_rows(x, g_ref[...]).astype(BF16)
            op_ref[r0:r0 + FFN_ROW_CHUNK, :] = x
        x = xs_ref[...]
        h_ref[tp:tp + ts, :] = _rms_rows(x, g_ref[...]).astype(BF16)
        os_ref[...] = x

    wgb_ref[...] = wg_ref[...].astype(BF16)
    wub_ref[...] = wu_ref[...].astype(BF16)
    wdb_ref[...] = wd_ref[...].astype(BF16)

    def half_step(h):
        g = jnp.dot(h, wgb_ref[...], preferred_element_type=F32)
        u = jnp.dot(h, wub_ref[...], preferred_element_type=F32)
        a = (_silu(g) * u * FFN_RES).astype(BF16)
        return jnp.dot(a, wdb_ref[...], preferred_element_type=F32)

    chunk = (tp + ts) // 2
    op_ref[0:chunk, :] += half_step(h_ref[0:chunk, :])
    res = half_step(h_ref[chunk:tp + ts, :])
    op_ref[chunk:tp, :] += res[0:tp - chunk, :]
    os_ref[...] += res[tp - chunk:chunk, :]


def _ffn(xp, xs, gain, wg, wu, wd, casts=(), tiles=8, tf=256):
    tp_all, d = xp.shape
    ts_all = xs.shape[0]
    dff = wg.shape[1]
    tp, ts = tp_all // tiles, ts_all // tiles
    assert tp % FFN_ROW_CHUNK == 0 and ts % 16 == 0 and dff % tf == 0 and (tp + ts) % 32 == 0
    row = lambda rows: pl.BlockSpec((rows, d), lambda i, f: (i, 0))
    row_in = lambda rows: pl.BlockSpec(
        (rows, d), lambda i, f: (jnp.where(f == 0, i, jnp.minimum(i + 1, tiles - 1)), 0))
    steps = dff // tf
    cast_specs, cast_shapes = [], []
    for m in casts:
        slabs = m.shape[0] // BF16_ROWS
        assert m.shape[0] % BF16_ROWS == 0 and slabs <= tiles * steps
        cast_specs.append(pl.BlockSpec(
            (BF16_ROWS, m.shape[1]),
            lambda i, f, slabs=slabs: (jnp.minimum(i * steps + f, slabs - 1), 0)))
        cast_shapes.append(jax.ShapeDtypeStruct(m.shape, BF16))
    outs = pl.pallas_call(
        functools.partial(_ffn_body, n_casts=len(casts)),
        grid=(tiles, steps),
        in_specs=[
            row_in(tp), row_in(ts),
            pl.BlockSpec((1, d), lambda i, f: (0, 0)),
            pl.BlockSpec((d, tf), lambda i, f: (0, f)),
            pl.BlockSpec((d, tf), lambda i, f: (0, f)),
            pl.BlockSpec((tf, d), lambda i, f: (f, 0)),
        ] + cast_specs,
        out_specs=[row(tp), row(ts)] + cast_specs,
        out_shape=[jax.ShapeDtypeStruct((tp_all, d), F32), jax.ShapeDtypeStruct((ts_all, d), F32)]
        + cast_shapes,
        scratch_shapes=[pltpu.VMEM((tp + ts, d), BF16), pltpu.VMEM((d, tf), BF16),
                        pltpu.VMEM((d, tf), BF16), pltpu.VMEM((tf, d), BF16)],
        compiler_params=pltpu.CompilerParams(
            dimension_semantics=("arbitrary", "arbitrary"), vmem_limit_bytes=FFN_VMEM_LIMIT),
        name="ffn",
    )(xp, xs, gain, wg, wu, wd, *casts)
    return outs[0], outs[1], tuple(outs[2:])


def _inproj_load(xp_ref, xs_ref, g_ref, hn_ref):
    tp = xp_ref.shape[0]
    hn_ref[0:tp, :] = _rms_rows(xp_ref[...], g_ref[...]).astype(BF16)
    hn_ref[tp:, :] = _rms_rows(xs_ref[...], g_ref[...]).astype(BF16)


def _inproj_matmul(hn_ref, w_ref, z_ref, slot, group, c=None):
    width = z_ref.shape[2]
    if c is None:
        lo, n = group * width, width
        z_ref[slot] = jnp.dot(hn_ref[...], w_ref[:, lo:lo + n], preferred_element_type=F32)
    else:
        lo = group * width + c * MXU_DIM
        z_ref[slot, :, c * MXU_DIM:(c + 1) * MXU_DIM] = jnp.dot(
            hn_ref[...], w_ref[:, lo:lo + MXU_DIM], preferred_element_type=F32)


def _inproj_qk_body(xp_ref, xs_ref, g_ref, w_ref, gq_ref, gk_ref, cos_ref, sin_ref, pm_ref,
                    qp_ref, qs_ref, kp_ref, ks_ref, kt_ref, hn_ref, z_ref):
    tp = xp_ref.shape[0]
    tm, width = z_ref.shape[1], z_ref.shape[2]
    lane = lax.broadcasted_iota(jnp.int32, (tm, LANES), 1)
    first_half = (lane % HEAD_DIM) < (HEAD_DIM // 2)

    def norm_rope(slot, gain_ref, p_ref, s_ref, t_ref, scale, next_group):
        pm = pm_ref[...]
        for c in range(width // MXU_DIM):
            sl = slice(c * MXU_DIM, (c + 1) * MXU_DIM)
            if next_group is not None:
                _inproj_matmul(hn_ref, w_ref, z_ref, 1 - slot, next_group, c)
            z = z_ref[slot, :, sl]
            zz = z * z
            hi = zz.astype(BF16)
            lo = (zz - hi.astype(F32)).astype(BF16)
            ss = (jnp.dot(hi, pm, preferred_element_type=F32)
                  + jnp.dot(lo, pm, preferred_element_type=F32))
            zn = z * lax.rsqrt(ss * (1.0 / HEAD_DIM) + EPS) * gain_ref[:, sl]
            for e in range(MXU_DIM // LANES):
                x = zn[:, e * LANES:(e + 1) * LANES]
                partner = jnp.where(first_half,
                                    pltpu.roll(x, LANES - HEAD_DIM // 2, 1),
                                    pltpu.roll(x, HEAD_DIM // 2, 1))
                y = x * cos_ref[...] + partner * sin_ref[...]
                if scale != 1.0:
                    y = y * scale
                lanes = slice(c * MXU_DIM + e * LANES, c * MXU_DIM + (e + 1) * LANES)
                p_ref[:, lanes] = y[0:tp]
                s_ref[:, lanes] = y[tp:tm]
                if t_ref is not None:
                    t_ref[0, lanes, :] = y[0:tp].T

    _inproj_load(xp_ref, xs_ref, g_ref, hn_ref)
    _inproj_matmul(hn_ref, w_ref, z_ref, 0, 1)
    norm_rope(0, gk_ref, kp_ref, ks_ref, kt_ref, 1.0, 0)
    norm_rope(1, gq_ref, qp_ref, qs_ref, None, HEAD_DIM ** -0.5, None)


def _inproj_vu_body(xp_ref, xs_ref, g_ref, wv_ref, wa_ref, wb_ref, vp_ref, vs_ref, vt_ref,
                    up_ref, us_ref, hn_ref, z_ref):
    tp = xp_ref.shape[0]
    tm, width = z_ref.shape[1], z_ref.shape[2]

    _inproj_load(xp_ref, xs_ref, g_ref, hn_ref)
    _inproj_matmul(hn_ref, wv_ref, z_ref, 0, 0)
    for c in range(width // MXU_DIM):
        sl = slice(c * MXU_DIM, (c + 1) * MXU_DIM)
        _inproj_matmul(hn_ref, wa_ref, z_ref, 1, 0, c)
        v = z_ref[0, :, sl]
        vp_ref[:, sl] = v[0:tp]
        vs_ref[:, sl] = v[tp:tm]
        vt_ref[0, sl, :] = v[0:tp].T
    for c in range(width // MXU_DIM):
        sl = slice(c * MXU_DIM, (c + 1) * MXU_DIM)
        _inproj_matmul(hn_ref, wb_ref, z_ref, 0, 0, c)
        u = z_ref[1, :, sl] * jax.nn.sigmoid(z_ref[0, :, sl])
        up_ref[:, sl] = u[0:tp]
        us_ref[:, sl] = u[tp:tm]


def _inproj(xp, xs, gain, w, gq, gk, cos_t, sin_t, pm, seq_len, tiles=16):
    tp_all, d = xp.shape
    ts_all = xs.shape[0]
    tp, ts = tp_all // tiles, ts_all // tiles
    tm = tp + ts
    width = w.shape[1] // 5
    tiles_per_seq = seq_len // tp
    assert ts % SUBLANES == 0 and tm % 16 == 0 and seq_len % tp == 0 and cos_t.shape[0] == tiles * tm

    x_in = lambda rows: pl.BlockSpec((rows, d), lambda i: (i, 0))
    const = lambda shape: pl.BlockSpec(shape, lambda i: (0,) * len(shape))
    w_cols = lambda first, groups: pl.BlockSpec((d, groups * width), lambda i: (0, first // groups))
    rows_out = lambda rows: pl.BlockSpec((rows, width), lambda i: (i, 0))
    cols_out = pl.BlockSpec((1, width, tp), lambda i: (i // tiles_per_seq, 0, i % tiles_per_seq))

    f32 = lambda *shape: jax.ShapeDtypeStruct(shape, F32)
    scratch = [pltpu.VMEM((tm, d), BF16), pltpu.VMEM((2, tm, width), F32)]
    params = pltpu.CompilerParams(
        dimension_semantics=("parallel",), vmem_limit_bytes=VMEM_LIMIT)
    tab_spec = pl.BlockSpec((tm, LANES), lambda i: (i, 0))

    qp, qs, kp, ks, kt = pl.pallas_call(
        _inproj_qk_body,
        grid=(tiles,),
        in_specs=[x_in(tp), x_in(ts), const((1, d)), w_cols(0, 2),
                  const((1, width)), const((1, width)), tab_spec, tab_spec, const(pm.shape)],
        out_specs=[rows_out(tp), rows_out(ts), rows_out(tp), rows_out(ts), cols_out],
        out_shape=[f32(tp_all, width), f32(ts_all, width), f32(tp_all, width), f32(ts_all, width),
                   f32(tp_all // seq_len, width, seq_len)],
        scratch_shapes=scratch, compiler_params=params, name="inproj_qk",
    )(xp, xs, gain, w, gq, gk, cos_t, sin_t, pm)

    vp, vs, vt, up, us = pl.pallas_call(
        _inproj_vu_body,
        grid=(tiles,),
        in_specs=[x_in(tp), x_in(ts), const((1, d)), w_cols(2, 1), w_cols(3, 1), w_cols(4, 1)],
        out_specs=[rows_out(tp), rows_out(ts), cols_out, rows_out(tp), rows_out(ts)],
        out_shape=[f32(tp_all, width), f32(ts_all, width), f32(tp_all // seq_len, width, seq_len),
                   f32(tp_all, width), f32(ts_all, width)],
        scratch_shapes=scratch, compiler_params=params, name="inproj_vu",
    )(xp, xs, gain, w, w, w)
    return (qp, kp, vp, up, kt, vt), (qs, ks, vs, us)


def _pattn_body(q_ref, k_ref, v_ref, o_ref, kt, on, ls, dils, span):
    s_len = q_ref.shape[0]
    nblk = s_len // Q_BLOCK

    def block_rows(n, d, bpc):
        if d == 1:
            return pl.ds(pl.multiple_of(n * Q_BLOCK, Q_BLOCK), Q_BLOCK)
        return pl.ds(n // bpc + (n % bpc) * (Q_BLOCK * d), Q_BLOCK, stride=d)

    for p, d in enumerate(dils):
        bpc = (s_len // d) // Q_BLOCK
        for n in range(nblk):
            kt[p, n] = k_ref[block_rows(n, d, bpc), :].T.astype(BF16)

    lane = lax.broadcasted_iota(jnp.int32, (Q_BLOCK, LANES), 1)
    head0 = lane < HEAD_DIM
    qi = lax.broadcasted_iota(jnp.int32, (Q_BLOCK, Q_BLOCK), 0)
    ki = lax.broadcasted_iota(jnp.int32, (Q_BLOCK, Q_BLOCK), 1)
    causal = ki <= qi
    band = (qi + Q_BLOCK - ki) <= span
    one = jnp.ones((), BF16)
    zero = jnp.zeros((), BF16)

    for p, d in enumerate(dils):
        bpc = (s_len // d) // Q_BLOCK
        use_prev = bpc > 1

        def block(n, carry, p=p, d=d, bpc=bpc, use_prev=use_prev):
            rows = block_rows(n, d, bpc)
            q = q_ref[rows, :].astype(BF16)
            vc = v_ref[rows, :].astype(BF16)
            ktc = kt[p, n]
            if use_prev:
                n_prev = jnp.maximum(n - 1, 0)
                ktp = kt[p, n_prev]
                vp = v_ref[block_rows(n_prev, d, bpc), :].astype(BF16)
                prev_ok = band & ((n % bpc) != 0)
            res = []
            for h in range(2):
                own = head0 if h == 0 else ~head0
                qh = jnp.where(own, q, zero)
                sc = jnp.where(causal, jnp.dot(qh, ktc, preferred_element_type=F32), NEG)
                if use_prev:
                    sp = jnp.where(prev_ok, jnp.dot(qh, ktp, preferred_element_type=F32), NEG)
                    m = jnp.max(jnp.maximum(sc, sp), axis=1, keepdims=True)
                else:
                    m = jnp.max(sc, axis=1, keepdims=True)
                o = jnp.dot(jnp.exp(sc - m).astype(BF16), jnp.where(own, vc, one),
                            preferred_element_type=F32)
                if use_prev:
                    o = o + jnp.dot(jnp.exp(sp - m).astype(BF16), jnp.where(own, vp, one),
                                    preferred_element_type=F32)
                res.append((m, o))
            (m0, o0), (m1, o1) = res
            l = pltpu.roll(jnp.where(head0, o1, o0), HEAD_DIM, 1)
            on[p, rows, :] = jnp.where(head0, o0, o1) / l
            ls[p, rows, :] = jnp.where(head0, m0, m1) + jnp.log(l)
            return carry

        lax.fori_loop(0, nblk, block, 0, unroll=8)

    def merge(c, carry):
        rows = pl.ds(pl.multiple_of(c * Q_BLOCK, Q_BLOCK), Q_BLOCK)
        lses = [ls[p, rows, :] for p in range(len(dils))]
        m = lses[0]
        for lp in lses[1:]:
            m = jnp.maximum(m, lp)
        num = jnp.zeros((Q_BLOCK, LANES), F32)
        den = jnp.zeros((Q_BLOCK, LANES), F32)
        for p, lp in enumerate(lses):
            e = jnp.exp(lp - m)
            num = num + on[p, rows, :] * e
            den = den + e
        o_ref[rows, :] = (num / den).astype(o_ref.dtype)
        return carry

    lax.fori_loop(0, nblk, merge, 0, unroll=2)


SAMPLE_KEY_CHUNK = 512


def _sattn_body(q_ref, kn_ref, vn_ref, kt_ref, vt_ref, mc_ref, mn_ref, o_ref, s_ref):
    t_len, width = q_ref.shape
    buf_len = kt_ref.shape[2]
    n_heads = width // HEAD_DIM
    rows = n_heads * t_len
    q = q_ref[...]
    q_rep = jnp.concatenate([q] * n_heads, axis=0)
    row_h = lax.broadcasted_iota(jnp.int32, (rows, width), 0) // t_len
    lane_h = lax.broadcasted_iota(jnp.int32, (rows, width), 1) // HEAD_DIM
    own = row_h == lane_h
    q_exp = jnp.where(own, q_rep, 0.0).astype(BF16)

    pad = jnp.zeros((mn_ref.shape[1] - t_len, width), F32)
    kn = jnp.concatenate([kn_ref[...], pad], axis=0).astype(BF16)
    vn = jnp.concatenate([vn_ref[...], pad], axis=0).astype(BF16)

    chunks = [slice(c, c + SAMPLE_KEY_CHUNK) for c in range(0, buf_len, SAMPLE_KEY_CHUNK)]
    m = None
    for sl in chunks:
        s = jnp.dot(q_exp, kt_ref[0, :, sl].astype(BF16), preferred_element_type=F32)
        s = jnp.where(mc_ref[:, sl] > 0, s, NEG)
        s_ref[:, sl] = s
        mx = jnp.max(s, axis=1, keepdims=True)
        m = mx if m is None else jnp.maximum(m, mx)
    sn = lax.dot_general(q_exp, kn, NT_DIMS, preferred_element_type=F32)
    mn = mn_ref[...]
    sn = jnp.where(mn > 0, sn, NEG)
    m = jnp.maximum(m, jnp.max(sn, axis=1, keepdims=True))
    pn = mn * jnp.exp(sn - m)
    l = jnp.sum(pn, axis=1, keepdims=True)
    o = jnp.dot(pn.astype(BF16), vn, preferred_element_type=F32)
    for sl in chunks:
        p = mc_ref[:, sl] * jnp.exp(s_ref[:, sl] - m)
        l = l + jnp.sum(p, axis=1, keepdims=True)
        o = o + lax.dot_general(p.astype(BF16), vt_ref[0, :, sl].astype(BF16), NT_DIMS,
                                preferred_element_type=F32)
    o = jnp.where(own, o / l, 0.0)
    out = o[0:t_len]
    for h in range(1, n_heads):
        out = out + o[h * t_len:(h + 1) * t_len]
    o_ref[...] = out


def _attn_body(q_ref, k_ref, v_ref, qs_ref, kn_ref, vn_ref, kt_ref, vt_ref, mc_ref, mn_ref,
               o_ref, os_ref, kt, on, ls, s_ref, *, dils, span):
    _pattn_body(q_ref, k_ref, v_ref, o_ref, kt, on, ls, dils, span)
    _sattn_body(qs_ref, kn_ref, vn_ref, kt_ref, vt_ref, mc_ref, mn_ref, os_ref, s_ref)


def _attention(q, k, v, batch, s_len, qs, kn, vn, cache_kt, cache_vt, t_len):
    t, width = q.shape
    dils = tuple(d for _, d in DILATED)
    spans = {w // d for w, d in DILATED}
    assert len(spans) == 1 and dils[0] == 1
    span = spans.pop()
    assert span == Q_BLOCK and all(s_len % (d * Q_BLOCK) == 0 for d in dils)
    npat = len(dils)
    nblk = s_len // Q_BLOCK
    pairs = width // LANES
    dec_batch, _, buf_len = cache_kt.shape
    assert dec_batch == batch * pairs and buf_len % SAMPLE_KEY_CHUNK == 0
    n_heads = width // HEAD_DIM
    mult = _sample_mult(t_len, buf_len)
    mc = np.tile(mult[:, :buf_len], (n_heads, 1))
    mn = np.zeros((n_heads * t_len, LANES), np.float32)
    mn[:, :t_len] = np.tile(mult[:, buf_len:], (n_heads, 1))

    spec = pl.BlockSpec((s_len, LANES), lambda b, hp: (b, hp))
    row_spec = pl.BlockSpec((t_len, width), lambda b, hp: (b * pairs + hp, 0))
    cache_spec = pl.BlockSpec((1, width, buf_len), lambda b, hp: (b * pairs + hp, 0, 0))
    const = lambda shape: pl.BlockSpec(shape, lambda b, hp: (0, 0))
    tok_f = pltpu.VMEM((npat, s_len, LANES), F32)
    return pl.pallas_call(
        functools.partial(_attn_body, dils=dils, span=span),
        grid=(batch, pairs),
        in_specs=[spec, spec, spec, row_spec, row_spec, row_spec, cache_spec, cache_spec,
                  const(mc.shape), const(mn.shape)],
        out_specs=[spec, row_spec],
        out_shape=[jax.ShapeDtypeStruct((t, width), BF16),
                   jax.ShapeDtypeStruct((dec_batch * t_len, width), F32)],
        scratch_shapes=[pltpu.VMEM((npat, nblk, LANES, Q_BLOCK), BF16), tok_f, tok_f,
                        pltpu.VMEM((n_heads * t_len, buf_len), F32)],
        compiler_params=pltpu.CompilerParams(
            dimension_semantics=("parallel", "parallel"), vmem_limit_bytes=VMEM_LIMIT),
        name="attention",
    )(q, k, v, qs, kn, vn, cache_kt, cache_vt, jnp.asarray(mc), jnp.asarray(mn))


def _sample_mult(t_len, buf_len):
    mult = np.zeros((t_len, buf_len + t_len), np.float32)
    for w, d in DILATED:
        for t in range(t_len):
            for j in range(w // d + 1):
                idx = buf_len + t - d * j
                if idx >= 0:
                    mult[t, idx] += 1.0
    return mult


def _ln_swish(y, g, b):
    mu = jnp.mean(y, axis=-1, keepdims=True)
    yc = y - mu
    var = jnp.mean(yc * yc, axis=-1, keepdims=True)
    return _silu(yc * lax.rsqrt(var + EPS) * g + b)


CONV_ROWS = 32
SUBLANES = 8


def _conv_out_prompt_body(ucur_ref, uprev_ref, attn_ref, h_ref, w_ref, b_ref, g_ref, beta_ref,
                          wo_ref, o_ref, ext_ref, sh_ref, wb_ref, conv_ref, *, tiles_per_seq):
    tm, ch = ucur_ref.shape
    halo = uprev_ref.shape[0]
    taps = w_ref.shape[0]
    first = (pl.program_id(0) % tiles_per_seq) == 0
    ext_ref[0:halo, :] = jnp.where(first, 0.0, uprev_ref[...])
    ext_ref[halo:halo + tm, :] = ucur_ref[...]
    sh_rows = sh_ref.shape[1]
    for s in range(1, SUBLANES):
        sh_ref[s - 1] = ext_ref[pl.ds(s, sh_rows), :]
    @pl.when(pl.program_id(0) == 0)
    def _():
        for w in range(taps):
            wb_ref[w] = jnp.broadcast_to(w_ref[w:w + 1, :], (SUBLANES, ch))
    off = halo - (taps - 1)
    groups = CONV_ROWS // SUBLANES

    for c in range(tm // CONV_ROWS):
        accs = [jnp.zeros((SUBLANES, ch), F32) for _ in range(groups)]
        for w in range(taps):
            s = (off + w) % SUBLANES
            wb = wb_ref[w]
            for g in range(groups):
                base = c * CONV_ROWS + g * SUBLANES + off + w - s
                x = (ext_ref[pl.ds(base, SUBLANES), :] if s == 0
                     else sh_ref[s - 1, pl.ds(base, SUBLANES), :])
                accs[g] = accs[g] + x * wb
        acc = jnp.concatenate(accs, axis=0)
        act = _ln_swish(acc + b_ref[...], g_ref[...], beta_ref[...])
        conv_ref[c * CONV_ROWS:(c + 1) * CONV_ROWS, :] = act.astype(BF16)
    aw = attn_ref.shape[1]
    o_ref[...] = (h_ref[...]
                  + jnp.dot(attn_ref[...], wo_ref[0:aw, :], preferred_element_type=F32)
                  + jnp.dot(conv_ref[...], wo_ref[aw:aw + ch, :], preferred_element_type=F32))


def _conv_out_prompt(u, attn, h, dw_w, dw_b, ln_g, ln_b, wo, s_len, tm=256, halo=32):
    t, ch = u.shape
    d = h.shape[1]
    aw = attn.shape[1]
    taps = dw_w.shape[0]
    assert taps - 1 <= halo and s_len % tm == 0 and tm % halo == 0 and halo % SUBLANES == 0
    hb = tm // halo
    vec = pl.BlockSpec((1, ch), lambda i: (0, 0))
    return pl.pallas_call(
        functools.partial(_conv_out_prompt_body, tiles_per_seq=s_len // tm),
        grid=(t // tm,),
        in_specs=[
            pl.BlockSpec((tm, ch), lambda i: (i, 0)),
            pl.BlockSpec((halo, ch), lambda i: (jnp.maximum(i * hb - 1, 0), 0)),
            pl.BlockSpec((tm, aw), lambda i: (i, 0)),
            pl.BlockSpec((tm, d), lambda i: (i, 0)),
            pl.BlockSpec((taps, ch), lambda i: (0, 0)),
            vec, vec, vec,
            pl.BlockSpec((aw + ch, d), lambda i: (0, 0)),
        ],
        out_specs=pl.BlockSpec((tm, d), lambda i: (i, 0)),
        out_shape=jax.ShapeDtypeStruct((t, d), F32),
        scratch_shapes=[pltpu.VMEM((halo + tm, ch), F32),
                        pltpu.VMEM((SUBLANES - 1, halo + tm - SUBLANES, ch), F32),
                        pltpu.VMEM((taps, SUBLANES, ch), F32),
                        pltpu.VMEM((tm, ch), BF16)],
        compiler_params=pltpu.CompilerParams(
            dimension_semantics=("arbitrary",), vmem_limit_bytes=VMEM_LIMIT),
        name="conv_out_prompt",
    )(u, u, attn, h, dw_w, dw_b, ln_g, ln_b, wo)


def _conv_out_sample_body(u_ref, st_ref, attn_ref, h_ref, w_ref, b_ref, g_ref, beta_ref, wo_ref,
                          o_ref, ns_ref, ext_ref, conv_ref, at_ref):
    n_state, batch, ch = st_ref.shape
    t_len = u_ref.shape[0]
    taps = w_ref.shape[0]
    ext_ref[0:n_state] = st_ref[...]
    ext_ref[n_state:n_state + t_len] = u_ref[...]
    ns_ref[...] = ext_ref[t_len:t_len + n_state]
    for t in range(t_len):
        acc = jnp.zeros((batch, ch), F32)
        for w in range(taps):
            acc = acc + ext_ref[t + w] * w_ref[w:w + 1, :]
        act = _ln_swish(acc + b_ref[...], g_ref[...], beta_ref[...])
        conv_ref[t * batch:(t + 1) * batch, :] = act.astype(BF16)
        at_ref[t * batch:(t + 1) * batch, :] = attn_ref[t].astype(BF16)
    aw = attn_ref.shape[2]
    proj = (jnp.dot(at_ref[...], wo_ref[0:aw, :], preferred_element_type=F32)
            + jnp.dot(conv_ref[...], wo_ref[aw:aw + ch, :], preferred_element_type=F32))
    for t in range(t_len):
        o_ref[t] = h_ref[t] + proj[t * batch:(t + 1) * batch, :]


def _conv_out_sample(u_t, state_t, attn_t, h_t, dw_w, dw_b, ln_g, ln_b, wo):
    t_len, batch, ch = u_t.shape
    d = h_t.shape[2]
    n_state = state_t.shape[0]
    assert dw_w.shape[0] == n_state + 1
    return pl.pallas_call(
        _conv_out_sample_body,
        out_shape=[jax.ShapeDtypeStruct((t_len, batch, d), F32),
                   jax.ShapeDtypeStruct((n_state, batch, ch), F32)],
        scratch_shapes=[pltpu.VMEM((n_state + t_len, batch, ch), F32),
                        pltpu.VMEM((t_len * batch, ch), BF16),
                        pltpu.VMEM((t_len * batch, attn_t.shape[2]), BF16)],
        compiler_params=pltpu.CompilerParams(vmem_limit_bytes=VMEM_LIMIT),
        name="conv_out_sample",
    )(u_t, state_t, attn_t, h_t, dw_w, dw_b, ln_g, ln_b, wo)


def _rope_tables(positions):
    half = HEAD_DIM // 2
    inv = ROPE_THETA ** (-np.arange(half, dtype=np.float64) / half)
    ang = np.asarray(positions, np.float64)[:, None] * inv[None, :]
    cos, sin = np.cos(ang), np.sin(ang)
    reps = LANES // HEAD_DIM
    cos_t = np.tile(np.concatenate([cos, cos], axis=1), (1, reps))
    sin_t = np.tile(np.concatenate([-sin, sin], axis=1), (1, reps))
    return jnp.asarray(cos_t, F32), jnp.asarray(sin_t, F32)


def kernel(x_prompt, x_sample, cache_k, cache_v, state_conv, ln_ffn1, ffn1_w_gate, ffn1_w_up,
           ffn1_w_down, ln_mix, w_in, q_norm, k_norm, conv_dw_w, conv_dw_b, conv_ln_g, conv_ln_b,
           w_out, ln_ffn2, ffn2_w_gate, ffn2_w_up, ffn2_w_down):
    batch, s_len, d_model = x_prompt.shape
    dec_batch, t_len, _ = x_sample.shape
    depth, _, buf_len, n_heads, head_dim = cache_k.shape
    assert depth == 1 and head_dim == HEAD_DIM
    width = n_heads * head_dim
    ch = conv_dw_w.shape[2]

    xp = x_prompt.reshape(batch * s_len, d_model)
    xs = x_sample.reshape(dec_batch * t_len, d_model)
    ts = dec_batch * t_len

    reps = width // HEAD_DIM
    gq = jnp.tile(q_norm, (1, reps))
    gk = jnp.tile(k_norm, (1, reps))
    heads_per_tile = MXU_DIM // HEAD_DIM
    pm = jnp.asarray(np.kron(np.eye(heads_per_tile), np.ones((HEAD_DIM, HEAD_DIM))), BF16)

    tiles = INPROJ_TILES
    pos_p = (np.arange(batch * s_len) % s_len).reshape(tiles, -1)
    pos_s = np.tile(PAST_LEN + np.arange(t_len), dec_batch).reshape(tiles, -1)
    cos_t, sin_t = _rope_tables(np.concatenate([pos_p, pos_s], axis=1).reshape(-1))

    hp, hs, (wi, wo) = _ffn(xp, xs, ln_ffn1, ffn1_w_gate[0], ffn1_w_up[0], ffn1_w_down[0],
                            casts=(w_in[0], w_out[0]))

    (qp, kp, vp, up, kt_p, vt_p), (qs, ks, vs, us) = _inproj(
        hp, hs, ln_mix, wi, gq, gk, cos_t, sin_t, pm, seq_len=s_len, tiles=tiles)

    kt = jnp.transpose(cache_k[0], (0, 2, 3, 1)).reshape(dec_batch, width, buf_len)
    vt = jnp.transpose(cache_v[0], (0, 2, 3, 1)).reshape(dec_batch, width, buf_len)
    attn_p, attn_s = _attention(qp, kp, vp, batch, s_len, qs, ks, vs, kt, vt, t_len)

    hp = _conv_out_prompt(up, attn_p, hp, conv_dw_w[0], conv_dw_b, conv_ln_g, conv_ln_b, wo, s_len)
    state_t = jnp.transpose(state_conv[0], (1, 0, 2))
    time_major = lambda a: jnp.transpose(a.reshape(dec_batch, t_len, a.shape[-1]), (1, 0, 2))
    hs_t, new_state_t = _conv_out_sample(time_major(us), state_t, time_major(attn_s),
                                         time_major(hs), conv_dw_w[0], conv_dw_b,
                                         conv_ln_g, conv_ln_b, wo)
    hs = jnp.transpose(hs_t, (1, 0, 2)).reshape(ts, d_model)

    yp, ys, _ = _ffn(hp, hs, ln_ffn2, ffn2_w_gate[0], ffn2_w_up[0], ffn2_w_down[0])

    n_state = state_conv.shape[2]
    seq_major = lambda a: jnp.transpose(a.reshape(batch, n_heads, head_dim, s_len), (0, 3, 1, 2))[None]
    kv_s = (1, dec_batch, t_len, n_heads, head_dim)
    return (yp.reshape(batch, s_len, d_model),
            ys.reshape(dec_batch, t_len, d_model),
            seq_major(kt_p), seq_major(vt_p),
            up.reshape(batch, s_len, ch)[:, s_len - n_state:][None],
            ks.reshape(kv_s), vs.reshape(kv_s),
            jnp.transpose(new_state_t, (1, 0, 2))[None])
```
